```python
import jax, jax.numpy as jnp
from jax import lax
import numpy as np

D_MODEL = 2048
BATCH = 2
SEQ = 4096
DEPTH = 1

N_META = 16
BLK = 128
WINDOW = 128
HEAD_DIM = 64
N_Q_HEADS = D_MODEL // 128
N_KV_HEADS = N_Q_HEADS // 4
Q_PER_KV = N_Q_HEADS // N_KV_HEADS
ATTN_DIM = N_Q_HEADS * HEAD_DIM
KV_DIM = N_KV_HEADS * HEAD_DIM
ROT_DIM = HEAD_DIM // 4
ROPE_THETA = 500000.0
CONV_DIM = D_MODEL // 2
CONV_K = 3
N_EXPERTS = 32
TOP_K = 4
D_FF = D_MODEL
SWIGLU_ALPHA = 1.702
SWIGLU_LIMIT = 7.0
MOE_BLK = 128
RMS_EPS = 1e-5
IN_SIZES = (ATTN_DIM, KV_DIM, KV_DIM, CONV_DIM, CONV_DIM, CONV_DIM, D_MODEL, D_MODEL)
IN_DIM = ATTN_DIM + 2 * KV_DIM + 3 * CONV_DIM + 2 * D_MODEL

kernel_name = "hybrid_gated_conv_swa_moe_encoder"


def rmsnorm(x, g):
    xf = x.astype(jnp.float32)
    y = xf * lax.rsqrt(jnp.mean(xf * xf, axis=-1, keepdims=True) + RMS_EPS)
    return (y * g.astype(jnp.float32)).astype(x.dtype)


def rope_partial(t, cos, sin):
    half = ROT_DIM // 2
    c = cos[None, :, None, :]
    s = sin[None, :, None, :]
    x1 = t[..., :half]
    x2 = t[..., half:ROT_DIM]
    return jnp.concatenate([x1 * c - x2 * s, x2 * c + x1 * s, t[..., ROT_DIM:]], axis=-1)


def windowed_gqa(q, k, v, sink):
    b, l = q.shape[0], q.shape[1]
    pad = BLK - N_META
    lp = l + pad
    nb = lp // BLK
    qb = jnp.pad(q, ((0, 0), (pad, 0), (0, 0), (0, 0))).reshape(b, nb, BLK, N_KV_HEADS, Q_PER_KV, HEAD_DIM)
    kp = jnp.pad(k, ((0, 0), (pad + BLK, BLK), (0, 0), (0, 0))).reshape(b, nb + 2, BLK, N_KV_HEADS, HEAD_DIM)
    vp = jnp.pad(v, ((0, 0), (pad + BLK, BLK), (0, 0), (0, 0))).reshape(b, nb + 2, BLK, N_KV_HEADS, HEAD_DIM)
    band = lambda t: jnp.concatenate([t[:, :-2], t[:, 1:-1], t[:, 2:]], axis=2)
    kb, vb = band(kp), band(vp)
    km, vm = k[:, :N_META], v[:, :N_META]
    scale = HEAD_DIM ** -0.5
    s_band = jnp.einsum('bnqhgd,bnshd->bnhgqs', qb, kb).astype(jnp.float32) * scale
    s_meta = jnp.einsum('bnqhgd,bmhd->bnhgqm', qb, km).astype(jnp.float32) * scale
    qpos = jnp.arange(nb)[:, None] * BLK + jnp.arange(BLK)[None, :]
    kpos = (jnp.arange(nb)[:, None] - 1) * BLK + jnp.arange(3 * BLK)[None, :]
    kk = kpos[:, None, :]
    band_ok = (jnp.abs(qpos[:, :, None] - kk) <= WINDOW) & (kk >= BLK) & (kk < lp)
    s_band = jnp.where(band_ok[None, :, None, None], s_band, jnp.finfo(jnp.float32).min)
    sink_col = jnp.broadcast_to(
        sink.astype(jnp.float32).reshape(N_KV_HEADS, Q_PER_KV)[None, None, :, :, None, None],
        s_band.shape[:-1] + (1,))
    p = jax.nn.softmax(jnp.concatenate([s_meta, s_band, sink_col], axis=-1), axis=-1).astype(v.dtype)
    p_meta = p[..., :N_META]
    p_band = p[..., N_META:N_META + 3 * BLK]
    o = (jnp.einsum('bnhgqm,bmhd->bnqhgd', p_meta, vm)
         + jnp.einsum('bnhgqs,bnshd->bnqhgd', p_band, vb))
    return o.reshape(b, lp, ATTN_DIM)[:, pad:]


def mixer_block(h, w_in, conv_w, sink, w_attn_o, w_conv_o, w_out):
    b, l, _ = h.shape
    proj = h @ w_in
    idx = np.cumsum(IN_SIZES)[:-1].tolist()
    q, k, v, c_h, c_b, c_c, g_a, g_c = jnp.split(proj, idx, axis=-1)
    pos = jnp.arange(l, dtype=jnp.float32)
    inv_freq = ROPE_THETA ** (-jnp.arange(0, ROT_DIM, 2, dtype=jnp.float32) / ROT_DIM)
    ang = pos[:, None] * inv_freq[None, :]
    cos, sin = jnp.cos(ang).astype(h.dtype), jnp.sin(ang).astype(h.dtype)
    q = rope_partial(q.reshape(b, l, N_Q_HEADS, HEAD_DIM), cos, sin)
    k = rope_partial(k.reshape(b, l, N_KV_HEADS, HEAD_DIM), cos, sin)
    v = v.reshape(b, l, N_KV_HEADS, HEAD_DIM)
    y_attn = windowed_gqa(q, k, v, sink) @ w_attn_o
    u = c_c * c_h
    conv = lax.conv_general_dilated(
        u, conv_w[:, None, :], window_strides=(1,), padding=((CONV_K // 2, CONV_K // 2),),
        dimension_numbers=('NWC', 'WIO', 'NWC'), feature_group_count=CONV_DIM)
    y_conv = (c_b * conv) @ w_conv_o
    merged = jax.nn.sigmoid(g_a) * y_attn + jax.nn.sigmoid(g_c) * y_conv
    return merged @ w_out


def moe_ffn(h, router_w, router_b, w_gate_up, b_gate_up, w_down, b_down):
    t = h.shape[0]
    logits = (h @ router_w).astype(jnp.float32) + router_b.astype(jnp.float32)
    top_v, top_i = lax.top_k(logits, TOP_K)
    gates = jax.nn.softmax(top_v, axis=-1).astype(h.dtype)
    tk = t * TOP_K
    flat_e = top_i.reshape(-1)
    flat_tok = jnp.repeat(jnp.arange(t, dtype=jnp.int32), TOP_K)
    flat_g = gates.reshape(-1)
    order = jnp.argsort(flat_e)
    sorted_e = flat_e[order]
    counts = jnp.bincount(flat_e, length=N_EXPERTS)
    padded = (counts + MOE_BLK - 1) // MOE_BLK * MOE_BLK
    start = jnp.cumsum(counts) - counts
    cum_padded = jnp.cumsum(padded)
    start_padded = cum_padded - padded
    dest = start_padded[sorted_e] + (jnp.arange(tk) - start[sorted_e])
    n_blocks = (tk + N_EXPERTS * (MOE_BLK - 1) + MOE_BLK - 1) // MOE_BLK
    n_rows = n_blocks * MOE_BLK
    row_tok = jnp.full((n_rows,), t, jnp.int32).at[dest].set(flat_tok[order])
    row_gate = jnp.zeros((n_rows,), h.dtype).at[dest].set(flat_g[order])
    block_e = jnp.minimum(
        jnp.searchsorted(cum_padded, jnp.arange(n_blocks) * MOE_BLK, side='right'), N_EXPERTS - 1)
    h_pad = jnp.concatenate([h, jnp.zeros((1, h.shape[1]), h.dtype)], axis=0)
    xs = h_pad[row_tok].reshape(n_blocks, MOE_BLK, h.shape[1])

    def expert_block(args):
        xb, e = args
        gu = xb @ w_gate_up[e] + b_gate_up[e]
        gate = jnp.minimum(gu[:, :D_FF], SWIGLU_LIMIT)
        up = jnp.clip(gu[:, D_FF:], -SWIGLU_LIMIT, SWIGLU_LIMIT)
        act = (up + 1.0) * gate * jax.nn.sigmoid(SWIGLU_ALPHA * gate)
        return act @ w_down[e] + b_down[e]

    y_rows = lax.map(expert_block, (xs, block_e)).reshape(n_rows, h.shape[1])
    y = jax.ops.segment_sum(y_rows * row_gate[:, None], row_tok, num_segments=t + 1)
    return y[:t]


def setup_inputs(seed: int = 0) -> dict:
    key = jax.random.key(seed)
    ks = jax.random.split(key, 20)
    f32 = jnp.float32
    nrm = lambda k, shape, s: jax.random.normal(k, shape, f32) * s
    return {
        "x": nrm(ks[0], (BATCH, SEQ, D_MODEL), 1.0),
        "meta_tokens": nrm(ks[1], (N_META, D_MODEL), 1.0),
        "norm_mix_g": 1.0 + nrm(ks[2], (DEPTH, D_MODEL), 0.02),
        "w_in": nrm(ks[3], (DEPTH, D_MODEL, IN_DIM), D_MODEL ** -0.5),
        "conv_w": nrm(ks[4], (DEPTH, CONV_K, CONV_DIM), CONV_K ** -0.5),
        "sink": nrm(ks[5], (DEPTH, N_Q_HEADS), 0.5),
        "w_attn_o": nrm(ks[6], (DEPTH, ATTN_DIM, D_MODEL), ATTN_DIM ** -0.5),
        "w_conv_o": nrm(ks[7], (DEPTH, CONV_DIM, D_MODEL), CONV_DIM ** -0.5),
        "w_out": nrm(ks[8], (DEPTH, D_MODEL, D_MODEL), D_MODEL ** -0.5),
        "norm_ffn_g": 1.0 + nrm(ks[9], (DEPTH, D_MODEL), 0.02),
        "router_w": nrm(ks[10], (DEPTH, D_MODEL, N_EXPERTS), D_MODEL ** -0.5),
        "router_b": nrm(ks[11], (DEPTH, N_EXPERTS), 0.01),
        "w_gate_up": nrm(ks[12], (DEPTH, N_EXPERTS, D_MODEL, 2 * D_FF), D_MODEL ** -0.5),
        "b_gate_up": nrm(ks[13], (DEPTH, N_EXPERTS, 2 * D_FF), 0.02),
        "w_down": nrm(ks[14], (DEPTH, N_EXPERTS, D_FF, D_MODEL), D_FF ** -0.5),
        "b_down": nrm(ks[15], (DEPTH, N_EXPERTS, D_MODEL), 0.02),
        "norm_final_g": 1.0 + nrm(ks[16], (D_MODEL,), 0.02),
    }


def reference(x, meta_tokens, norm_mix_g, w_in, conv_w, sink, w_attn_o, w_conv_o, w_out,
              norm_ffn_g, router_w, router_b, w_gate_up, b_gate_up, w_down, b_down, norm_final_g):
    b = x.shape[0]
    meta = jnp.broadcast_to(meta_tokens.astype(x.dtype)[None], (b, N_META, x.shape[-1]))
    h = jnp.concatenate([meta, x], axis=1)
    l = h.shape[1]
    for i in range(DEPTH):
        h = h + mixer_block(rmsnorm(h, norm_mix_g[i]), w_in[i], conv_w[i], sink[i],
                            w_attn_o[i], w_conv_o[i], w_out[i])
        f = moe_ffn(rmsnorm(h, norm_ffn_g[i]).reshape(b * l, -1), router_w[i], router_b[i],
                    w_gate_up[i], b_gate_up[i], w_down[i], b_down[i])
        h = h + f.reshape(b, l, -1)
    return rmsnorm(h, norm_final_g)[:, N_META:]
```

```python
import functools

import numpy as np
import jax
import jax.numpy as jnp
from jax import lax
from jax.experimental import pallas as pl
from jax.experimental.pallas import tpu as pltpu

N_META = 16
BLK = 128
WINDOW = 128
HEAD_DIM = 64
Q_PER_KV = 4
ROT_DIM = HEAD_DIM // 4
ROPE_THETA = 500000.0
CONV_K = 3
TOP_K = 4
SWIGLU_ALPHA = 1.702
SWIGLU_LIMIT = 7.0
RMS_EPS = 1e-5

LANES = 128
SUBLANES = 8
VMEM_LIMIT_CAP = 60000 * 1024
MOE_ROW_TILE = 256
MOE_CHUNK_TILES = 5

F32 = jnp.float32
BF16 = jnp.bfloat16
I32 = jnp.int32


def _vmem_limit(nbytes):
    return int(min(VMEM_LIMIT_CAP, max(32 * 1024 * 1024, nbytes * 5 // 4 + (4 << 20))))


def _plan(b, seq, d, n_exp, d_ff):
    attn = (d // 128) * HEAD_DIM
    kvd = attn // Q_PER_KV
    conv = d // 2
    r = b * seq
    p = dict(b=b, seq=seq, d=d, n_exp=n_exp, d_ff=d_ff, attn=attn, kvd=kvd, conv=conv, r=r)
    p["nq"] = attn // HEAD_DIM
    p["nkv"] = kvd // HEAD_DIM
    p["in_dim"] = attn + 2 * kvd + 3 * conv + 2 * d
    p["tn"] = 2 * kvd
    p["tm"] = min(1024, seq)
    p["n_qkv_tiles"] = (attn + 2 * kvd) // p["tn"]
    p["n_col_tiles"] = p["in_dim"] // p["tn"]
    p["rest_w"] = 3 * conv + 2 * d
    p["tc"] = min(256, seq)
    p["te"] = min(256, seq)
    p["tg"] = min(128, seq)
    p["tf"] = min(256, d_ff)
    p["nf"] = d_ff // p["tf"]
    p["rt"] = MOE_ROW_TILE
    p["tpc"] = MOE_CHUNK_TILES
    p["cap"] = MOE_ROW_TILE * MOE_CHUNK_TILES
    p["n_tiles_max"] = (r * TOP_K) // p["rt"] + n_exp
    p["n_chunks_max"] = n_exp + -(-p["n_tiles_max"] // p["tpc"])
    p["n_slots"] = p["n_tiles_max"] * p["rt"]
    assert seq % p["tm"] == 0 and seq % p["tc"] == 0 and seq % BLK == 0
    assert attn % p["tn"] == 0 and p["in_dim"] % p["tn"] == 0 and kvd % LANES == 0
    assert (r * TOP_K) % p["rt"] == 0 and d_ff % p["tf"] == 0
    assert p["rt"] & (p["rt"] - 1) == 0, "the zero-fill decomposition needs a power-of-two row tile"
    return p


def _rope_tables(seq):
    half = ROT_DIM // 2
    pos = jnp.arange(N_META + seq, dtype=F32)
    inv_freq = ROPE_THETA ** (-jnp.arange(0, ROT_DIM, 2, dtype=F32) / ROT_DIM)
    ang = pos[:, None] * inv_freq[None, :]
    cos, sin = jnp.cos(ang), jnp.sin(ang)
    lane = np.arange(LANES) % HEAD_DIM
    idx = np.where(lane < ROT_DIM, lane % half, 0)
    rot = jnp.asarray(lane < ROT_DIM)
    sign = jnp.asarray(np.where(lane < half, -1.0, 1.0).astype(np.float32))
    cos_l = jnp.where(rot[None, :], cos[:, idx], 1.0)
    sin_l = jnp.where(rot[None, :], sin[:, idx] * sign[None, :], 0.0)
    tab = jnp.stack([cos_l, sin_l]).astype(F32)
    return tab[:, N_META:], tab[:, :N_META]


def _rope(t, cos, sin, n_cols):
    lane = lax.broadcasted_iota(I32, (t.shape[0], LANES), 1)
    first = (lane % HEAD_DIM) < (ROT_DIM // 2)
    outs = []
    for c in range(t.shape[1] // LANES):
        s = t[:, c * LANES:(c + 1) * LANES]
        if c * LANES < n_cols:
            partner = jnp.where(first, pltpu.roll(s, LANES - ROT_DIM // 2, 1), pltpu.roll(s, ROT_DIM // 2, 1))
            s = s * cos + partner * sin
        outs.append(s)
    return jnp.concatenate(outs, axis=1)


def _inproj_kernel(x_ref, meta_ref, g_ref, w_ref, csx_ref, csm_ref,
                   qkv_ref, rest_ref, qkvm_ref, restm_ref, hn_ref, wb_ref, *, p):
    tm, tn, kvd = p["tm"], p["tn"], p["kvd"]
    n_q_tiles = p["attn"] // tn
    n = pl.program_id(1)

    @pl.when(n == 0)
    def _():
        g = g_ref[...]

        def norm(v):
            ms = jnp.mean(v * v, axis=-1, keepdims=True)
            return ((v * lax.rsqrt(ms + RMS_EPS)) * g).astype(BF16)

        rows = min(128, tm)

        def body(i, _):
            r0 = pl.multiple_of(i * rows, rows)
            hn_ref[pl.ds(r0, rows), :] = norm(x_ref[pl.ds(r0, rows), :])
            return 0

        lax.fori_loop(0, tm // rows, body, 0)
        hn_ref[tm:tm + N_META, :] = norm(meta_ref[...])

    wb_ref[...] = w_ref[...].astype(BF16)
    res = jnp.dot(hn_ref[...], wb_ref[...], preferred_element_type=F32)
    res_x, res_m = res[:tm], res[tm:]

    def store_qkv(n_cols):
        if n_cols:
            rx = _rope(res_x, csx_ref[0], csx_ref[1], n_cols)
            rm = _rope(res_m, csm_ref[0], csm_ref[1], n_cols)
        else:
            rx, rm = res_x, res_m
        qkv_ref[...] = rx.astype(BF16)
        qkvm_ref[0] = rm.astype(BF16)

    @pl.when(n < n_q_tiles)
    def _():
        store_qkv(tn)

    @pl.when(n == n_q_tiles)
    def _():
        store_qkv(kvd)

    @pl.when(n >= p["n_qkv_tiles"])
    def _():
        rest_ref[...] = res_x.astype(BF16)
        restm_ref[0] = res_m.astype(BF16)


def _inproj(x2, meta, g, w_in, p):
    tm, tn, d, r = p["tm"], p["tn"], p["d"], p["r"]
    nm, nn, nqkv = r // tm, p["n_col_tiles"], p["n_qkv_tiles"]
    csx, csm = _rope_tables(p["seq"])
    spt = p["seq"] // tm
    qkv_w = nqkv * tn
    est = (2 * tm * d * 4 + (tm + 16) * d * 2 + 2 * d * tn * 4 + d * tn * 2 + 4 * tm * tn * 2
           + 4 * 2 * tm * LANES * 4 + 4 * (tm + 16) * tn * 4)
    return pl.pallas_call(
        functools.partial(_inproj_kernel, p=p),
        grid=(nm, nn),
        in_specs=[
            pl.BlockSpec((tm, d), lambda m, n: (m, 0)),
            pl.BlockSpec((N_META, d), lambda m, n: (0, 0)),
            pl.BlockSpec((1, d), lambda m, n: (0, 0)),
            pl.BlockSpec((d, tn), lambda m, n: (0, n)),
            pl.BlockSpec((2, tm, LANES), lambda m, n: (0, m % spt, 0)),
            pl.BlockSpec((2, N_META, LANES), lambda m, n: (0, 0, 0)),
        ],
        out_specs=[
            pl.BlockSpec((tm, tn), lambda m, n: (m, jnp.minimum(n, nqkv - 1))),
            pl.BlockSpec((tm, tn), lambda m, n: (m, jnp.maximum(n - nqkv, 0))),
            pl.BlockSpec((1, N_META, tn), lambda m, n: (m, 0, jnp.minimum(n, nqkv - 1))),
            pl.BlockSpec((1, N_META, tn), lambda m, n: (m, 0, jnp.maximum(n - nqkv, 0))),
        ],
        out_shape=[
            jax.ShapeDtypeStruct((r, qkv_w), BF16),
            jax.ShapeDtypeStruct((r, p["rest_w"]), BF16),
            jax.ShapeDtypeStruct((nm, N_META, qkv_w), BF16),
            jax.ShapeDtypeStruct((nm, N_META, p["rest_w"]), BF16),
        ],
        scratch_shapes=[pltpu.VMEM((tm + N_META, d), BF16), pltpu.VMEM((d, tn), BF16)],
        compiler_params=pltpu.CompilerParams(
            dimension_semantics=("arbitrary", "arbitrary"), vmem_limit_bytes=_vmem_limit(est)),
        name="inproj",
    )(x2, meta, g, w_in, csx, csm)


def _attn_kernel(sink_ref, q_ref, k0_ref, k1_ref, k2_ref, v0_ref, v1_ref, v2_ref, km_ref, vm_ref,
                 o_ref, kcat_ref, vcat_ref, *, p):
    seq, nkv = p["seq"], p["nkv"]
    nband = 3 * BLK
    nkeys = nband + N_META
    n = pl.program_id(1)
    for j, (kr, vr) in enumerate(((k0_ref, v0_ref), (k1_ref, v1_ref), (k2_ref, v2_ref))):
        kcat_ref[j * BLK:(j + 1) * BLK, :] = kr[...]
        vcat_ref[j * BLK:(j + 1) * BLK, :] = vr[...]
    kcat_ref[nband:nkeys, :] = km_ref[...]
    vcat_ref[nband:nkeys, :] = vm_ref[...]

    qi = lax.broadcasted_iota(I32, (BLK, nkeys), 0)
    sj = lax.broadcasted_iota(I32, (BLK, nkeys), 1)
    kx = (n - 1) * BLK + sj
    dq = n * BLK + qi - kx
    visible = (sj >= nband) | ((jnp.abs(dq) <= WINDOW) & (kx >= 0) & (kx < seq))
    neg = jnp.finfo(F32).min
    scale = HEAD_DIM ** -0.5
    for h in range(nkv):
        kh = kcat_ref[:, h * HEAD_DIM:(h + 1) * HEAD_DIM]
        vh = vcat_ref[:, h * HEAD_DIM:(h + 1) * HEAD_DIM]
        for g in range(Q_PER_KV):
            hd = h * Q_PER_KV + g
            qh = q_ref[:, hd * HEAD_DIM:(hd + 1) * HEAD_DIM]
            s = lax.dot_general(qh, kh, (((1,), (1,)), ((), ())), preferred_element_type=F32) * scale
            s = jnp.where(visible, s, neg)
            snk = sink_ref[hd]
            m = jnp.maximum(jnp.max(s, axis=1, keepdims=True), snk)
            e = jnp.exp(s - m)
            denom = jnp.sum(e, axis=1, keepdims=True) + jnp.exp(snk - m)
            o = jnp.dot(e.astype(BF16), vh, preferred_element_type=F32) / denom
            o_ref[:, hd * HEAD_DIM:(hd + 1) * HEAD_DIM] = o.astype(BF16)


def _attention(qkv, qkvm, sink, p):
    attn, kvd, seq, b = p["attn"], p["kvd"], p["seq"], p["b"]
    nbx = seq // BLK
    kc = attn // kvd
    nkeys = 3 * BLK + N_META

    def kv_spec(off, col):
        return pl.BlockSpec((BLK, kvd), lambda bi, n: (bi * nbx + jnp.clip(n + off, 0, nbx - 1), col))

    return pl.pallas_call(
        functools.partial(_attn_kernel, p=p),
        grid=(b, nbx),
        in_specs=[
            pl.BlockSpec(memory_space=pltpu.SMEM),
            pl.BlockSpec((BLK, attn), lambda bi, n: (bi * nbx + n, 0)),
            kv_spec(-1, kc), kv_spec(0, kc), kv_spec(1, kc),
            kv_spec(-1, kc + 1), kv_spec(0, kc + 1), kv_spec(1, kc + 1),
            pl.BlockSpec((None, N_META, kvd), lambda bi, n: (0, 0, kc)),
            pl.BlockSpec((None, N_META, kvd), lambda bi, n: (0, 0, kc + 1)),
        ],
        out_specs=pl.BlockSpec((BLK, attn), lambda bi, n: (bi * nbx + n, 0)),
        out_shape=jax.ShapeDtypeStruct((p["r"], attn), BF16),
        scratch_shapes=[pltpu.VMEM((nkeys, kvd), BF16), pltpu.VMEM((nkeys, kvd), BF16)],
        compiler_params=pltpu.CompilerParams(dimension_semantics=("arbitrary", "arbitrary")),
        name="attn",
    )(sink, qkv, qkv, qkv, qkv, qkv, qkv, qkv, qkvm, qkvm)


def _lane_pack(cols, rows, dtype):
    lane = lax.broadcasted_iota(I32, (rows, LANES), 1)
    out = jnp.zeros((rows, LANES), dtype)
    for k, c in enumerate(cols):
        out = jnp.where(lane == k, c.astype(dtype), out)
    return out


def _mixer_kernel(attn_ref, rest_ref, prev_ref, next_ref, restm_ref, x_ref, cw_ref, wao_ref, wco_ref,
                  wout_ref, g_ref, rw_ref, rb_ref,
                  h1_ref, hn2_ref, ti_ref, gate_ref, rank_ref, cnt_ref, carry_ref, *, p):
    tc, conv, d, n_exp, seq = p["tc"], p["conv"], p["d"], p["n_exp"], p["seq"]
    i = pl.program_id(0)
    tiles_per_seq = seq // tc
    is_first = (i % tiles_per_seq) == 0
    is_last = (i % tiles_per_seq) == tiles_per_seq - 1
    o_ch, o_cb, o_cc, o_ga, o_gc = 0, conv, 2 * conv, 3 * conv, 3 * conv + d

    @pl.when(i == 0)
    def _():
        carry_ref[...] = jnp.zeros_like(carry_ref)

    def u_of(ref):
        return ref[:, o_cc:o_cc + conv].astype(F32) * ref[:, o_ch:o_ch + conv].astype(F32)

    u = u_of(rest_ref)
    last = N_META - 1
    u_prev = jnp.where(is_first, u_of(restm_ref)[last:last + 1], u_of(prev_ref)[last:last + 1])
    u_next = jnp.where(is_last, 0.0, u_of(next_ref)[0:1])
    row = lax.broadcasted_iota(I32, (tc, conv), 0)
    u_m1 = jnp.where(row == 0, u_prev, pltpu.roll(u, 1, 0))
    u_p1 = jnp.where(row == tc - 1, u_next, pltpu.roll(u, tc - 1, 0))
    cw = cw_ref[...]
    cv = u_m1 * cw[0:1] + u * cw[1:2] + u_p1 * cw[2:3]
    yc_in = (rest_ref[:, o_cb:o_cb + conv].astype(F32) * cv).astype(BF16)
    y_conv = jnp.dot(yc_in, wco_ref[...], preferred_element_type=F32)
    y_attn = jnp.dot(attn_ref[...], wao_ref[...], preferred_element_type=F32)
    g_a = rest_ref[:, o_ga:o_ga + d].astype(F32)
    g_c = rest_ref[:, o_gc:o_gc + d].astype(F32)
    merged = jax.nn.sigmoid(g_a) * y_attn + jax.nn.sigmoid(g_c) * y_conv
    h1 = x_ref[...] + jnp.dot(merged.astype(BF16), wout_ref[...], preferred_element_type=F32)
    h1_ref[...] = h1
    ms = jnp.mean(h1 * h1, axis=-1, keepdims=True)
    hn2 = (h1 * lax.rsqrt(ms + RMS_EPS)) * g_ref[...]
    hn2_ref[...] = hn2

    logits = jnp.dot(hn2.astype(BF16), rw_ref[...], preferred_element_type=F32) + rb_ref[...]
    lane = lax.broadcasted_iota(I32, (tc, n_exp), 1).astype(F32)
    sel = jnp.zeros((tc, n_exp), F32)
    tv, ti = [], []
    cur = logits
    for _ in range(TOP_K):
        m = jnp.max(cur, axis=1, keepdims=True)
        idx = jnp.min(jnp.where(cur == m, lane, float(n_exp)), axis=1, keepdims=True)
        hit = lane == idx
        tv.append(m)
        ti.append(idx)
        sel = jnp.where(hit, 1.0, sel)
        cur = jnp.where(hit, -jnp.inf, cur)
    ex = [jnp.exp(v - tv[0]) for v in tv]
    tot = ex[0] + ex[1] + ex[2] + ex[3]
    gates = [e / tot for e in ex]

    r_i = lax.broadcasted_iota(I32, (tc, tc), 0)
    c_i = lax.broadcasted_iota(I32, (tc, tc), 1)
    lower = jnp.where(r_i > c_i, 1.0, 0.0).astype(BF16)
    before = jnp.dot(lower, sel.astype(BF16), preferred_element_type=F32) + carry_ref[0:1, 0:n_exp]
    ranks = [jnp.sum(jnp.where(lane == t, before, 0.0), axis=1, keepdims=True) for t in ti]
    carry_ref[0:1, 0:n_exp] = carry_ref[0:1, 0:n_exp] + jnp.sum(sel, axis=0, keepdims=True)

    ti_ref[...] = _lane_pack(ti, tc, I32)
    gate_ref[...] = _lane_pack(gates, tc, F32)
    rank_ref[...] = _lane_pack(ranks, tc, I32)
    cnt_ref[...] = carry_ref[...]


def _mixer(attn_o, rest, restm, x2, conv_w, wao, wco, wout, g_ffn, router_w, router_b, p):
    tc, d, r, rw, n_exp = p["tc"], p["d"], p["r"], p["rest_w"], p["n_exp"]
    nt = r // tc
    sub = tc // N_META
    n16 = r // N_META
    const = lambda shape: pl.BlockSpec(shape, lambda i: (0,) * len(shape))
    est = (2 * (tc * p["attn"] * 2 + tc * rw * 2 + tc * d * 4 + 3 * N_META * rw * 2)
           + 2 * (p["attn"] * d + p["conv"] * d + d * d + d * n_exp) * 2
           + 2 * (2 * tc * d * 4 + 3 * tc * LANES * 4) + 10 * tc * d * 4)
    return pl.pallas_call(
        functools.partial(_mixer_kernel, p=p),
        grid=(nt,),
        in_specs=[
            pl.BlockSpec((tc, p["attn"]), lambda i: (i, 0)),
            pl.BlockSpec((tc, rw), lambda i: (i, 0)),
            pl.BlockSpec((N_META, rw), lambda i: (jnp.maximum(i * sub - 1, 0), 0)),
            pl.BlockSpec((N_META, rw), lambda i: (jnp.minimum((i + 1) * sub, n16 - 1), 0)),
            pl.BlockSpec((None, N_META, rw), lambda i: (0, 0, 0)),
            pl.BlockSpec((tc, d), lambda i: (i, 0)),
            const((CONV_K, p["conv"])),
            const((p["attn"], d)), const((p["conv"], d)), const((d, d)),
            const((1, d)), const((d, n_exp)), const((1, n_exp)),
        ],
        out_specs=[
            pl.BlockSpec((tc, d), lambda i: (i, 0)),
            pl.BlockSpec((tc, d), lambda i: (i, 0)),
            pl.BlockSpec((tc, LANES), lambda i: (i, 0)),
            pl.BlockSpec((tc, LANES), lambda i: (i, 0)),
            pl.BlockSpec((tc, LANES), lambda i: (i, 0)),
            pl.BlockSpec((8, LANES), lambda i: (0, 0)),
        ],
        out_shape=[
            jax.ShapeDtypeStruct((r, d), F32),
            jax.ShapeDtypeStruct((r, d), F32),
            jax.ShapeDtypeStruct((r, LANES), I32),
            jax.ShapeDtypeStruct((r, LANES), F32),
            jax.ShapeDtypeStruct((r, LANES), I32),
            jax.ShapeDtypeStruct((8, LANES), F32),
        ],
        scratch_shapes=[pltpu.VMEM((8, LANES), F32)],
        compiler_params=pltpu.CompilerParams(
            dimension_semantics=("arbitrary",), vmem_limit_bytes=_vmem_limit(est)),
        name="mixer",
    )(attn_o, rest, rest, rest, restm, x2, conv_w, wao, wco, wout, g_ffn, router_w, router_b)


def _dest_kernel(ti_ref, rank_ref, start_ref, dest_ref):
    rows = ti_ref.shape[0]
    lane = lax.broadcasted_iota(I32, (rows, LANES), 1).astype(F32)
    ti = ti_ref[...].astype(F32)
    rank = rank_ref[...].astype(F32)
    start = start_ref[0:1, :].astype(F32)
    out = jnp.zeros((rows, LANES), F32)
    for k in range(TOP_K):
        e_k = jnp.sum(jnp.where(lane == k, ti, 0.0), axis=1, keepdims=True)
        r_k = jnp.sum(jnp.where(lane == k, rank, 0.0), axis=1, keepdims=True)
        s_k = jnp.sum(jnp.where(lane == e_k, start, 0.0), axis=1, keepdims=True)
        out = jnp.where(lane == k, s_k + r_k, out)
    dest_ref[...] = out.astype(I32)


def _dest(ti, rank, row_start, p):
    r = p["r"]
    rows = min(1024, r)
    return pl.pallas_call(
        _dest_kernel,
        grid=(r // rows,),
        in_specs=[pl.BlockSpec((rows, LANES), lambda i: (i, 0)),
                  pl.BlockSpec((rows, LANES), lambda i: (i, 0)),
                  pl.BlockSpec((8, LANES), lambda i: (0, 0))],
        out_specs=pl.BlockSpec((rows, LANES), lambda i: (i, 0)),
        out_shape=jax.ShapeDtypeStruct((r, LANES), I32),
        name="dest",
    )(ti, rank, row_start)


def _routing_tables(counts, p):
    rt, tpc, n_exp, nch = p["rt"], p["tpc"], p["n_exp"], p["n_chunks_max"]
    cnt = counts[0, :n_exp].astype(I32)
    ntile = (cnt + rt - 1) // rt
    tile_start = jnp.cumsum(ntile) - ntile
    nchunk = (ntile + tpc - 1) // tpc
    chunk_end = jnp.cumsum(nchunk)
    c = jnp.arange(nch, dtype=I32)
    ce = jnp.minimum(jnp.sum((chunk_end[None, :] <= c[:, None]).astype(I32), axis=1), n_exp - 1)
    first = c - (chunk_end - nchunk)[ce]
    c_nt = jnp.clip(ntile[ce] - first * tpc, 0, tpc)
    c_nt = jnp.where(c < chunk_end[-1], c_nt, 0)
    c_ts = tile_start[ce] + first * tpc
    last_e = ce[jnp.maximum(chunk_end[-1] - 1, 0)]
    ce = jnp.where(c_nt > 0, ce, last_e)
    row_start = jnp.zeros((8, LANES), I32).at[0, :n_exp].set(tile_start * rt)
    used = jnp.sum(ntile).reshape(1).astype(I32)
    return row_start, cnt, tile_start * rt, ce.astype(I32), c_ts.astype(I32), c_nt.astype(I32), used


def _dispatch_kernel(cnt_ref, start_ref, dest_ref, hn_ref, xs_ref, zero_ref, sem, zsem, *, p):
    te, n_exp, rt, d = p["te"], p["n_exp"], p["rt"], p["d"]
    step = pl.program_id(0)

    @pl.when(step == 0)
    def _():
        zero_ref[...] = jnp.zeros_like(zero_ref)

        def zero_rows(dst0, size):
            cp = pltpu.make_async_copy(zero_ref.at[pl.ds(0, size)], xs_ref.at[pl.ds(dst0, size)], zsem)
            cp.start()
            cp.wait()

        def per_expert(e, _):
            cnt = cnt_ref[e]
            cur = start_ref[e] + cnt
            pad = (rt - (cnt & (rt - 1))) & (rt - 1)
            head = (SUBLANES - (cur & (SUBLANES - 1))) & (SUBLANES - 1)
            for j in range(SUBLANES - 1):
                @pl.when(j < head)
                def _(j=j):
                    zero_rows(cur + j, 1)
            cur = cur + head
            rem = pad - head
            size = SUBLANES
            while size < rt:
                @pl.when((rem & size) != 0)
                def _(cur=cur, size=size):
                    zero_rows(pl.multiple_of(cur, SUBLANES), size)
                cur = cur + (rem & size)
                size *= 2
            return 0

        lax.fori_loop(0, n_exp, per_expert, 0)

        half = rt // 2
        used = (start_ref[n_exp - 1] + cnt_ref[n_exp - 1] + rt - 1) // rt

        def tail_copy(t, j):
            r0 = pl.multiple_of(t * rt + j * half, half)
            return pltpu.make_async_copy(zero_ref, xs_ref.at[pl.ds(r0, half)], zsem)

        def tail_start(t, _):
            tail_copy(t, 0).start()
            tail_copy(t, 1).start()
            return 0

        def tail_wait(t, _):
            tail_copy(t, 0).wait()
            tail_copy(t, 1).wait()
            return 0

        lax.fori_loop(used, p["n_tiles_max"], tail_start, 0)
        lax.fori_loop(used, p["n_tiles_max"], tail_wait, 0)

    def row_copy(i, k):
        return pltpu.make_async_copy(hn_ref.at[pl.ds(i, 1)], xs_ref.at[pl.ds(dest_ref[i * TOP_K + k], 1)], sem)

    def issue(i, _):
        for k in range(TOP_K):
            row_copy(i, k).start()
        return 0

    lax.fori_loop(0, te, issue, 0)

    def drain(i, _):
        for k in range(TOP_K):
            row_copy(i, k).wait()
        return 0

    lax.fori_loop(0, te, drain, 0)


def _dispatch(hn2, dest_flat, cnt, start, p):
    te, d, r = p["te"], p["d"], p["r"]
    return pl.pallas_call(
        functools.partial(_dispatch_kernel, p=p),
        grid=(r // te,),
        in_specs=[
            pl.BlockSpec(memory_space=pltpu.SMEM),
            pl.BlockSpec(memory_space=pltpu.SMEM),
            pl.BlockSpec((te * TOP_K,), lambda i: (i,), memory_space=pltpu.SMEM),
            pl.BlockSpec((te, d), lambda i: (i, 0)),
        ],
        out_specs=pl.BlockSpec(memory_space=pl.ANY),
        out_shape=jax.ShapeDtypeStruct((p["n_slots"], d), F32),
        scratch_shapes=[pltpu.VMEM((p["rt"] // 2, d), F32), pltpu.SemaphoreType.DMA, pltpu.SemaphoreType.DMA],
        compiler_params=pltpu.CompilerParams(dimension_semantics=("arbitrary",)),
        name="dispatch",
    )(cnt, start, dest_flat, hn2)


def _moe_kernel(ce_ref, cts_ref, cnt_ref, used_ref, xs_ref, wg_ref, wu_ref, bg_ref, bu_ref, wd_ref, bd_ref,
                ys_ref, xbuf_ref, yacc_ref, stage_ref, wgb_ref, wub_ref, wdb_ref, sem, *, p):
    rt, nf = p["rt"], p["nf"]
    c = pl.program_id(0)
    f = pl.program_id(1)
    nt = cnt_ref[c]

    @pl.when((c == p["n_chunks_max"] - 1) & (f == nf - 1))
    def _():
        stage_ref[...] = jnp.zeros_like(stage_ref)

        def tail_copy(t):
            return pltpu.make_async_copy(stage_ref, ys_ref.at[pl.ds(pl.multiple_of(t * rt, rt), rt)], sem)

        def tail_start(t, _):
            tail_copy(t).start()
            return 0

        def tail_wait(t, _):
            tail_copy(t).wait()
            return 0

        lax.fori_loop(used_ref[0], p["n_tiles_max"], tail_start, 0)
        lax.fori_loop(used_ref[0], p["n_tiles_max"], tail_wait, 0)

    @pl.when(nt > 0)
    def _():
        row0 = pl.multiple_of(cts_ref[c] * rt, rt)

        @pl.when(f == 0)
        def _():
            def load(t, _):
                r0 = pl.multiple_of(t * rt, rt)
                cp = pltpu.make_async_copy(xs_ref.at[pl.ds(row0 + r0, rt)], stage_ref, sem)
                cp.start()
                cp.wait()
                xbuf_ref[pl.ds(r0, rt), :] = stage_ref[...].astype(BF16)
                return 0

            lax.fori_loop(0, nt, load, 0)

        wgb_ref[...] = wg_ref[0].astype(BF16)
        wub_ref[...] = wu_ref[0].astype(BF16)
        wdb_ref[...] = wd_ref[0].astype(BF16)
        bg = bg_ref[0]
        bu = bu_ref[0]
        bd = bd_ref[0]

        def sub(t, _):
            r0 = pl.multiple_of(t * rt, rt)
            xt = xbuf_ref[pl.ds(r0, rt), :]
            gate = jnp.dot(xt, wgb_ref[...], preferred_element_type=F32) + bg
            up = jnp.dot(xt, wub_ref[...], preferred_element_type=F32) + bu
            gate = jnp.minimum(gate, SWIGLU_LIMIT)
            up = jnp.clip(up, -SWIGLU_LIMIT, SWIGLU_LIMIT)
            act = (up + 1.0) * gate * jax.nn.sigmoid(SWIGLU_ALPHA * gate)
            part = jnp.dot(act.astype(BF16), wdb_ref[...], preferred_element_type=F32)

            @pl.when(f == 0)
            def _():
                yacc_ref[pl.ds(r0, rt), :] = part + bd

            @pl.when(f > 0)
            def _():
                yacc_ref[pl.ds(r0, rt), :] += part

            return 0

        lax.fori_loop(0, nt, sub, 0)

        @pl.when(f == nf - 1)
        def _():
            def store(t, _):
                r0 = pl.multiple_of(t * rt, rt)
                cp = pltpu.make_async_copy(yacc_ref.at[pl.ds(r0, rt)], ys_ref.at[pl.ds(row0 + r0, rt)], sem)
                cp.start()
                cp.wait()
                return 0

            lax.fori_loop(0, nt, store, 0)


def _moe(xs, ce, cts, cnt, used, w_gate_up, b_gate_up, w_down, b_down, p):
    d, tf, nf, rt, cap = p["d"], p["tf"], p["nf"], p["rt"], p["cap"]
    n_exp = p["n_exp"]
    bgu = b_gate_up.reshape(n_exp, 1, 2 * p["d_ff"])
    bdn = b_down.reshape(n_exp, 1, d)

    def fi(c, f, cnt_r):
        return jnp.where(cnt_r[c] > 0, f, nf - 1)

    est = (2 * (2 * d * tf * 4 + tf * d * 4) + 3 * d * tf * 2 + cap * d * 2 + cap * d * 4 + rt * d * 4
           + 6 * rt * d * 4)
    grid_spec = pltpu.PrefetchScalarGridSpec(
        num_scalar_prefetch=4,
        grid=(p["n_chunks_max"], nf),
        in_specs=[
            pl.BlockSpec(memory_space=pl.ANY),
            pl.BlockSpec((1, d, tf), lambda c, f, ce_r, cts_r, cnt_r, u_r: (ce_r[c], 0, fi(c, f, cnt_r))),
            pl.BlockSpec((1, d, tf), lambda c, f, ce_r, cts_r, cnt_r, u_r: (ce_r[c], 0, nf + fi(c, f, cnt_r))),
            pl.BlockSpec((1, 1, tf), lambda c, f, ce_r, cts_r, cnt_r, u_r: (ce_r[c], 0, fi(c, f, cnt_r))),
            pl.BlockSpec((1, 1, tf), lambda c, f, ce_r, cts_r, cnt_r, u_r: (ce_r[c], 0, nf + fi(c, f, cnt_r))),
            pl.BlockSpec((1, tf, d), lambda c, f, ce_r, cts_r, cnt_r, u_r: (ce_r[c], fi(c, f, cnt_r), 0)),
            pl.BlockSpec((1, 1, d), lambda c, f, ce_r, cts_r, cnt_r, u_r: (ce_r[c], 0, 0)),
        ],
        out_specs=pl.BlockSpec(memory_space=pl.ANY),
        scratch_shapes=[
            pltpu.VMEM((cap, d), BF16), pltpu.VMEM((cap, d), F32), pltpu.VMEM((rt, d), F32),
            pltpu.VMEM((d, tf), BF16), pltpu.VMEM((d, tf), BF16), pltpu.VMEM((tf, d), BF16),
            pltpu.SemaphoreType.DMA,
        ],
    )
    return pl.pallas_call(
        functools.partial(_moe_kernel, p=p),
        grid_spec=grid_spec,
        out_shape=jax.ShapeDtypeStruct((p["n_slots"], d), F32),
        compiler_params=pltpu.CompilerParams(
            dimension_semantics=("arbitrary", "arbitrary"), vmem_limit_bytes=_vmem_limit(est)),
        name="moe",
    )(ce, cts, cnt, used, xs, w_gate_up, w_gate_up, bgu, bgu, w_down, bdn)


def _combine_kernel(dest_ref, h1_ref, gate_ref, g_ref, ys_ref, o_ref, ybuf_ref, sem, *, p):
    tg = p["tg"]

    def row_copy(i, k):
        return pltpu.make_async_copy(ys_ref.at[pl.ds(dest_ref[i * TOP_K + k], 1)],
                                     ybuf_ref.at[k, pl.ds(i, 1)], sem)

    def issue(i, _):
        for k in range(TOP_K):
            row_copy(i, k).start()
        return 0

    lax.fori_loop(0, tg, issue, 0)

    def drain(i, _):
        for k in range(TOP_K):
            row_copy(i, k).wait()
        return 0

    lax.fori_loop(0, tg, drain, 0)

    lane = lax.broadcasted_iota(I32, (tg, LANES), 1)
    gates = gate_ref[...]
    h = h1_ref[...]
    for k in range(TOP_K):
        g_k = jnp.sum(jnp.where(lane == k, gates, 0.0), axis=1, keepdims=True)
        h = h + ybuf_ref[k] * g_k
    ms = jnp.mean(h * h, axis=-1, keepdims=True)
    o_ref[...] = (h * lax.rsqrt(ms + RMS_EPS)) * g_ref[...]


def _combine(ys, dest_flat, h1, gates, g_final, p):
    tg, d, r = p["tg"], p["d"], p["r"]
    return pl.pallas_call(
        functools.partial(_combine_kernel, p=p),
        grid=(r // tg,),
        in_specs=[
            pl.BlockSpec((tg * TOP_K,), lambda i: (i,), memory_space=pltpu.SMEM),
            pl.BlockSpec((tg, d), lambda i: (i, 0)),
            pl.BlockSpec((tg, LANES), lambda i: (i, 0)),
            pl.BlockSpec((1, d), lambda i: (0, 0)),
            pl.BlockSpec(memory_space=pl.ANY),
        ],
        out_specs=pl.BlockSpec((tg, d), lambda i: (i, 0)),
        out_shape=jax.ShapeDtypeStruct((r, d), F32),
        scratch_shapes=[pltpu.VMEM((TOP_K, tg, d), F32), pltpu.SemaphoreType.DMA],
        compiler_params=pltpu.CompilerParams(dimension_semantics=("arbitrary",)),
        name="combine",
    )(dest_flat, h1, gates, g_final, ys)


def kernel(x, meta_tokens, norm_mix_g, w_in, conv_w, sink, w_attn_o, w_conv_o, w_out, norm_ffn_g,
           router_w, router_b, w_gate_up, b_gate_up, w_down, b_down, norm_final_g):
    b, seq, d = x.shape
    n_exp, d_ff = w_down.shape[1], w_down.shape[2]
    assert norm_mix_g.shape[0] == 1 and meta_tokens.shape[0] == N_META
    p = _plan(b, seq, d, n_exp, d_ff)
    x2 = x.reshape(p["r"], d)

    qkv, rest, qkvm, restm = _inproj(x2, meta_tokens.astype(F32), norm_mix_g[0].reshape(1, d), w_in[0], p)
    attn_o = _attention(qkv, qkvm, sink[0].astype(F32), p)
    h1, hn2, ti, gates, rank, counts = _mixer(
        attn_o, rest, restm, x2, conv_w[0], w_attn_o[0].astype(BF16), w_conv_o[0].astype(BF16),
        w_out[0].astype(BF16), norm_ffn_g[0].reshape(1, d), router_w[0].astype(BF16),
        router_b[0].reshape(1, n_exp), p)
    row_start, cnt, start, ce, cts, cnt_tiles, used = _routing_tables(counts, p)
    dest = _dest(ti, rank, row_start, p)
    dest_flat = dest[:, :TOP_K].reshape(-1)
    xs = _dispatch(hn2, dest_flat, cnt, start, p)
    ys = _moe(xs, ce, cts, cnt_tiles, used, w_gate_up[0], b_gate_up[0], w_down[0], b_down[0], p)
    out = _combine(ys, dest_flat, h1, gates, norm_final_g.reshape(1, d), p)
    return out.reshape(b, seq, d)
```

```python
import functools

import numpy as np
import jax
import jax.numpy as jnp
from jax import lax
from jax.experimental import pallas as pl
from jax.experimental.pallas import tpu as pltpu

N_META = 16
BLK = 128
WINDOW = 128
HEAD_DIM = 64
Q_PER_KV = 4
ROT_DIM = HEAD_DIM // 4
ROPE_THETA = 500000.0
CONV_K = 3
TOP_K = 4
SWIGLU_ALPHA = 1.702
SWIGLU_LIMIT = 7.0
RMS_EPS = 1e-5

LANES = 128
SUBLANES = 8
VMEM_LIMIT_CAP = 60000 * 1024
MOE_ROW_TILE = 256
MOE_CHUNK_TILES = 5

F32 = jnp.float32
BF16 = jnp.bfloat16
I32 = jnp.int32


def _vmem_limit(nbytes):
    return int(min(VMEM_LIMIT_CAP, max(32 * 1024 * 1024, nbytes * 5 // 4 + (4 << 20))))


def _plan(b, seq, d, n_exp, d_ff):
    attn = (d // 128) * HEAD_DIM
    kvd = attn // Q_PER_KV
    conv = d // 2
    r = b * seq
    p = dict(b=b, seq=seq, d=d, n_exp=n_exp, d_ff=d_ff, attn=attn, kvd=kvd, conv=conv, r=r)
    p["nq"] = attn // HEAD_DIM
    p["nkv"] = kvd // HEAD_DIM
    p["in_dim"] = attn + 2 * kvd + 3 * conv + 2 * d
    p["tn"] = 2 * kvd
    p["tm"] = min(1024, seq)
    p["n_qkv_tiles"] = (attn + 2 * kvd) // p["tn"]
    p["n_col_tiles"] = p["in_dim"] // p["tn"]
    p["rest_w"] = 3 * conv + 2 * d
    p["tc"] = min(256, seq)
    p["te"] = min(256, seq)
    p["tg"] = min(128, seq)
    p["tf"] = min(512, d_ff)
    p["nf"] = d_ff // p["tf"]
    p["rt"] = MOE_ROW_TILE
    p["tpc"] = MOE_CHUNK_TILES
    p["cap"] = MOE_ROW_TILE * MOE_CHUNK_TILES
    p["n_tiles_max"] = (r * TOP_K) // p["rt"] + n_exp
    p["n_chunks_max"] = n_exp + -(-p["n_tiles_max"] // p["tpc"])
    p["n_slots"] = p["n_tiles_max"] * p["rt"]
    assert seq % p["tm"] == 0 and seq % p["tc"] == 0 and seq % BLK == 0
    assert attn % p["tn"] == 0 and p["in_dim"] % p["tn"] == 0 and kvd % LANES == 0
    assert (r * TOP_K) % p["rt"] == 0 and d_ff % p["tf"] == 0
    assert p["rt"] & (p["rt"] - 1) == 0, "the zero-fill decomposition needs a power-of-two row tile"
    return p


def _rope_tables(seq):
    half = ROT_DIM // 2
    pos = jnp.arange(N_META + seq, dtype=F32)
    inv_freq = ROPE_THETA ** (-jnp.arange(0, ROT_DIM, 2, dtype=F32) / ROT_DIM)
    ang = pos[:, None] * inv_freq[None, :]
    cos, sin = jnp.cos(ang), jnp.sin(ang)
    lane = np.arange(LANES) % HEAD_DIM
    idx = np.where(lane < ROT_DIM, lane % half, 0)
    rot = jnp.asarray(lane < ROT_DIM)
    sign = jnp.asarray(np.where(lane < half, -1.0, 1.0).astype(np.float32))
    cos_l = jnp.where(rot[None, :], cos[:, idx], 1.0)
    sin_l = jnp.where(rot[None, :], sin[:, idx] * sign[None, :], 0.0)
    tab = jnp.stack([cos_l, sin_l]).astype(F32)
    return tab[:, N_META:], tab[:, :N_META]


def _rope(t, cos, sin, n_cols):
    lane = lax.broadcasted_iota(I32, (t.shape[0], LANES), 1)
    first = (lane % HEAD_DIM) < (ROT_DIM // 2)
    outs = []
    for c in range(t.shape[1] // LANES):
        s = t[:, c * LANES:(c + 1) * LANES]
        if c * LANES < n_cols:
            partner = jnp.where(first, pltpu.roll(s, LANES - ROT_DIM // 2, 1), pltpu.roll(s, ROT_DIM // 2, 1))
            s = s * cos + partner * sin
        outs.append(s)
    return jnp.concatenate(outs, axis=1)


def _inproj_kernel(x_ref, meta_ref, g_ref, w_ref, csx_ref, csm_ref,
                   qkv_ref, rest_ref, qkvm_ref, restm_ref, hn_ref, wb_ref, *, p):
    tm, tn, kvd = p["tm"], p["tn"], p["kvd"]
    n_q_tiles = p["attn"] // tn
    n = pl.program_id(1)

    @pl.when(n == 0)
    def _():
        g = g_ref[...]

        def norm(v):
            ms = jnp.mean(v * v, axis=-1, keepdims=True)
            return ((v * lax.rsqrt(ms + RMS_EPS)) * g).astype(BF16)

        rows = min(128, tm)

        def body(i, _):
            r0 = pl.multiple_of(i * rows, rows)
            hn_ref[pl.ds(r0, rows), :] = norm(x_ref[pl.ds(r0, rows), :])
            return 0

        lax.fori_loop(0, tm // rows, body, 0)
        hn_ref[tm:tm + N_META, :] = norm(meta_ref[...])

    wb_ref[...] = w_ref[...].astype(BF16)
    res = jnp.dot(hn_ref[...], wb_ref[...], preferred_element_type=F32)
    res_x, res_m = res[:tm], res[tm:]

    def store_qkv(n_cols):
        if n_cols:
            rx = _rope(res_x, csx_ref[0], csx_ref[1], n_cols)
            rm = _rope(res_m, csm_ref[0], csm_ref[1], n_cols)
        else:
            rx, rm = res_x, res_m
        qkv_ref[...] = rx.astype(BF16)
        qkvm_ref[0] = rm.astype(BF16)

    @pl.when(n < n_q_tiles)
    def _():
        store_qkv(tn)

    @pl.when(n == n_q_tiles)
    def _():
        store_qkv(kvd)

    @pl.when(n >= p["n_qkv_tiles"])
    def _():
        rest_ref[...] = res_x.astype(BF16)
        restm_ref[0] = res_m.astype(BF16)


def _inproj(x2, meta, g, w_in, p):
    tm, tn, d, r = p["tm"], p["tn"], p["d"], p["r"]
    nm, nn, nqkv = r // tm, p["n_col_tiles"], p["n_qkv_tiles"]
    csx, csm = _rope_tables(p["seq"])
    spt = p["seq"] // tm
    qkv_w = nqkv * tn
    est = (2 * tm * d * 4 + (tm + 16) * d * 2 + 2 * d * tn * 4 + d * tn * 2 + 4 * tm * tn * 2
           + 4 * 2 * tm * LANES * 4 + 4 * (tm + 16) * tn * 4)
    return pl.pallas_call(
        functools.partial(_inproj_kernel, p=p),
        grid=(nm, nn),
        in_specs=[
            pl.BlockSpec((tm, d), lambda m, n: (m, 0)),
            pl.BlockSpec((N_META, d), lambda m, n: (0, 0)),
            pl.BlockSpec((1, d), lambda m, n: (0, 0)),
            pl.BlockSpec((d, tn), lambda m, n: (0, n)),
            pl.BlockSpec((2, tm, LANES), lambda m, n: (0, m % spt, 0)),
            pl.BlockSpec((2, N_META, LANES), lambda m, n: (0, 0, 0)),
        ],
        out_specs=[
            pl.BlockSpec((tm, tn), lambda m, n: (m, jnp.minimum(n, nqkv - 1))),
            pl.BlockSpec((tm, tn), lambda m, n: (m, jnp.maximum(n - nqkv, 0))),
            pl.BlockSpec((1, N_META, tn), lambda m, n: (m, 0, jnp.minimum(n, nqkv - 1))),
            pl.BlockSpec((1, N_META, tn), lambda m, n: (m, 0, jnp.maximum(n - nqkv, 0))),
        ],
        out_shape=[
            jax.ShapeDtypeStruct((r, qkv_w), BF16),
            jax.ShapeDtypeStruct((r, p["rest_w"]), BF16),
            jax.ShapeDtypeStruct((nm, N_META, qkv_w), BF16),
            jax.ShapeDtypeStruct((nm, N_META, p["rest_w"]), BF16),
        ],
        scratch_shapes=[pltpu.VMEM((tm + N_META, d), BF16), pltpu.VMEM((d, tn), BF16)],
        compiler_params=pltpu.CompilerParams(
            dimension_semantics=("arbitrary", "arbitrary"), vmem_limit_bytes=_vmem_limit(est)),
        name="inproj",
    )(x2, meta, g, w_in, csx, csm)


def _attn_kernel(sink_ref, q_ref, k0_ref, k1_ref, k2_ref, v0_ref, v1_ref, v2_ref, km_ref, vm_ref,
                 o_ref, kcat_ref, vcat_ref, *, p):
    seq, nkv = p["seq"], p["nkv"]
    nband = 3 * BLK
    nkeys = nband + N_META
    n = pl.program_id(1)
    for j, (kr, vr) in enumerate(((k0_ref, v0_ref), (k1_ref, v1_ref), (k2_ref, v2_ref))):
        kcat_ref[j * BLK:(j + 1) * BLK, :] = kr[...]
        vcat_ref[j * BLK:(j + 1) * BLK, :] = vr[...]
    kcat_ref[nband:nkeys, :] = km_ref[...]
    vcat_ref[nband:nkeys, :] = vm_ref[...]

    qi = lax.broadcasted_iota(I32, (BLK, nkeys), 0)
    sj = lax.broadcasted_iota(I32, (BLK, nkeys), 1)
    kx = (n - 1) * BLK + sj
    dq = n * BLK + qi - kx
    visible = (sj >= nband) | ((jnp.abs(dq) <= WINDOW) & (kx >= 0) & (kx < seq))
    neg = jnp.finfo(F32).min
    scale = HEAD_DIM ** -0.5
    for h in range(nkv):
        kh = kcat_ref[:, h * HEAD_DIM:(h + 1) * HEAD_DIM]
        vh = vcat_ref[:, h * HEAD_DIM:(h + 1) * HEAD_DIM]
        for g in range(Q_PER_KV):
            hd = h * Q_PER_KV + g
            qh = q_ref[:, hd * HEAD_DIM:(hd + 1) * HEAD_DIM]
            s = lax.dot_general(qh, kh, (((1,), (1,)), ((), ())), preferred_element_type=F32) * scale
            s = jnp.where(visible, s, neg)
            snk = sink_ref[hd]
            m = jnp.maximum(jnp.max(s, axis=1, keepdims=True), snk)
            e = jnp.exp(s - m)
            denom = jnp.sum(e, axis=1, keepdims=True) + jnp.exp(snk - m)
            o = jnp.dot(e.astype(BF16), vh, preferred_element_type=F32) / denom
            o_ref[:, hd * HEAD_DIM:(hd + 1) * HEAD_DIM] = o.astype(BF16)


def _attention(qkv, qkvm, sink, p):
    attn, kvd, seq, b = p["attn"], p["kvd"], p["seq"], p["b"]
    nbx = seq // BLK
    kc = attn // kvd
    nkeys = 3 * BLK + N_META

    def kv_spec(off, col):
        return pl.BlockSpec((BLK, kvd), lambda bi, n: (bi * nbx + jnp.clip(n + off, 0, nbx - 1), col))

    return pl.pallas_call(
        functools.partial(_attn_kernel, p=p),
        grid=(b, nbx),
        in_specs=[
            pl.BlockSpec(memory_space=pltpu.SMEM),
            pl.BlockSpec((BLK, attn), lambda bi, n: (bi * nbx + n, 0)),
            kv_spec(-1, kc), kv_spec(0, kc), kv_spec(1, kc),
            kv_spec(-1, kc + 1), kv_spec(0, kc + 1), kv_spec(1, kc + 1),
            pl.BlockSpec((None, N_META, kvd), lambda bi, n: (0, 0, kc)),
            pl.BlockSpec((None, N_META, kvd), lambda bi, n: (0, 0, kc + 1)),
        ],
        out_specs=pl.BlockSpec((BLK, attn), lambda bi, n: (bi * nbx + n, 0)),
        out_shape=jax.ShapeDtypeStruct((p["r"], attn), BF16),
        scratch_shapes=[pltpu.VMEM((nkeys, kvd), BF16), pltpu.VMEM((nkeys, kvd), BF16)],
        compiler_params=pltpu.CompilerParams(dimension_semantics=("arbitrary", "arbitrary")),
        name="attn",
    )(sink, qkv, qkv, qkv, qkv, qkv, qkv, qkv, qkvm, qkvm)


def _lane_pack(cols, rows, dtype):
    lane = lax.broadcasted_iota(I32, (rows, LANES), 1)
    out = jnp.zeros((rows, LANES), dtype)
    for k, c in enumerate(cols):
        out = jnp.where(lane == k, c.astype(dtype), out)
    return out


def _mixer_kernel(attn_ref, rest_ref, prev_ref, next_ref, restm_ref, x_ref, cw_ref, wao_ref, wco_ref,
                  wout_ref, g_ref, rw_ref, rb_ref,
                  h1_ref, hn2_ref, ti_ref, gate_ref, rank_ref, cnt_ref, carry_ref, *, p):
    tc, conv, d, n_exp, seq = p["tc"], p["conv"], p["d"], p["n_exp"], p["seq"]
    i = pl.program_id(0)
    tiles_per_seq = seq // tc
    is_first = (i % tiles_per_seq) == 0
    is_last = (i % tiles_per_seq) == tiles_per_seq - 1
    o_ch, o_cb, o_cc, o_ga, o_gc = 0, conv, 2 * conv, 3 * conv, 3 * conv + d

    @pl.when(i == 0)
    def _():
        carry_ref[...] = jnp.zeros_like(carry_ref)

    def u_of(ref):
        return ref[:, o_cc:o_cc + conv].astype(F32) * ref[:, o_ch:o_ch + conv].astype(F32)

    u = u_of(rest_ref)
    last = N_META - 1
    u_prev = jnp.where(is_first, u_of(restm_ref)[last:last + 1], u_of(prev_ref)[last:last + 1])
    u_next = jnp.where(is_last, 0.0, u_of(next_ref)[0:1])
    row = lax.broadcasted_iota(I32, (tc, conv), 0)
    u_m1 = jnp.where(row == 0, u_prev, pltpu.roll(u, 1, 0))
    u_p1 = jnp.where(row == tc - 1, u_next, pltpu.roll(u, tc - 1, 0))
    cw = cw_ref[...]
    cv = u_m1 * cw[0:1] + u * cw[1:2] + u_p1 * cw[2:3]
    yc_in = (rest_ref[:, o_cb:o_cb + conv].astype(F32) * cv).astype(BF16)
    y_conv = jnp.dot(yc_in, wco_ref[...], preferred_element_type=F32)
    y_attn = jnp.dot(attn_ref[...], wao_ref[...], preferred_element_type=F32)
    g_a = rest_ref[:, o_ga:o_ga + d].astype(F32)
    g_c = rest_ref[:, o_gc:o_gc + d].astype(F32)
    merged = jax.nn.sigmoid(g_a) * y_attn + jax.nn.sigmoid(g_c) * y_conv
    h1 = x_ref[...] + jnp.dot(merged.astype(BF16), wout_ref[...], preferred_element_type=F32)
    h1_ref[...] = h1
    ms = jnp.mean(h1 * h1, axis=-1, keepdims=True)
    hn2 = (h1 * lax.rsqrt(ms + RMS_EPS)) * g_ref[...]
    hn2_ref[...] = hn2

    logits = jnp.dot(hn2.astype(BF16), rw_ref[...], preferred_element_type=F32) + rb_ref[...]
    lane = lax.broadcasted_iota(I32, (tc, n_exp), 1).astype(F32)
    sel = jnp.zeros((tc, n_exp), F32)
    tv, ti = [], []
    cur = logits
    for _ in range(TOP_K):
        m = jnp.max(cur, axis=1, keepdims=True)
        idx = jnp.min(jnp.where(cur == m, lane, float(n_exp)), axis=1, keepdims=True)
        hit = lane == idx
        tv.append(m)
        ti.append(idx)
        sel = jnp.where(hit, 1.0, sel)
        cur = jnp.where(hit, -jnp.inf, cur)
    ex = [jnp.exp(v - tv[0]) for v in tv]
    tot = ex[0] + ex[1] + ex[2] + ex[3]
    gates = [e / tot for e in ex]

    r_i = lax.broadcasted_iota(I32, (tc, tc), 0)
    c_i = lax.broadcasted_iota(I32, (tc, tc), 1)
    lower = jnp.where(r_i > c_i, 1.0, 0.0).astype(BF16)
    before = jnp.dot(lower, sel.astype(BF16), preferred_element_type=F32) + carry_ref[0:1, 0:n_exp]
    ranks = [jnp.sum(jnp.where(lane == t, before, 0.0), axis=1, keepdims=True) for t in ti]
    carry_ref[0:1, 0:n_exp] = carry_ref[0:1, 0:n_exp] + jnp.sum(sel, axis=0, keepdims=True)

    ti_ref[...] = _lane_pack(ti, tc, I32)
    gate_ref[...] = _lane_pack(gates, tc, F32)
    rank_ref[...] = _lane_pack(ranks, tc, I32)
    cnt_ref[...] = carry_ref[...]


def _mixer(attn_o, rest, restm, x2, conv_w, wao, wco, wout, g_ffn, router_w, router_b, p):
    tc, d, r, rw, n_exp = p["tc"], p["d"], p["r"], p["rest_w"], p["n_exp"]
    nt = r // tc
    sub = tc // N_META
    n16 = r // N_META
    const = lambda shape: pl.BlockSpec(shape, lambda i: (0,) * len(shape))
    est = (2 * (tc * p["attn"] * 2 + tc * rw * 2 + tc * d * 4 + 3 * N_META * rw * 2)
           + 2 * (p["attn"] * d + p["conv"] * d + d * d + d * n_exp) * 2
           + 2 * (2 * tc * d * 4 + 3 * tc * LANES * 4) + 10 * tc * d * 4)
    return pl.pallas_call(
        functools.partial(_mixer_kernel, p=p),
        grid=(nt,),
        in_specs=[
            pl.BlockSpec((tc, p["attn"]), lambda i: (i, 0)),
            pl.BlockSpec((tc, rw), lambda i: (i, 0)),
            pl.BlockSpec((N_META, rw), lambda i: (jnp.maximum(i * sub - 1, 0), 0)),
            pl.BlockSpec((N_META, rw), lambda i: (jnp.minimum((i + 1) * sub, n16 - 1), 0)),
            pl.BlockSpec((None, N_META, rw), lambda i: (0, 0, 0)),
            pl.BlockSpec((tc, d), lambda i: (i, 0)),
            const((CONV_K, p["conv"])),
            const((p["attn"], d)), const((p["conv"], d)), const((d, d)),
            const((1, d)), const((d, n_exp)), const((1, n_exp)),
        ],
        out_specs=[
            pl.BlockSpec((tc, d), lambda i: (i, 0)),
            pl.BlockSpec((tc, d), lambda i: (i, 0)),
            pl.BlockSpec((tc, LANES), lambda i: (i, 0)),
            pl.BlockSpec((tc, LANES), lambda i: (i, 0)),
            pl.BlockSpec((tc, LANES), lambda i: (i, 0)),
            pl.BlockSpec((8, LANES), lambda i: (0, 0)),
        ],
        out_shape=[
            jax.ShapeDtypeStruct((r, d), F32),
            jax.ShapeDtypeStruct((r, d), F32),
            jax.ShapeDtypeStruct((r, LANES), I32),
            jax.ShapeDtypeStruct((r, LANES), F32),
            jax.ShapeDtypeStruct((r, LANES), I32),
            jax.ShapeDtypeStruct((8, LANES), F32),
        ],
        scratch_shapes=[pltpu.VMEM((8, LANES), F32)],
        compiler_params=pltpu.CompilerParams(
            dimension_semantics=("arbitrary",), vmem_limit_bytes=_vmem_limit(est)),
        name="mixer",
    )(attn_o, rest, rest, rest, restm, x2, conv_w, wao, wco, wout, g_ffn, router_w, router_b)


def _dest_kernel(ti_ref, rank_ref, start_ref, dest_ref):
    rows = ti_ref.shape[0]
    lane = lax.broadcasted_iota(I32, (rows, LANES), 1).astype(F32)
    ti = ti_ref[...].astype(F32)
    rank = rank_ref[...].astype(F32)
    start = start_ref[0:1, :].astype(F32)
    out = jnp.zeros((rows, LANES), F32)
    for k in range(TOP_K):
        e_k = jnp.sum(jnp.where(lane == k, ti, 0.0), axis=1, keepdims=True)
        r_k = jnp.sum(jnp.where(lane == k, rank, 0.0), axis=1, keepdims=True)
        s_k = jnp.sum(jnp.where(lane == e_k, start, 0.0), axis=1, keepdims=True)
        out = jnp.where(lane == k, s_k + r_k, out)
    dest_ref[...] = out.astype(I32)


def _dest(ti, rank, row_start, p):
    r = p["r"]
    rows = min(1024, r)
    return pl.pallas_call(
        _dest_kernel,
        grid=(r // rows,),
        in_specs=[pl.BlockSpec((rows, LANES), lambda i: (i, 0)),
                  pl.BlockSpec((rows, LANES), lambda i: (i, 0)),
                  pl.BlockSpec((8, LANES), lambda i: (0, 0))],
        out_specs=pl.BlockSpec((rows, LANES), lambda i: (i, 0)),
        out_shape=jax.ShapeDtypeStruct((r, LANES), I32),
        name="dest",
    )(ti, rank, row_start)


def _routing_tables(counts, p):
    rt, tpc, n_exp, nch = p["rt"], p["tpc"], p["n_exp"], p["n_chunks_max"]
    cnt = counts[0, :n_exp].astype(I32)
    ntile = (cnt + rt - 1) // rt
    tile_start = jnp.cumsum(ntile) - ntile
    nchunk = (ntile + tpc - 1) // tpc
    chunk_end = jnp.cumsum(nchunk)
    c = jnp.arange(nch, dtype=I32)
    ce = jnp.minimum(jnp.sum((chunk_end[None, :] <= c[:, None]).astype(I32), axis=1), n_exp - 1)
    first = c - (chunk_end - nchunk)[ce]
    c_nt = jnp.clip(ntile[ce] - first * tpc, 0, tpc)
    c_nt = jnp.where(c < chunk_end[-1], c_nt, 0)
    c_ts = tile_start[ce] + first * tpc
    last_e = ce[jnp.maximum(chunk_end[-1] - 1, 0)]
    ce = jnp.where(c_nt > 0, ce, last_e)
    row_start = jnp.zeros((8, LANES), I32).at[0, :n_exp].set(tile_start * rt)
    used = jnp.sum(ntile).reshape(1).astype(I32)
    return row_start, cnt, tile_start * rt, ce.astype(I32), c_ts.astype(I32), c_nt.astype(I32), used


def _dispatch_kernel(cnt_ref, start_ref, dest_ref, hn_ref, xs_ref, zero_ref, sem, zsem, *, p):
    te, n_exp, rt, d = p["te"], p["n_exp"], p["rt"], p["d"]
    step = pl.program_id(0)

    @pl.when(step == 0)
    def _():
        zero_ref[...] = jnp.zeros_like(zero_ref)

        def zero_rows(dst0, size):
            cp = pltpu.make_async_copy(zero_ref.at[pl.ds(0, size)], xs_ref.at[pl.ds(dst0, size)], zsem)
            cp.start()
            cp.wait()

        def per_expert(e, _):
            cnt = cnt_ref[e]
            cur = start_ref[e] + cnt
            pad = (rt - (cnt & (rt - 1))) & (rt - 1)
            head = (SUBLANES - (cur & (SUBLANES - 1))) & (SUBLANES - 1)
            for j in range(SUBLANES - 1):
                @pl.when(j < head)
                def _(j=j):
                    zero_rows(cur + j, 1)
            cur = cur + head
            rem = pad - head
            size = SUBLANES
            while size < rt:
                @pl.when((rem & size) != 0)
                def _(cur=cur, size=size):
                    zero_rows(pl.multiple_of(cur, SUBLANES), size)
                cur = cur + (rem & size)
                size *= 2
            return 0

        lax.fori_loop(0, n_exp, per_expert, 0)

        half = rt // 2
        used = (start_ref[n_exp - 1] + cnt_ref[n_exp - 1] + rt - 1) // rt

        def tail_copy(t, j):
            r0 = pl.multiple_of(t * rt + j * half, half)
            return pltpu.make_async_copy(zero_ref, xs_ref.at[pl.ds(r0, half)], zsem)

        def tail_start(t, _):
            tail_copy(t, 0).start()
            tail_copy(t, 1).start()
            return 0

        def tail_wait(t, _):
            tail_copy(t, 0).wait()
            tail_copy(t, 1).wait()
            return 0

        lax.fori_loop(used, p["n_tiles_max"], tail_start, 0)
        lax.fori_loop(used, p["n_tiles_max"], tail_wait, 0)

    def row_copy(i, k):
        return pltpu.make_async_copy(hn_ref.at[pl.ds(i, 1)], xs_ref.at[pl.ds(dest_ref[i * TOP_K + k], 1)], sem)

    def issue(i, _):
        for k in range(TOP_K):
            row_copy(i, k).start()
        return 0

    lax.fori_loop(0, te, issue, 0)

    def drain(i, _):
        for k in range(TOP_K):
            row_copy(i, k).wait()
        return 0

    lax.fori_loop(0, te, drain, 0)


def _dispatch(hn2, dest_flat, cnt, start, p):
    te, d, r = p["te"], p["d"], p["r"]
    return pl.pallas_call(
        functools.partial(_dispatch_kernel, p=p),
        grid=(r // te,),
        in_specs=[
            pl.BlockSpec(memory_space=pltpu.SMEM),
            pl.BlockSpec(memory_space=pltpu.SMEM),
            pl.BlockSpec((te * TOP_K,), lambda i: (i,), memory_space=pltpu.SMEM),
            pl.BlockSpec((te, d), lambda i: (i, 0)),
        ],
        out_specs=pl.BlockSpec(memory_space=pl.ANY),
        out_shape=jax.ShapeDtypeStruct((p["n_slots"], d), F32),
        scratch_shapes=[pltpu.VMEM((p["rt"] // 2, d), F32), pltpu.SemaphoreType.DMA, pltpu.SemaphoreType.DMA],
        compiler_params=pltpu.CompilerParams(dimension_semantics=("arbitrary",)),
        name="dispatch",
    )(cnt, start, dest_flat, hn2)


def _moe_kernel(ce_ref, cts_ref, cnt_ref, used_ref, xs_ref, wg_ref, wu_ref, bg_ref, bu_ref, wd_ref, bd_ref,
                ys_ref, xbuf_ref, yacc_ref, stage_ref, wgb_ref, wub_ref, wdb_ref, sem, *, p):
    rt, nf = p["rt"], p["nf"]
    c = pl.program_id(0)
    f = pl.program_id(1)
    nt = cnt_ref[c]

    @pl.when((c == p["n_chunks_max"] - 1) & (f == nf - 1))
    def _():
        stage_ref[...] = jnp.zeros_like(stage_ref)

        def tail_copy(t):
            return pltpu.make_async_copy(stage_ref, ys_ref.at[pl.ds(pl.multiple_of(t * rt, rt), rt)], sem)

        def tail_start(t, _):
            tail_copy(t).start()
            return 0

        def tail_wait(t, _):
            tail_copy(t).wait()
            return 0

        lax.fori_loop(used_ref[0], p["n_tiles_max"], tail_start, 0)
        lax.fori_loop(used_ref[0], p["n_tiles_max"], tail_wait, 0)

    @pl.when(nt > 0)
    def _():
        row0 = pl.multiple_of(cts_ref[c] * rt, rt)

        @pl.when(f == 0)
        def _():
            def load(t, _):
                r0 = pl.multiple_of(t * rt, rt)
                cp = pltpu.make_async_copy(xs_ref.at[pl.ds(row0 + r0, rt)], stage_ref, sem)
                cp.start()
                cp.wait()
                xbuf_ref[pl.ds(r0, rt), :] = stage_ref[...].astype(BF16)
                return 0

            lax.fori_loop(0, nt, load, 0)

        wgb_ref[...] = wg_ref[0].astype(BF16)
        wub_ref[...] = wu_ref[0].astype(BF16)
        wdb_ref[...] = wd_ref[0].astype(BF16)
        bg = bg_ref[0]
        bu = bu_ref[0]
        bd = bd_ref[0]

        @pl.when(f == 0)
        def _():
            def init(t, _):
                r0 = pl.multiple_of(t * rt, rt)
                yacc_ref[pl.ds(r0, rt), :] = jnp.broadcast_to(bd, (rt, p["d"]))
                return 0

            lax.fori_loop(0, nt, init, 0)

        def sub(t):
            r0 = pl.multiple_of(t * rt, rt)
            xt = xbuf_ref[pl.ds(r0, rt), :]
            gate = jnp.dot(xt, wgb_ref[...], preferred_element_type=F32) + bg
            up = jnp.dot(xt, wub_ref[...], preferred_element_type=F32) + bu
            gate = jnp.minimum(gate, SWIGLU_LIMIT)
            up = jnp.clip(up, -SWIGLU_LIMIT, SWIGLU_LIMIT)
            act = (up + 1.0) * gate * jax.nn.sigmoid(SWIGLU_ALPHA * gate)
            yacc_ref[pl.ds(r0, rt), :] += jnp.dot(act.astype(BF16), wdb_ref[...], preferred_element_type=F32)

        def pair(i, _):
            sub(2 * i)
            sub(2 * i + 1)
            return 0

        lax.fori_loop(0, nt // 2, pair, 0)

        @pl.when(nt % 2 == 1)
        def _():
            sub(nt - 1)

        @pl.when(f == nf - 1)
        def _():
            def store(t, _):
                r0 = pl.multiple_of(t * rt, rt)
                cp = pltpu.make_async_copy(yacc_ref.at[pl.ds(r0, rt)], ys_ref.at[pl.ds(row0 + r0, rt)], sem)
                cp.start()
                cp.wait()
                return 0

            lax.fori_loop(0, nt, store, 0)


def _moe(xs, ce, cts, cnt, used, w_gate_up, b_gate_up, w_down, b_down, p):
    d, tf, nf, rt, cap = p["d"], p["tf"], p["nf"], p["rt"], p["cap"]
    n_exp = p["n_exp"]
    bgu = b_gate_up.reshape(n_exp, 1, 2 * p["d_ff"])
    bdn = b_down.reshape(n_exp, 1, d)

    def fi(c, f, cnt_r):
        return jnp.where(cnt_r[c] > 0, f, nf - 1)

    est = (2 * (2 * d * tf * 4 + tf * d * 4) + 3 * d * tf * 2 + cap * d * 2 + cap * d * 4 + rt * d * 4
           + 6 * rt * d * 4)
    grid_spec = pltpu.PrefetchScalarGridSpec(
        num_scalar_prefetch=4,
        grid=(p["n_chunks_max"], nf),
        in_specs=[
            pl.BlockSpec(memory_space=pl.ANY),
            pl.BlockSpec((1, d, tf), lambda c, f, ce_r, cts_r, cnt_r, u_r: (ce_r[c], 0, fi(c, f, cnt_r))),
            pl.BlockSpec((1, d, tf), lambda c, f, ce_r, cts_r, cnt_r, u_r: (ce_r[c], 0, nf + fi(c, f, cnt_r))),
            pl.BlockSpec((1, 1, tf), lambda c, f, ce_r, cts_r, cnt_r, u_r: (ce_r[c], 0, fi(c, f, cnt_r))),
            pl.BlockSpec((1, 1, tf), lambda c, f, ce_r, cts_r, cnt_r, u_r: (ce_r[c], 0, nf + fi(c, f, cnt_r))),
            pl.BlockSpec((1, tf, d), lambda c, f, ce_r, cts_r, cnt_r, u_r: (ce_r[c], fi(c, f, cnt_r), 0)),
            pl.BlockSpec((1, 1, d), lambda c, f, ce_r, cts_r, cnt_r, u_r: (ce_r[c], 0, 0)),
        ],
        out_specs=pl.BlockSpec(memory_space=pl.ANY),
        scratch_shapes=[
            pltpu.VMEM((cap, d), BF16), pltpu.VMEM((cap, d), F32), pltpu.VMEM((rt, d), F32),
            pltpu.VMEM((d, tf), BF16), pltpu.VMEM((d, tf), BF16), pltpu.VMEM((tf, d), BF16),
            pltpu.SemaphoreType.DMA,
        ],
    )
    return pl.pallas_call(
        functools.partial(_moe_kernel, p=p),
        grid_spec=grid_spec,
        out_shape=jax.ShapeDtypeStruct((p["n_slots"], d), F32),
        compiler_params=pltpu.CompilerParams(
            dimension_semantics=("arbitrary", "arbitrary"), vmem_limit_bytes=_vmem_limit(est)),
        name="moe",
    )(ce, cts, cnt, used, xs, w_gate_up, w_gate_up, bgu, bgu, w_down, bdn)


def _combine_kernel(dest_ref, h1_ref, gate_ref, g_ref, ys_ref, o_ref, ybuf_ref, sem, *, p):
    tg = p["tg"]

    def row_copy(i, k):
        return pltpu.make_async_copy(ys_ref.at[pl.ds(dest_ref[i * TOP_K + k], 1)],
                                     ybuf_ref.at[k, pl.ds(i, 1)], sem)

    def issue(i, _):
        for k in range(TOP_K):
            row_copy(i, k).start()
        return 0

    lax.fori_loop(0, tg, issue, 0)

    def drain(i, _):
        for k in range(TOP_K):
            row_copy(i, k).wait()
        return 0

    lax.fori_loop(0, tg, drain, 0)

    lane = lax.broadcasted_iota(I32, (tg, LANES), 1)
    gates = gate_ref[...]
    h = h1_ref[...]
    for k in range(TOP_K):
        g_k = jnp.sum(jnp.where(lane == k, gates, 0.0), axis=1, keepdims=True)
        h = h + ybuf_ref[k] * g_k
    ms = jnp.mean(h * h, axis=-1, keepdims=True)
    o_ref[...] = (h * lax.rsqrt(ms + RMS_EPS)) * g_ref[...]


def _combine(ys, dest_flat, h1, gates, g_final, p):
    tg, d, r = p["tg"], p["d"], p["r"]
    return pl.pallas_call(
        functools.partial(_combine_kernel, p=p),
        grid=(r // tg,),
        in_specs=[
            pl.BlockSpec((tg * TOP_K,), lambda i: (i,), memory_space=pltpu.SMEM),
            pl.BlockSpec((tg, d), lambda i: (i, 0)),
            pl.BlockSpec((tg, LANES), lambda i: (i, 0)),
            pl.BlockSpec((1, d), lambda i: (0, 0)),
            pl.BlockSpec(memory_space=pl.ANY),
        ],
        out_specs=pl.BlockSpec((tg, d), lambda i: (i, 0)),
        out_shape=jax.ShapeDtypeStruct((r, d), F32),
        scratch_shapes=[pltpu.VMEM((TOP_K, tg, d), F32), pltpu.SemaphoreType.DMA],
        compiler_params=pltpu.CompilerParams(dimension_semantics=("arbitrary",)),
        name="combine",
    )(dest_flat, h1, gates, g_final, ys)


def kernel(x, meta_tokens, norm_mix_g, w_in, conv_w, sink, w_attn_o, w_conv_o, w_out, norm_ffn_g,
           router_w, router_b, w_gate_up, b_gate_up, w_down, b_down, norm_final_g):
    b, seq, d = x.shape
    n_exp, d_ff = w_down.shape[1], w_down.shape[2]
    assert norm_mix_g.shape[0] == 1 and meta_tokens.shape[0] == N_META
    p = _plan(b, seq, d, n_exp, d_ff)
    x2 = x.reshape(p["r"], d)

    qkv, rest, qkvm, restm = _inproj(x2, meta_tokens.astype(F32), norm_mix_g[0].reshape(1, d), w_in[0], p)
    attn_o = _attention(qkv, qkvm, sink[0].astype(F32), p)
    h1, hn2, ti, gates, rank, counts = _mixer(
        attn_o, rest, restm, x2, conv_w[0], w_attn_o[0].astype(BF16), w_conv_o[0].astype(BF16),
        w_out[0].astype(BF16), norm_ffn_g[0].reshape(1, d), router_w[0].astype(BF16),
        router_b[0].reshape(1, n_exp), p)
    row_start, cnt, start, ce, cts, cnt_tiles, used = _routing_tables(counts, p)
    dest = _dest(ti, rank, row_start, p)
    dest_flat = dest[:, :TOP_K].reshape(-1)
    xs = _dispatch(hn2, dest_flat, cnt, start, p)
    ys = _moe(xs, ce, cts, cnt_tiles, used, w_gate_up[0], b_gate_up[0], w_down[0], b_down[0], p)
    out = _combine(ys, dest_flat, h1, gates, norm_final_g.reshape(1, d), p)
    return out.reshape(b, seq, d)
```

```python
import functools

import numpy as np
import jax
import jax.numpy as jnp
from jax import lax
from jax.experimental import pallas as pl
from jax.experimental.pallas import tpu as pltpu

N_META = 16
BLK = 128
WINDOW = 128
HEAD_DIM = 64
Q_PER_KV = 4
ROT_DIM = HEAD_DIM // 4
ROPE_THETA = 500000.0
CONV_K = 3
TOP_K = 4
SWIGLU_ALPHA = 1.702
SWIGLU_LIMIT = 7.0
RMS_EPS = 1e-5

LANES = 128
SUBLANES = 8
VMEM_LIMIT_CAP = 60000 * 1024
MOE_ROW_TILE = 256
MOE_CHUNK_TILES = 5

F32 = jnp.float32
BF16 = jnp.bfloat16
I32 = jnp.int32


def _vmem_limit(nbytes):
    return int(min(VMEM_LIMIT_CAP, max(32 * 1024 * 1024, nbytes * 5 // 4 + (4 << 20))))


def _plan(b, seq, d, n_exp, d_ff):
    attn = (d // 128) * HEAD_DIM
    kvd = attn // Q_PER_KV
    conv = d // 2
    r = b * seq
    p = dict(b=b, seq=seq, d=d, n_exp=n_exp, d_ff=d_ff, attn=attn, kvd=kvd, conv=conv, r=r)
    p["nq"] = attn // HEAD_DIM
    p["nkv"] = kvd // HEAD_DIM
    p["in_dim"] = attn + 2 * kvd + 3 * conv + 2 * d
    p["tn"] = 2 * kvd
    p["tm"] = min(1024, seq)
    p["n_qkv_tiles"] = (attn + 2 * kvd) // p["tn"]
    p["n_col_tiles"] = p["in_dim"] // p["tn"]
    p["rest_w"] = 3 * conv + 2 * d
    p["tc"] = min(256, seq)
    p["te"] = min(256, seq)
    p["tg"] = min(128, seq)
    p["tf"] = min(512, d_ff)
    p["nf"] = d_ff // p["tf"]
    p["rt"] = MOE_ROW_TILE
    p["tpc"] = MOE_CHUNK_TILES
    p["cap"] = MOE_ROW_TILE * MOE_CHUNK_TILES
    p["n_tiles_max"] = (r * TOP_K) // p["rt"] + n_exp
    p["n_chunks_max"] = n_exp + -(-p["n_tiles_max"] // p["tpc"])
    p["n_slots"] = p["n_tiles_max"] * p["rt"]
    assert seq % p["tm"] == 0 and seq % p["tc"] == 0 and seq % BLK == 0
    assert attn % p["tn"] == 0 and p["in_dim"] % p["tn"] == 0 and kvd % LANES == 0
    assert (r * TOP_K) % p["rt"] == 0 and d_ff % p["tf"] == 0
    assert p["rt"] & (p["rt"] - 1) == 0, "the zero-fill decomposition needs a power-of-two row tile"
    return p


def _rope_tables(seq):
    half = ROT_DIM // 2
    pos = jnp.arange(N_META + seq, dtype=F32)
    inv_freq = ROPE_THETA ** (-jnp.arange(0, ROT_DIM, 2, dtype=F32) / ROT_DIM)
    ang = pos[:, None] * inv_freq[None, :]
    cos, sin = jnp.cos(ang), jnp.sin(ang)
    lane = np.arange(LANES) % HEAD_DIM
    idx = np.where(lane < ROT_DIM, lane % half, 0)
    rot = jnp.asarray(lane < ROT_DIM)
    sign = jnp.asarray(np.where(lane < half, -1.0, 1.0).astype(np.float32))
    cos_l = jnp.where(rot[None, :], cos[:, idx], 1.0)
    sin_l = jnp.where(rot[None, :], sin[:, idx] * sign[None, :], 0.0)
    tab = jnp.stack([cos_l, sin_l]).astype(F32)
    return tab[:, N_META:], tab[:, :N_META]


def _rope(t, cos, sin, n_cols):
    lane = lax.broadcasted_iota(I32, (t.shape[0], LANES), 1)
    first = (lane % HEAD_DIM) < (ROT_DIM // 2)
    outs = []
    for c in range(t.shape[1] // LANES):
        s = t[:, c * LANES:(c + 1) * LANES]
        if c * LANES < n_cols:
            partner = jnp.where(first, pltpu.roll(s, LANES - ROT_DIM // 2, 1), pltpu.roll(s, ROT_DIM // 2, 1))
            s = s * cos + partner * sin
        outs.append(s)
    return jnp.concatenate(outs, axis=1)


def _inproj_kernel(x_ref, meta_ref, g_ref, w_ref, csx_ref, csm_ref,
                   qkv_ref, rest_ref, qkvm_ref, restm_ref, hn_ref, wb_ref, *, p):
    tm, tn, kvd = p["tm"], p["tn"], p["kvd"]
    n_q_tiles = p["attn"] // tn
    n = pl.program_id(1)

    @pl.when(n == 0)
    def _():
        g = g_ref[...]

        def norm(v):
            ms = jnp.mean(v * v, axis=-1, keepdims=True)
            return ((v * lax.rsqrt(ms + RMS_EPS)) * g).astype(BF16)

        rows = min(128, tm)

        def body(i, _):
            r0 = pl.multiple_of(i * rows, rows)
            hn_ref[pl.ds(r0, rows), :] = norm(x_ref[pl.ds(r0, rows), :])
            return 0

        lax.fori_loop(0, tm // rows, body, 0)
        hn_ref[tm:tm + N_META, :] = norm(meta_ref[...])

    wb_ref[...] = w_ref[...].astype(BF16)
    res = jnp.dot(hn_ref[...], wb_ref[...], preferred_element_type=F32)
    res_x, res_m = res[:tm], res[tm:]

    def store_qkv(n_cols):
        if n_cols:
            rx = _rope(res_x, csx_ref[0], csx_ref[1], n_cols)
            rm = _rope(res_m, csm_ref[0], csm_ref[1], n_cols)
        else:
            rx, rm = res_x, res_m
        qkv_ref[...] = rx.astype(BF16)
        qkvm_ref[0] = rm.astype(BF16)

    @pl.when(n < n_q_tiles)
    def _():
        store_qkv(tn)

    @pl.when(n == n_q_tiles)
    def _():
        store_qkv(kvd)

    @pl.when(n >= p["n_qkv_tiles"])
    def _():
        rest_ref[...] = res_x.astype(BF16)
        restm_ref[0] = res_m.astype(BF16)


def _inproj(x2, meta, g, w_in, p):
    tm, tn, d, r = p["tm"], p["tn"], p["d"], p["r"]
    nm, nn, nqkv = r // tm, p["n_col_tiles"], p["n_qkv_tiles"]
    csx, csm = _rope_tables(p["seq"])
    spt = p["seq"] // tm
    qkv_w = nqkv * tn
    est = (2 * tm * d * 4 + (tm + 16) * d * 2 + 2 * d * tn * 4 + d * tn * 2 + 4 * tm * tn * 2
           + 4 * 2 * tm * LANES * 4 + 4 * (tm + 16) * tn * 4)
    return pl.pallas_call(
        functools.partial(_inproj_kernel, p=p),
        grid=(nm, nn),
        in_specs=[
            pl.BlockSpec((tm, d), lambda m, n: (m, 0)),
            pl.BlockSpec((N_META, d), lambda m, n: (0, 0)),
            pl.BlockSpec((1, d), lambda m, n: (0, 0)),
            pl.BlockSpec((d, tn), lambda m, n: (0, n)),
            pl.BlockSpec((2, tm, LANES), lambda m, n: (0, m % spt, 0)),
            pl.BlockSpec((2, N_META, LANES), lambda m, n: (0, 0, 0)),
        ],
        out_specs=[
            pl.BlockSpec((tm, tn), lambda m, n: (m, jnp.minimum(n, nqkv - 1))),
            pl.BlockSpec((tm, tn), lambda m, n: (m, jnp.maximum(n - nqkv, 0))),
            pl.BlockSpec((1, N_META, tn), lambda m, n: (m, 0, jnp.minimum(n, nqkv - 1))),
            pl.BlockSpec((1, N_META, tn), lambda m, n: (m, 0, jnp.maximum(n - nqkv, 0))),
        ],
        out_shape=[
            jax.ShapeDtypeStruct((r, qkv_w), BF16),
            jax.ShapeDtypeStruct((r, p["rest_w"]), BF16),
            jax.ShapeDtypeStruct((nm, N_META, qkv_w), BF16),
            jax.ShapeDtypeStruct((nm, N_META, p["rest_w"]), BF16),
        ],
        scratch_shapes=[pltpu.VMEM((tm + N_META, d), BF16), pltpu.VMEM((d, tn), BF16)],
        compiler_params=pltpu.CompilerParams(
            dimension_semantics=("arbitrary", "arbitrary"), vmem_limit_bytes=_vmem_limit(est)),
        name="inproj",
    )(x2, meta, g, w_in, csx, csm)


def _attn_kernel(sink_ref, q_ref, k0_ref, k1_ref, k2_ref, v0_ref, v1_ref, v2_ref, km_ref, vm_ref,
                 o_ref, kcat_ref, vcat_ref, *, p):
    seq, nkv = p["seq"], p["nkv"]
    nband = 3 * BLK
    nkeys = nband + N_META
    n = pl.program_id(1)
    for j, (kr, vr) in enumerate(((k0_ref, v0_ref), (k1_ref, v1_ref), (k2_ref, v2_ref))):
        kcat_ref[j * BLK:(j + 1) * BLK, :] = kr[...]
        vcat_ref[j * BLK:(j + 1) * BLK, :] = vr[...]
    kcat_ref[nband:nkeys, :] = km_ref[...]
    vcat_ref[nband:nkeys, :] = vm_ref[...]

    qi = lax.broadcasted_iota(I32, (BLK, nkeys), 0)
    sj = lax.broadcasted_iota(I32, (BLK, nkeys), 1)
    kx = (n - 1) * BLK + sj
    dq = n * BLK + qi - kx
    visible = (sj >= nband) | ((jnp.abs(dq) <= WINDOW) & (kx >= 0) & (kx < seq))
    neg = jnp.finfo(F32).min
    scale = HEAD_DIM ** -0.5
    for h in range(nkv):
        kh = kcat_ref[:, h * HEAD_DIM:(h + 1) * HEAD_DIM]
        vh = vcat_ref[:, h * HEAD_DIM:(h + 1) * HEAD_DIM]
        for g in range(Q_PER_KV):
            hd = h * Q_PER_KV + g
            qh = q_ref[:, hd * HEAD_DIM:(hd + 1) * HEAD_DIM]
            s = lax.dot_general(qh, kh, (((1,), (1,)), ((), ())), preferred_element_type=F32) * scale
            s = jnp.where(visible, s, neg)
            snk = sink_ref[hd]
            m = jnp.maximum(jnp.max(s, axis=1, keepdims=True), snk)
            e = jnp.exp(s - m)
            denom = jnp.sum(e, axis=1, keepdims=True) + jnp.exp(snk - m)
            o = jnp.dot(e.astype(BF16), vh, preferred_element_type=F32) / denom
            o_ref[:, hd * HEAD_DIM:(hd + 1) * HEAD_DIM] = o.astype(BF16)


def _attention(qkv, qkvm, sink, p):
    attn, kvd, seq, b = p["attn"], p["kvd"], p["seq"], p["b"]
    nbx = seq // BLK
    kc = attn // kvd
    nkeys = 3 * BLK + N_META

    def kv_spec(off, col):
        return pl.BlockSpec((BLK, kvd), lambda bi, n: (bi * nbx + jnp.clip(n + off, 0, nbx - 1), col))

    return pl.pallas_call(
        functools.partial(_attn_kernel, p=p),
        grid=(b, nbx),
        in_specs=[
            pl.BlockSpec(memory_space=pltpu.SMEM),
            pl.BlockSpec((BLK, attn), lambda bi, n: (bi * nbx + n, 0)),
            kv_spec(-1, kc), kv_spec(0, kc), kv_spec(1, kc),
            kv_spec(-1, kc + 1), kv_spec(0, kc + 1), kv_spec(1, kc + 1),
            pl.BlockSpec((None, N_META, kvd), lambda bi, n: (0, 0, kc)),
            pl.BlockSpec((None, N_META, kvd), lambda bi, n: (0, 0, kc + 1)),
        ],
        out_specs=pl.BlockSpec((BLK, attn), lambda bi, n: (bi * nbx + n, 0)),
        out_shape=jax.ShapeDtypeStruct((p["r"], attn), BF16),
        scratch_shapes=[pltpu.VMEM((nkeys, kvd), BF16), pltpu.VMEM((nkeys, kvd), BF16)],
        compiler_params=pltpu.CompilerParams(dimension_semantics=("arbitrary", "arbitrary")),
        name="attn",
    )(sink, qkv, qkv, qkv, qkv, qkv, qkv, qkv, qkvm, qkvm)


def _lane_pack(cols, rows, dtype):
    lane = lax.broadcasted_iota(I32, (rows, LANES), 1)
    out = jnp.zeros((rows, LANES), dtype)
    for k, c in enumerate(cols):
        out = jnp.where(lane == k, c.astype(dtype), out)
    return out


def _mixer_kernel(attn_ref, rest_ref, prev_ref, next_ref, restm_ref, x_ref, cw_ref, wao_ref, wco_ref,
                  wout_ref, g_ref, rw_ref, rb_ref,
                  h1_ref, hn2_ref, ti_ref, gate_ref, rank_ref, cnt_ref, carry_ref, *, p):
    tc, conv, d, n_exp, seq = p["tc"], p["conv"], p["d"], p["n_exp"], p["seq"]
    i = pl.program_id(0)
    tiles_per_seq = seq // tc
    is_first = (i % tiles_per_seq) == 0
    is_last = (i % tiles_per_seq) == tiles_per_seq - 1
    o_ch, o_cb, o_cc, o_ga, o_gc = 0, conv, 2 * conv, 3 * conv, 3 * conv + d

    @pl.when(i == 0)
    def _():
        carry_ref[...] = jnp.zeros_like(carry_ref)

    def u_of(ref):
        return ref[:, o_cc:o_cc + conv].astype(F32) * ref[:, o_ch:o_ch + conv].astype(F32)

    u = u_of(rest_ref)
    last = N_META - 1
    u_prev = jnp.where(is_first, u_of(restm_ref)[last:last + 1], u_of(prev_ref)[last:last + 1])
    u_next = jnp.where(is_last, 0.0, u_of(next_ref)[0:1])
    row = lax.broadcasted_iota(I32, (tc, conv), 0)
    u_m1 = jnp.where(row == 0, u_prev, pltpu.roll(u, 1, 0))
    u_p1 = jnp.where(row == tc - 1, u_next, pltpu.roll(u, tc - 1, 0))
    cw = cw_ref[...]
    cv = u_m1 * cw[0:1] + u * cw[1:2] + u_p1 * cw[2:3]
    yc_in = (rest_ref[:, o_cb:o_cb + conv].astype(F32) * cv).astype(BF16)
    y_conv = jnp.dot(yc_in, wco_ref[...], preferred_element_type=F32)
    y_attn = jnp.dot(attn_ref[...], wao_ref[...], preferred_element_type=F32)
    g_a = rest_ref[:, o_ga:o_ga + d].astype(F32)
    g_c = rest_ref[:, o_gc:o_gc + d].astype(F32)
    merged = jax.nn.sigmoid(g_a) * y_attn + jax.nn.sigmoid(g_c) * y_conv
    h1 = x_ref[...] + jnp.dot(merged.astype(BF16), wout_ref[...], preferred_element_type=F32)
    h1_ref[...] = h1
    ms = jnp.mean(h1 * h1, axis=-1, keepdims=True)
    hn2 = (h1 * lax.rsqrt(ms + RMS_EPS)) * g_ref[...]
    hn2_ref[...] = hn2

    logits = jnp.dot(hn2.astype(BF16), rw_ref[...], preferred_element_type=F32) + rb_ref[...]
    lane = lax.broadcasted_iota(I32, (tc, n_exp), 1).astype(F32)
    sel = jnp.zeros((tc, n_exp), F32)
    tv, ti = [], []
    cur = logits
    for _ in range(TOP_K):
        m = jnp.max(cur, axis=1, keepdims=True)
        idx = jnp.min(jnp.where(cur == m, lane, float(n_exp)), axis=1, keepdims=True)
        hit = lane == idx
        tv.append(m)
        ti.append(idx)
        sel = jnp.where(hit, 1.0, sel)
        cur = jnp.where(hit, -jnp.inf, cur)
    ex = [jnp.exp(v - tv[0]) for v in tv]
    tot = ex[0] + ex[1] + ex[2] + ex[3]
    gates = [e / tot for e in ex]

    r_i = lax.broadcasted_iota(I32, (tc, tc), 0)
    c_i = lax.broadcasted_iota(I32, (tc, tc), 1)
    lower = jnp.where(r_i > c_i, 1.0, 0.0).astype(BF16)
    before = jnp.dot(lower, sel.astype(BF16), preferred_element_type=F32) + carry_ref[0:1, 0:n_exp]
    ranks = [jnp.sum(jnp.where(lane == t, before, 0.0), axis=1, keepdims=True) for t in ti]
    carry_ref[0:1, 0:n_exp] = carry_ref[0:1, 0:n_exp] + jnp.sum(sel, axis=0, keepdims=True)

    ti_ref[...] = _lane_pack(ti, tc, I32)
    gate_ref[...] = _lane_pack(gates, tc, F32)
    rank_ref[...] = _lane_pack(ranks, tc, I32)
    cnt_ref[...] = carry_ref[...]


def _mixer(attn_o, rest, restm, x2, conv_w, wao, wco, wout, g_ffn, router_w, router_b, p):
    tc, d, r, rw, n_exp = p["tc"], p["d"], p["r"], p["rest_w"], p["n_exp"]
    nt = r // tc
    sub = tc // N_META
    n16 = r // N_META
    const = lambda shape: pl.BlockSpec(shape, lambda i: (0,) * len(shape))
    est = (2 * (tc * p["attn"] * 2 + tc * rw * 2 + tc * d * 4 + 3 * N_META * rw * 2)
           + 2 * (p["attn"] * d + p["conv"] * d + d * d + d * n_exp) * 2
           + 2 * (2 * tc * d * 4 + 3 * tc * LANES * 4) + 10 * tc * d * 4)
    return pl.pallas_call(
        functools.partial(_mixer_kernel, p=p),
        grid=(nt,),
        in_specs=[
            pl.BlockSpec((tc, p["attn"]), lambda i: (i, 0)),
            pl.BlockSpec((tc, rw), lambda i: (i, 0)),
            pl.BlockSpec((N_META, rw), lambda i: (jnp.maximum(i * sub - 1, 0), 0)),
            pl.BlockSpec((N_META, rw), lambda i: (jnp.minimum((i + 1) * sub, n16 - 1), 0)),
            pl.BlockSpec((None, N_META, rw), lambda i: (0, 0, 0)),
            pl.BlockSpec((tc, d), lambda i: (i, 0)),
            const((CONV_K, p["conv"])),
            const((p["attn"], d)), const((p["conv"], d)), const((d, d)),
            const((1, d)), const((d, n_exp)), const((1, n_exp)),
        ],
        out_specs=[
            pl.BlockSpec((tc, d), lambda i: (i, 0)),
            pl.BlockSpec((tc, d), lambda i: (i, 0)),
            pl.BlockSpec((tc, LANES), lambda i: (i, 0)),
            pl.BlockSpec((tc, LANES), lambda i: (i, 0)),
            pl.BlockSpec((tc, LANES), lambda i: (i, 0)),
            pl.BlockSpec((8, LANES), lambda i: (0, 0)),
        ],
        out_shape=[
            jax.ShapeDtypeStruct((r, d), F32),
            jax.ShapeDtypeStruct((r, d), F32),
            jax.ShapeDtypeStruct((r, LANES), I32),
            jax.ShapeDtypeStruct((r, LANES), F32),
            jax.ShapeDtypeStruct((r, LANES), I32),
            jax.ShapeDtypeStruct((8, LANES), F32),
        ],
        scratch_shapes=[pltpu.VMEM((8, LANES), F32)],
        compiler_params=pltpu.CompilerParams(
            dimension_semantics=("arbitrary",), vmem_limit_bytes=_vmem_limit(est)),
        name="mixer",
    )(attn_o, rest, rest, rest, restm, x2, conv_w, wao, wco, wout, g_ffn, router_w, router_b)


def _dest_kernel(ti_ref, rank_ref, start_ref, dest_ref):
    rows = ti_ref.shape[0]
    lane = lax.broadcasted_iota(I32, (rows, LANES), 1).astype(F32)
    ti = ti_ref[...].astype(F32)
    rank = rank_ref[...].astype(F32)
    start = start_ref[0:1, :].astype(F32)
    out = jnp.zeros((rows, LANES), F32)
    for k in range(TOP_K):
        e_k = jnp.sum(jnp.where(lane == k, ti, 0.0), axis=1, keepdims=True)
        r_k = jnp.sum(jnp.where(lane == k, rank, 0.0), axis=1, keepdims=True)
        s_k = jnp.sum(jnp.where(lane == e_k, start, 0.0), axis=1, keepdims=True)
        out = jnp.where(lane == k, s_k + r_k, out)
    dest_ref[...] = out.astype(I32)


def _dest(ti, rank, row_start, p):
    r = p["r"]
    rows = min(1024, r)
    return pl.pallas_call(
        _dest_kernel,
        grid=(r // rows,),
        in_specs=[pl.BlockSpec((rows, LANES), lambda i: (i, 0)),
                  pl.BlockSpec((rows, LANES), lambda i: (i, 0)),
                  pl.BlockSpec((8, LANES), lambda i: (0, 0))],
        out_specs=pl.BlockSpec((rows, LANES), lambda i: (i, 0)),
        out_shape=jax.ShapeDtypeStruct((r, LANES), I32),
        name="dest",
    )(ti, rank, row_start)


def _routing_tables(counts, p):
    rt, tpc, n_exp, nch = p["rt"], p["tpc"], p["n_exp"], p["n_chunks_max"]
    cnt = counts[0, :n_exp].astype(I32)
    ntile = (cnt + rt - 1) // rt
    tile_start = jnp.cumsum(ntile) - ntile
    nchunk = (ntile + tpc - 1) // tpc
    chunk_end = jnp.cumsum(nchunk)
    c = jnp.arange(nch, dtype=I32)
    ce = jnp.minimum(jnp.sum((chunk_end[None, :] <= c[:, None]).astype(I32), axis=1), n_exp - 1)
    first = c - (chunk_end - nchunk)[ce]
    c_nt = jnp.clip(ntile[ce] - first * tpc, 0, tpc)
    c_nt = jnp.where(c < chunk_end[-1], c_nt, 0)
    c_ts = tile_start[ce] + first * tpc
    last_e = ce[jnp.maximum(chunk_end[-1] - 1, 0)]
    ce = jnp.where(c_nt > 0, ce, last_e)
    row_start = jnp.zeros((8, LANES), I32).at[0, :n_exp].set(tile_start * rt)
    used = jnp.sum(ntile).reshape(1).astype(I32)
    return row_start, cnt, tile_start * rt, ce.astype(I32), c_ts.astype(I32), c_nt.astype(I32), used


def _dispatch_kernel(cnt_ref, start_ref, dest_ref, hn_ref, xs_ref, zero_ref, sem, zsem, *, p):
    te, n_exp, rt, d = p["te"], p["n_exp"], p["rt"], p["d"]
    step = pl.program_id(0)

    @pl.when(step == 0)
    def _():
        zero_ref[...] = jnp.zeros_like(zero_ref)

        def zero_rows(dst0, size):
            cp = pltpu.make_async_copy(zero_ref.at[pl.ds(0, size)], xs_ref.at[pl.ds(dst0, size)], zsem)
            cp.start()
            cp.wait()

        def per_expert(e, _):
            cnt = cnt_ref[e]
            cur = start_ref[e] + cnt
            pad = (rt - (cnt & (rt - 1))) & (rt - 1)
            head = (SUBLANES - (cur & (SUBLANES - 1))) & (SUBLANES - 1)
            for j in range(SUBLANES - 1):
                @pl.when(j < head)
                def _(j=j):
                    zero_rows(cur + j, 1)
            cur = cur + head
            rem = pad - head
            size = SUBLANES
            while size < rt:
                @pl.when((rem & size) != 0)
                def _(cur=cur, size=size):
                    zero_rows(pl.multiple_of(cur, SUBLANES), size)
                cur = cur + (rem & size)
                size *= 2
            return 0

        lax.fori_loop(0, n_exp, per_expert, 0)

        half = rt // 2
        used = (start_ref[n_exp - 1] + cnt_ref[n_exp - 1] + rt - 1) // rt

        def tail_copy(t, j):
            r0 = pl.multiple_of(t * rt + j * half, half)
            return pltpu.make_async_copy(zero_ref, xs_ref.at[pl.ds(r0, half)], zsem)

        def tail_start(t, _):
            tail_copy(t, 0).start()
            tail_copy(t, 1).start()
            return 0

        def tail_wait(t, _):
            tail_copy(t, 0).wait()
            tail_copy(t, 1).wait()
            return 0

        lax.fori_loop(used, p["n_tiles_max"], tail_start, 0)
        lax.fori_loop(used, p["n_tiles_max"], tail_wait, 0)

    def row_copy(i, k):
        return pltpu.make_async_copy(hn_ref.at[pl.ds(i, 1)], xs_ref.at[pl.ds(dest_ref[i * TOP_K + k], 1)], sem)

    def issue(i, _):
        for k in range(TOP_K):
            row_copy(i, k).start()
        return 0

    lax.fori_loop(0, te, issue, 0)

    def drain(i, _):
        for k in range(TOP_K):
            row_copy(i, k).wait()
        return 0

    lax.fori_loop(0, te, drain, 0)


def _dispatch(hn2, dest_flat, cnt, start, p):
    te, d, r = p["te"], p["d"], p["r"]
    return pl.pallas_call(
        functools.partial(_dispatch_kernel, p=p),
        grid=(r // te,),
        in_specs=[
            pl.BlockSpec(memory_space=pltpu.SMEM),
            pl.BlockSpec(memory_space=pltpu.SMEM),
            pl.BlockSpec((te * TOP_K,), lambda i: (i,), memory_space=pltpu.SMEM),
            pl.BlockSpec((te, d), lambda i: (i, 0)),
        ],
        out_specs=pl.BlockSpec(memory_space=pl.ANY),
        out_shape=jax.ShapeDtypeStruct((p["n_slots"], d), F32),
        scratch_shapes=[pltpu.VMEM((p["rt"] // 2, d), F32), pltpu.SemaphoreType.DMA, pltpu.SemaphoreType.DMA],
        compiler_params=pltpu.CompilerParams(dimension_semantics=("arbitrary",)),
        name="dispatch",
    )(cnt, start, dest_flat, hn2)


def _moe_kernel(ce_ref, cts_ref, cnt_ref, used_ref, xs_ref, wg_ref, wu_ref, bg_ref, bu_ref, wd_ref, bd_ref,
                ys_ref, xbuf_ref, yacc_ref, stage_ref, wgb_ref, wub_ref, wdb_ref, xsem, ysem, *, p):
    rt, nf, nch = p["rt"], p["nf"], p["n_chunks_max"]
    c = pl.program_id(0)
    f = pl.program_id(1)
    nt = cnt_ref[c]
    c_next = jnp.minimum(c + 1, nch - 1)
    nt_next = jnp.where(c + 1 < nch, cnt_ref[c_next], 0)

    def x_copy(chunk, t, slot):
        src0 = pl.multiple_of((cts_ref[chunk] + t) * rt, rt)
        return pltpu.make_async_copy(xs_ref.at[pl.ds(src0, rt)], stage_ref.at[slot], xsem.at[slot])

    def y_copy(chunk, t):
        r0 = pl.multiple_of(t * rt, rt)
        dst0 = pl.multiple_of((cts_ref[chunk] + t) * rt, rt)
        return pltpu.make_async_copy(yacc_ref.at[pl.ds(r0, rt)], ys_ref.at[pl.ds(dst0, rt)], ysem)

    def start_first_two(chunk, n_tiles):
        x_copy(chunk, 0, 0).start()

        @pl.when(n_tiles > 1)
        def _():
            x_copy(chunk, 1, 1).start()

    @pl.when(nt > 0)
    def _():
        @pl.when(f == 0)
        def _():
            @pl.when(c == 0)
            def _():
                start_first_two(c, nt)

            def load(t, _):
                slot = t % 2
                x_copy(c, t, slot).wait()
                r0 = pl.multiple_of(t * rt, rt)
                xbuf_ref[pl.ds(r0, rt), :] = stage_ref[slot].astype(BF16)

                @pl.when(t + 2 < nt)
                def _():
                    x_copy(c, t + 2, slot).start()

                return 0

            lax.fori_loop(0, nt, load, 0)

        wgb_ref[...] = wg_ref[0].astype(BF16)
        wub_ref[...] = wu_ref[0].astype(BF16)
        wdb_ref[...] = wd_ref[0].astype(BF16)
        bg = bg_ref[0]
        bu = bu_ref[0]
        bd = bd_ref[0]

        @pl.when(f == 0)
        def _():
            @pl.when(c > 0)
            def _():
                def drain(t, _):
                    y_copy(c - 1, t).wait()
                    return 0

                lax.fori_loop(0, cnt_ref[jnp.maximum(c - 1, 0)], drain, 0)

            def init(t, _):
                r0 = pl.multiple_of(t * rt, rt)
                yacc_ref[pl.ds(r0, rt), :] = jnp.broadcast_to(bd, (rt, p["d"]))
                return 0

            lax.fori_loop(0, nt, init, 0)

        @pl.when((f == nf - 1) & (nt_next > 0))
        def _():
            start_first_two(c_next, nt_next)

        def sub(t):
            r0 = pl.multiple_of(t * rt, rt)
            xt = xbuf_ref[pl.ds(r0, rt), :]
            gate = jnp.dot(xt, wgb_ref[...], preferred_element_type=F32) + bg
            up = jnp.dot(xt, wub_ref[...], preferred_element_type=F32) + bu
            gate = jnp.minimum(gate, SWIGLU_LIMIT)
            up = jnp.clip(up, -SWIGLU_LIMIT, SWIGLU_LIMIT)
            act = (up + 1.0) * gate * jax.nn.sigmoid(SWIGLU_ALPHA * gate)
            yacc_ref[pl.ds(r0, rt), :] += jnp.dot(act.astype(BF16), wdb_ref[...], preferred_element_type=F32)

        def pair(i, _):
            sub(2 * i)
            sub(2 * i + 1)
            return 0

        lax.fori_loop(0, nt // 2, pair, 0)

        @pl.when(nt % 2 == 1)
        def _():
            sub(nt - 1)

        @pl.when(f == nf - 1)
        def _():
            def store(t, _):
                y_copy(c, t).start()
                return 0

            lax.fori_loop(0, nt, store, 0)

            @pl.when(nt_next == 0)
            def _():
                def drain(t, _):
                    y_copy(c, t).wait()
                    return 0

                lax.fori_loop(0, nt, drain, 0)

    @pl.when((c == nch - 1) & (f == nf - 1))
    def _():
        stage_ref[0] = jnp.zeros((rt, p["d"]), F32)

        def tail_copy(t):
            return pltpu.make_async_copy(stage_ref.at[0], ys_ref.at[pl.ds(pl.multiple_of(t * rt, rt), rt)], ysem)

        def tail_start(t, _):
            tail_copy(t).start()
            return 0

        def tail_wait(t, _):
            tail_copy(t).wait()
            return 0

        lax.fori_loop(used_ref[0], p["n_tiles_max"], tail_start, 0)
        lax.fori_loop(used_ref[0], p["n_tiles_max"], tail_wait, 0)


def _moe(xs, ce, cts, cnt, used, w_gate_up, b_gate_up, w_down, b_down, p):
    d, tf, nf, rt, cap = p["d"], p["tf"], p["nf"], p["rt"], p["cap"]
    n_exp = p["n_exp"]
    bgu = b_gate_up.reshape(n_exp, 1, 2 * p["d_ff"])
    bdn = b_down.reshape(n_exp, 1, d)

    def fi(c, f, cnt_r):
        return jnp.where(cnt_r[c] > 0, f, nf - 1)

    est = (2 * (2 * d * tf * 4 + tf * d * 4) + 3 * d * tf * 2 + cap * d * 2 + cap * d * 4 + 2 * rt * d * 4
           + 6 * rt * d * 4)
    grid_spec = pltpu.PrefetchScalarGridSpec(
        num_scalar_prefetch=4,
        grid=(p["n_chunks_max"], nf),
        in_specs=[
            pl.BlockSpec(memory_space=pl.ANY),
            pl.BlockSpec((1, d, tf), lambda c, f, ce_r, cts_r, cnt_r, u_r: (ce_r[c], 0, fi(c, f, cnt_r))),
            pl.BlockSpec((1, d, tf), lambda c, f, ce_r, cts_r, cnt_r, u_r: (ce_r[c], 0, nf + fi(c, f, cnt_r))),
            pl.BlockSpec((1, 1, tf), lambda c, f, ce_r, cts_r, cnt_r, u_r: (ce_r[c], 0, fi(c, f, cnt_r))),
            pl.BlockSpec((1, 1, tf), lambda c, f, ce_r, cts_r, cnt_r, u_r: (ce_r[c], 0, nf + fi(c, f, cnt_r))),
            pl.BlockSpec((1, tf, d), lambda c, f, ce_r, cts_r, cnt_r, u_r: (ce_r[c], fi(c, f, cnt_r), 0)),
            pl.BlockSpec((1, 1, d), lambda c, f, ce_r, cts_r, cnt_r, u_r: (ce_r[c], 0, 0)),
        ],
        out_specs=pl.BlockSpec(memory_space=pl.ANY),
        scratch_shapes=[
            pltpu.VMEM((cap, d), BF16), pltpu.VMEM((cap, d), F32), pltpu.VMEM((2, rt, d), F32),
            pltpu.VMEM((d, tf), BF16), pltpu.VMEM((d, tf), BF16), pltpu.VMEM((tf, d), BF16),
            pltpu.SemaphoreType.DMA((2,)), pltpu.SemaphoreType.DMA,
        ],
    )
    return pl.pallas_call(
        functools.partial(_moe_kernel, p=p),
        grid_spec=grid_spec,
        out_shape=jax.ShapeDtypeStruct((p["n_slots"], d), F32),
        compiler_params=pltpu.CompilerParams(
            dimension_semantics=("arbitrary", "arbitrary"), vmem_limit_bytes=_vmem_limit(est)),
        name="moe",
    )(ce, cts, cnt, used, xs, w_gate_up, w_gate_up, bgu, bgu, w_down, bdn)


def _combine_kernel(dest_ref, h1_ref, gate_ref, g_ref, ys_ref, o_ref, ybuf_ref, sem, *, p):
    tg = p["tg"]

    def row_copy(i, k):
        return pltpu.make_async_copy(ys_ref.at[pl.ds(dest_ref[i * TOP_K + k], 1)],
                                     ybuf_ref.at[k, pl.ds(i, 1)], sem)

    def issue(i, _):
        for k in range(TOP_K):
            row_copy(i, k).start()
        return 0

    lax.fori_loop(0, tg, issue, 0)

    def drain(i, _):
        for k in range(TOP_K):
            row_copy(i, k).wait()
        return 0

    lax.fori_loop(0, tg, drain, 0)

    lane = lax.broadcasted_iota(I32, (tg, LANES), 1)
    gates = gate_ref[...]
    h = h1_ref[...]
    for k in range(TOP_K):
        g_k = jnp.sum(jnp.where(lane == k, gates, 0.0), axis=1, keepdims=True)
        h = h + ybuf_ref[k] * g_k
    ms = jnp.mean(h * h, axis=-1, keepdims=True)
    o_ref[...] = (h * lax.rsqrt(ms + RMS_EPS)) * g_ref[...]


def _combine(ys, dest_flat, h1, gates, g_final, p):
    tg, d, r = p["tg"], p["d"], p["r"]
    return pl.pallas_call(
        functools.partial(_combine_kernel, p=p),
        grid=(r // tg,),
        in_specs=[
            pl.BlockSpec((tg * TOP_K,), lambda i: (i,), memory_space=pltpu.SMEM),
            pl.BlockSpec((tg, d), lambda i: (i, 0)),
            pl.BlockSpec((tg, LANES), lambda i: (i, 0)),
            pl.BlockSpec((1, d), lambda i: (0, 0)),
            pl.BlockSpec(memory_space=pl.ANY),
        ],
        out_specs=pl.BlockSpec((tg, d), lambda i: (i, 0)),
        out_shape=jax.ShapeDtypeStruct((r, d), F32),
        scratch_shapes=[pltpu.VMEM((TOP_K, tg, d), F32), pltpu.SemaphoreType.DMA],
        compiler_params=pltpu.CompilerParams(dimension_semantics=("arbitrary",)),
        name="combine",
    )(dest_flat, h1, gates, g_final, ys)


def kernel(x, meta_tokens, norm_mix_g, w_in, conv_w, sink, w_attn_o, w_conv_o, w_out, norm_ffn_g,
           router_w, router_b, w_gate_up, b_gate_up, w_down, b_down, norm_final_g):
    b, seq, d = x.shape
    n_exp, d_ff = w_down.shape[1], w_down.shape[2]
    assert norm_mix_g.shape[0] == 1 and meta_tokens.shape[0] == N_META
    p = _plan(b, seq, d, n_exp, d_ff)
    x2 = x.reshape(p["r"], d)

    qkv, rest, qkvm, restm = _inproj(x2, meta_tokens.astype(F32), norm_mix_g[0].reshape(1, d), w_in[0], p)
    attn_o = _attention(qkv, qkvm, sink[0].astype(F32), p)
    h1, hn2, ti, gates, rank, counts = _mixer(
        attn_o, rest, restm, x2, conv_w[0], w_attn_o[0].astype(BF16), w_conv_o[0].astype(BF16),
        w_out[0].astype(BF16), norm_ffn_g[0].reshape(1, d), router_w[0].astype(BF16),
        router_b[0].reshape(1, n_exp), p)
    row_start, cnt, start, ce, cts, cnt_tiles, used = _routing_tables(counts, p)
    dest = _dest(ti, rank, row_start, p)
    dest_flat = dest[:, :TOP_K].reshape(-1)
    xs = _dispatch(hn2, dest_flat, cnt, start, p)
    ys = _moe(xs, ce, cts, cnt_tiles, used, w_gate_up[0], b_gate_up[0], w_down[0], b_down[0], p)
    out = _combine(ys, dest_flat, h1, gates, norm_final_g.reshape(1, d), p)
    return out.reshape(b, seq, d)
```

```python
import functools

import numpy as np
import jax
import jax.numpy as jnp
from jax import lax
from jax.experimental import pallas as pl
from jax.experimental.pallas import tpu as pltpu

N_META = 16
BLK = 128
WINDOW = 128
HEAD_DIM = 64
Q_PER_KV = 4
ROT_DIM = HEAD_DIM // 4
ROPE_THETA = 500000.0
CONV_K = 3
TOP_K = 4
SWIGLU_ALPHA = 1.702
SWIGLU_LIMIT = 7.0
RMS_EPS = 1e-5

LANES = 128
SUBLANES = 8
VMEM_LIMIT_CAP = 60000 * 1024
MOE_ROW_TILE = 256
MOE_CHUNK_TILES = 5

F32 = jnp.float32
BF16 = jnp.bfloat16
I32 = jnp.int32


def _vmem_limit(nbytes):
    return int(min(VMEM_LIMIT_CAP, max(32 * 1024 * 1024, nbytes * 5 // 4 + (4 << 20))))


def _plan(b, seq, d, n_exp, d_ff):
    attn = (d // 128) * HEAD_DIM
    kvd = attn // Q_PER_KV
    conv = d // 2
    r = b * seq
    p = dict(b=b, seq=seq, d=d, n_exp=n_exp, d_ff=d_ff, attn=attn, kvd=kvd, conv=conv, r=r)
    p["nq"] = attn // HEAD_DIM
    p["nkv"] = kvd // HEAD_DIM
    p["in_dim"] = attn + 2 * kvd + 3 * conv + 2 * d
    p["tn"] = 2 * kvd
    p["tm"] = min(1024, seq)
    p["n_qkv_tiles"] = (attn + 2 * kvd) // p["tn"]
    p["n_col_tiles"] = p["in_dim"] // p["tn"]
    p["rest_w"] = 3 * conv + 2 * d
    p["tc"] = min(256, seq)
    p["te"] = min(256, seq)
    p["tg"] = min(128, seq)
    p["tf"] = min(512, d_ff)
    p["nf"] = d_ff // p["tf"]
    p["rt"] = MOE_ROW_TILE
    p["tpc"] = MOE_CHUNK_TILES
    p["cap"] = MOE_ROW_TILE * MOE_CHUNK_TILES
    p["n_tiles_max"] = (r * TOP_K) // p["rt"] + n_exp
    p["n_chunks_max"] = n_exp + (p["n_tiles_max"] - n_exp) // p["tpc"]
    p["n_slots"] = p["n_tiles_max"] * p["rt"]
    assert seq % p["tm"] == 0 and seq % p["tc"] == 0 and seq % BLK == 0
    assert attn % p["tn"] == 0 and p["in_dim"] % p["tn"] == 0 and kvd % LANES == 0
    assert (r * TOP_K) % p["rt"] == 0 and d_ff % p["tf"] == 0
    assert p["rt"] & (p["rt"] - 1) == 0, "the zero-fill decomposition needs a power-of-two row tile"
    return p


def _rope_tables(seq):
    half = ROT_DIM // 2
    pos = jnp.arange(N_META + seq, dtype=F32)
    inv_freq = ROPE_THETA ** (-jnp.arange(0, ROT_DIM, 2, dtype=F32) / ROT_DIM)
    ang = pos[:, None] * inv_freq[None, :]
    cos, sin = jnp.cos(ang), jnp.sin(ang)
    lane = np.arange(LANES) % HEAD_DIM
    idx = np.where(lane < ROT_DIM, lane % half, 0)
    rot = jnp.asarray(lane < ROT_DIM)
    sign = jnp.asarray(np.where(lane < half, -1.0, 1.0).astype(np.float32))
    cos_l = jnp.where(rot[None, :], cos[:, idx], 1.0)
    sin_l = jnp.where(rot[None, :], sin[:, idx] * sign[None, :], 0.0)
    tab = jnp.stack([cos_l, sin_l]).astype(F32)
    return tab[:, N_META:], tab[:, :N_META]


def _rope(t, cos, sin, n_cols):
    lane = lax.broadcasted_iota(I32, (t.shape[0], LANES), 1)
    first = (lane % HEAD_DIM) < (ROT_DIM // 2)
    outs = []
    for c in range(t.shape[1] // LANES):
        s = t[:, c * LANES:(c + 1) * LANES]
        if c * LANES < n_cols:
            partner = jnp.where(first, pltpu.roll(s, LANES - ROT_DIM // 2, 1), pltpu.roll(s, ROT_DIM // 2, 1))
            s = s * cos + partner * sin
        outs.append(s)
    return jnp.concatenate(outs, axis=1)


def _inproj_kernel(x_ref, meta_ref, g_ref, w_ref, csx_ref, csm_ref,
                   qkv_ref, rest_ref, qkvm_ref, restm_ref, hn_ref, wb_ref, *, p):
    tm, tn, kvd = p["tm"], p["tn"], p["kvd"]
    n_q_tiles = p["attn"] // tn
    n = pl.program_id(1)

    @pl.when(n == 0)
    def _():
        g = g_ref[...]

        def norm(v):
            ms = jnp.mean(v * v, axis=-1, keepdims=True)
            return ((v * lax.rsqrt(ms + RMS_EPS)) * g).astype(BF16)

        rows = min(128, tm)

        def body(i, _):
            r0 = pl.multiple_of(i * rows, rows)
            hn_ref[pl.ds(r0, rows), :] = norm(x_ref[pl.ds(r0, rows), :])
            return 0

        lax.fori_loop(0, tm // rows, body, 0)
        hn_ref[tm:tm + N_META, :] = norm(meta_ref[...])

    wb_ref[...] = w_ref[...].astype(BF16)
    res = jnp.dot(hn_ref[...], wb_ref[...], preferred_element_type=F32)
    res_x, res_m = res[:tm], res[tm:]

    def store_qkv(n_cols, scale):
        rx = _rope(res_x, csx_ref[0], csx_ref[1], n_cols)
        rm = _rope(res_m, csm_ref[0], csm_ref[1], n_cols)
        if scale != 1.0:
            rx = rx * scale
        qkv_ref[...] = rx.astype(BF16)
        qkvm_ref[0] = rm.astype(BF16)

    @pl.when(n < n_q_tiles)
    def _():
        store_qkv(tn, HEAD_DIM ** -0.5)

    @pl.when(n == n_q_tiles)
    def _():
        store_qkv(kvd, 1.0)

    @pl.when(n >= p["n_qkv_tiles"])
    def _():
        rest_ref[...] = res_x.astype(BF16)
        restm_ref[0] = res_m.astype(BF16)


def _inproj(x2, meta, g, w_in, p):
    tm, tn, d, r = p["tm"], p["tn"], p["d"], p["r"]
    nm, nn, nqkv = r // tm, p["n_col_tiles"], p["n_qkv_tiles"]
    csx, csm = _rope_tables(p["seq"])
    spt = p["seq"] // tm
    qkv_w = nqkv * tn
    est = (2 * tm * d * 4 + (tm + 16) * d * 2 + 2 * d * tn * 4 + d * tn * 2 + 4 * tm * tn * 2
           + 4 * 2 * tm * LANES * 4 + 4 * (tm + 16) * tn * 4)
    return pl.pallas_call(
        functools.partial(_inproj_kernel, p=p),
        grid=(nm, nn),
        in_specs=[
            pl.BlockSpec((tm, d), lambda m, n: (m, 0)),
            pl.BlockSpec((N_META, d), lambda m, n: (0, 0)),
            pl.BlockSpec((1, d), lambda m, n: (0, 0)),
            pl.BlockSpec((d, tn), lambda m, n: (0, n)),
            pl.BlockSpec((2, tm, LANES), lambda m, n: (0, m % spt, 0)),
            pl.BlockSpec((2, N_META, LANES), lambda m, n: (0, 0, 0)),
        ],
        out_specs=[
            pl.BlockSpec((tm, tn), lambda m, n: (m, jnp.minimum(n, nqkv - 1))),
            pl.BlockSpec((tm, tn), lambda m, n: (m, jnp.maximum(n - nqkv, 0))),
            pl.BlockSpec((1, N_META, tn), lambda m, n: (m, 0, jnp.minimum(n, nqkv - 1))),
            pl.BlockSpec((1, N_META, tn), lambda m, n: (m, 0, jnp.maximum(n - nqkv, 0))),
        ],
        out_shape=[
            jax.ShapeDtypeStruct((r, qkv_w), BF16),
            jax.ShapeDtypeStruct((r, p["rest_w"]), BF16),
            jax.ShapeDtypeStruct((nm, N_META, qkv_w), BF16),
            jax.ShapeDtypeStruct((nm, N_META, p["rest_w"]), BF16),
        ],
        scratch_shapes=[pltpu.VMEM((tm + N_META, d), BF16), pltpu.VMEM((d, tn), BF16)],
        compiler_params=pltpu.CompilerParams(
            dimension_semantics=("arbitrary", "arbitrary"), vmem_limit_bytes=_vmem_limit(est)),
        name="inproj",
    )(x2, meta, g, w_in, csx, csm)


def _attn_kernel(sink_ref, q_ref, k0_ref, k1_ref, k2_ref, v0_ref, v1_ref, v2_ref, km_ref, vm_ref,
                 o_ref, kcat_ref, vcat_ref, *, p):
    seq, nkv = p["seq"], p["nkv"]
    nband = 3 * BLK
    nkeys = nband + N_META
    n = pl.program_id(1)
    for j, (kr, vr) in enumerate(((k0_ref, v0_ref), (k1_ref, v1_ref), (k2_ref, v2_ref))):
        kcat_ref[j * BLK:(j + 1) * BLK, :] = kr[...]
        vcat_ref[j * BLK:(j + 1) * BLK, :] = vr[...]
    kcat_ref[nband:nkeys, :] = km_ref[...]
    vcat_ref[nband:nkeys, :] = vm_ref[...]

    qi = lax.broadcasted_iota(I32, (BLK, nkeys), 0)
    sj = lax.broadcasted_iota(I32, (BLK, nkeys), 1)
    kx = (n - 1) * BLK + sj
    dq = n * BLK + qi - kx
    visible = (sj >= nband) | ((jnp.abs(dq) <= WINDOW) & (kx >= 0) & (kx < seq))
    neg = jnp.finfo(F32).min
    ones = jnp.ones((nkeys, HEAD_DIM), BF16)
    gi = lax.broadcasted_iota(I32, (Q_PER_KV, 1, 1), 0)
    for h in range(nkv):
        kh = kcat_ref[:, h * HEAD_DIM:(h + 1) * HEAD_DIM]
        vh = jnp.concatenate([vcat_ref[:, h * HEAD_DIM:(h + 1) * HEAD_DIM], ones], axis=1)
        heads = [h * Q_PER_KV + g for g in range(Q_PER_KV)]
        qs = jnp.concatenate([q_ref[:, hd * HEAD_DIM:(hd + 1) * HEAD_DIM] for hd in heads], axis=0)
        s = lax.dot_general(qs, kh, (((1,), (1,)), ((), ())), preferred_element_type=F32)
        s = jnp.where(visible[None], s.reshape(Q_PER_KV, BLK, nkeys), neg)
        snk = jnp.zeros((Q_PER_KV, 1, 1), F32)
        for g, hd in enumerate(heads):
            snk = jnp.where(gi == g, sink_ref[hd], snk)
        m = jnp.maximum(jnp.max(s, axis=2, keepdims=True), snk)
        e = jnp.exp(s - m).reshape(Q_PER_KV * BLK, nkeys)
        pv = jnp.dot(e.astype(BF16), vh, preferred_element_type=F32)
        pv = pv.reshape(Q_PER_KV, BLK, 2 * HEAD_DIM)
        o = pv[:, :, :HEAD_DIM] / (pv[:, :, HEAD_DIM:HEAD_DIM + 1] + jnp.exp(snk - m))
        for g, hd in enumerate(heads):
            o_ref[:, hd * HEAD_DIM:(hd + 1) * HEAD_DIM] = o[g].astype(BF16)


def _attention(qkv, qkvm, sink, p):
    attn, kvd, seq, b = p["attn"], p["kvd"], p["seq"], p["b"]
    nbx = seq // BLK
    kc = attn // kvd
    nkeys = 3 * BLK + N_META

    def kv_spec(off, col):
        return pl.BlockSpec((BLK, kvd), lambda bi, n: (bi * nbx + jnp.clip(n + off, 0, nbx - 1), col))

    return pl.pallas_call(
        functools.partial(_attn_kernel, p=p),
        grid=(b, nbx),
        in_specs=[
            pl.BlockSpec(memory_space=pltpu.SMEM),
            pl.BlockSpec((BLK, attn), lambda bi, n: (bi * nbx + n, 0)),
            kv_spec(-1, kc), kv_spec(0, kc), kv_spec(1, kc),
            kv_spec(-1, kc + 1), kv_spec(0, kc + 1), kv_spec(1, kc + 1),
            pl.BlockSpec((None, N_META, kvd), lambda bi, n: (0, 0, kc)),
            pl.BlockSpec((None, N_META, kvd), lambda bi, n: (0, 0, kc + 1)),
        ],
        out_specs=pl.BlockSpec((BLK, attn), lambda bi, n: (bi * nbx + n, 0)),
        out_shape=jax.ShapeDtypeStruct((p["r"], attn), BF16),
        scratch_shapes=[pltpu.VMEM((nkeys, kvd), BF16), pltpu.VMEM((nkeys, kvd), BF16)],
        compiler_params=pltpu.CompilerParams(dimension_semantics=("arbitrary", "arbitrary")),
        name="attn",
    )(sink, qkv, qkv, qkv, qkv, qkv, qkv, qkv, qkvm, qkvm)


def _lane_pack(cols, rows, dtype):
    lane = lax.broadcasted_iota(I32, (rows, LANES), 1)
    out = jnp.zeros((rows, LANES), dtype)
    for k, c in enumerate(cols):
        out = jnp.where(lane == k, c.astype(dtype), out)
    return out


def _mixer_kernel(attn_ref, rest_ref, prev_ref, next_ref, restm_ref, x_ref, cw_ref, wao_ref, wco_ref,
                  wout_ref, g_ref, rw_ref, rb_ref,
                  h1_ref, hn2_ref, ti_ref, gate_ref, rank_ref, cnt_ref, carry_ref, *, p):
    tc, conv, d, n_exp, seq = p["tc"], p["conv"], p["d"], p["n_exp"], p["seq"]
    i = pl.program_id(0)
    tiles_per_seq = seq // tc
    is_first = (i % tiles_per_seq) == 0
    is_last = (i % tiles_per_seq) == tiles_per_seq - 1
    o_ch, o_cb, o_cc, o_ga, o_gc = 0, conv, 2 * conv, 3 * conv, 3 * conv + d

    @pl.when(i == 0)
    def _():
        carry_ref[...] = jnp.zeros_like(carry_ref)

    def u_of(ref):
        return ref[:, o_cc:o_cc + conv].astype(F32) * ref[:, o_ch:o_ch + conv].astype(F32)

    u = u_of(rest_ref)
    last = N_META - 1
    u_prev = jnp.where(is_first, u_of(restm_ref)[last:last + 1], u_of(prev_ref)[last:last + 1])
    u_next = jnp.where(is_last, 0.0, u_of(next_ref)[0:1])
    row = lax.broadcasted_iota(I32, (tc, conv), 0)
    u_m1 = jnp.where(row == 0, u_prev, pltpu.roll(u, 1, 0))
    u_p1 = jnp.where(row == tc - 1, u_next, pltpu.roll(u, tc - 1, 0))
    cw = cw_ref[...]
    cv = u_m1 * cw[0:1] + u * cw[1:2] + u_p1 * cw[2:3]
    yc_in = (rest_ref[:, o_cb:o_cb + conv].astype(F32) * cv).astype(BF16)
    y_conv = jnp.dot(yc_in, wco_ref[...], preferred_element_type=F32)
    y_attn = jnp.dot(attn_ref[...], wao_ref[...], preferred_element_type=F32)
    g_a = rest_ref[:, o_ga:o_ga + d].astype(F32)
    g_c = rest_ref[:, o_gc:o_gc + d].astype(F32)
    merged = jax.nn.sigmoid(g_a) * y_attn + jax.nn.sigmoid(g_c) * y_conv
    h1 = x_ref[...] + jnp.dot(merged.astype(BF16), wout_ref[...], preferred_element_type=F32)
    h1_ref[...] = h1
    ms = jnp.mean(h1 * h1, axis=-1, keepdims=True)
    hn2 = (h1 * lax.rsqrt(ms + RMS_EPS)) * g_ref[...]
    hn2_ref[...] = hn2

    logits = jnp.dot(hn2.astype(BF16), rw_ref[...], preferred_element_type=F32) + rb_ref[...]
    lane = lax.broadcasted_iota(I32, (tc, n_exp), 1).astype(F32)
    sel = jnp.zeros((tc, n_exp), F32)
    tv, ti = [], []
    cur = logits
    for _ in range(TOP_K):
        m = jnp.max(cur, axis=1, keepdims=True)
        idx = jnp.min(jnp.where(cur == m, lane, float(n_exp)), axis=1, keepdims=True)
        hit = lane == idx
        tv.append(m)
        ti.append(idx)
        sel = jnp.where(hit, 1.0, sel)
        cur = jnp.where(hit, -jnp.inf, cur)
    ex = [jnp.exp(v - tv[0]) for v in tv]
    tot = ex[0] + ex[1] + ex[2] + ex[3]
    gates = [e / tot for e in ex]

    r_i = lax.broadcasted_iota(I32, (tc, tc), 0)
    c_i = lax.broadcasted_iota(I32, (tc, tc), 1)
    lower = jnp.where(r_i > c_i, 1.0, 0.0).astype(BF16)
    before = jnp.dot(lower, sel.astype(BF16), preferred_element_type=F32) + carry_ref[0:1, 0:n_exp]
    ranks = [jnp.sum(jnp.where(lane == t, before, 0.0), axis=1, keepdims=True) for t in ti]
    carry_ref[0:1, 0:n_exp] = carry_ref[0:1, 0:n_exp] + jnp.sum(sel, axis=0, keepdims=True)

    ti_ref[...] = _lane_pack(ti, tc, I32)
    gate_ref[...] = _lane_pack(gates, tc, F32)
    rank_ref[...] = _lane_pack(ranks, tc, I32)
    cnt_ref[...] = carry_ref[...]


def _mixer(attn_o, rest, restm, x2, conv_w, wao, wco, wout, g_ffn, router_w, router_b, p):
    tc, d, r, rw, n_exp = p["tc"], p["d"], p["r"], p["rest_w"], p["n_exp"]
    nt = r // tc
    sub = tc // N_META
    n16 = r // N_META
    const = lambda shape: pl.BlockSpec(shape, lambda i: (0,) * len(shape))
    est = (2 * (tc * p["attn"] * 2 + tc * rw * 2 + tc * d * 4 + 3 * N_META * rw * 2)
           + 2 * (p["attn"] * d + p["conv"] * d + d * d + d * n_exp) * 2
           + 2 * (2 * tc * d * 4 + 3 * tc * LANES * 4) + 10 * tc * d * 4)
    return pl.pallas_call(
        functools.partial(_mixer_kernel, p=p),
        grid=(nt,),
        in_specs=[
            pl.BlockSpec((tc, p["attn"]), lambda i: (i, 0)),
            pl.BlockSpec((tc, rw), lambda i: (i, 0)),
            pl.BlockSpec((N_META, rw), lambda i: (jnp.maximum(i * sub - 1, 0), 0)),
            pl.BlockSpec((N_META, rw), lambda i: (jnp.minimum((i + 1) * sub, n16 - 1), 0)),
            pl.BlockSpec((None, N_META, rw), lambda i: (0, 0, 0)),
            pl.BlockSpec((tc, d), lambda i: (i, 0)),
            const((CONV_K, p["conv"])),
            const((p["attn"], d)), const((p["conv"], d)), const((d, d)),
            const((1, d)), const((d, n_exp)), const((1, n_exp)),
        ],
        out_specs=[
            pl.BlockSpec((tc, d), lambda i: (i, 0)),
            pl.BlockSpec((tc, d), lambda i: (i, 0)),
            pl.BlockSpec((tc, LANES), lambda i: (i, 0)),
            pl.BlockSpec((tc, LANES), lambda i: (i, 0)),
            pl.BlockSpec((tc, LANES), lambda i: (i, 0)),
            pl.BlockSpec((8, LANES), lambda i: (0, 0)),
        ],
        out_shape=[
            jax.ShapeDtypeStruct((r, d), F32),
            jax.ShapeDtypeStruct((r, d), F32),
            jax.ShapeDtypeStruct((r, LANES), I32),
            jax.ShapeDtypeStruct((r, LANES), F32),
            jax.ShapeDtypeStruct((r, LANES), I32),
            jax.ShapeDtypeStruct((8, LANES), F32),
        ],
        scratch_shapes=[pltpu.VMEM((8, LANES), F32)],
        compiler_params=pltpu.CompilerParams(
            dimension_semantics=("arbitrary",), vmem_limit_bytes=_vmem_limit(est)),
        name="mixer",
    )(attn_o, rest, rest, rest, restm, x2, conv_w, wao, wco, wout, g_ffn, router_w, router_b)


def _dest_kernel(ti_ref, rank_ref, start_ref, dest_ref):
    rows = ti_ref.shape[0]
    lane = lax.broadcasted_iota(I32, (rows, LANES), 1).astype(F32)
    ti = ti_ref[...].astype(F32)
    rank = rank_ref[...].astype(F32)
    start = start_ref[0:1, :].astype(F32)
    out = jnp.zeros((rows, LANES), F32)
    for k in range(TOP_K):
        e_k = jnp.sum(jnp.where(lane == k, ti, 0.0), axis=1, keepdims=True)
        r_k = jnp.sum(jnp.where(lane == k, rank, 0.0), axis=1, keepdims=True)
        s_k = jnp.sum(jnp.where(lane == e_k, start, 0.0), axis=1, keepdims=True)
        out = jnp.where(lane == k, s_k + r_k, out)
    dest_ref[...] = out.astype(I32)


def _dest(ti, rank, row_start, p):
    r = p["r"]
    rows = min(1024, r)
    return pl.pallas_call(
        _dest_kernel,
        grid=(r // rows,),
        in_specs=[pl.BlockSpec((rows, LANES), lambda i: (i, 0)),
                  pl.BlockSpec((rows, LANES), lambda i: (i, 0)),
                  pl.BlockSpec((8, LANES), lambda i: (0, 0))],
        out_specs=pl.BlockSpec((rows, LANES), lambda i: (i, 0)),
        out_shape=jax.ShapeDtypeStruct((r, LANES), I32),
        name="dest",
    )(ti, rank, row_start)


def _routing_tables(counts, p):
    rt, tpc, n_exp, nch = p["rt"], p["tpc"], p["n_exp"], p["n_chunks_max"]
    cnt = counts[0, :n_exp].astype(I32)
    ntile = (cnt + rt - 1) // rt
    tile_start = jnp.cumsum(ntile) - ntile
    nchunk = (ntile + tpc - 1) // tpc
    chunk_end = jnp.cumsum(nchunk)
    c = jnp.arange(nch, dtype=I32)
    ce = jnp.minimum(jnp.sum((chunk_end[None, :] <= c[:, None]).astype(I32), axis=1), n_exp - 1)
    first = c - (chunk_end - nchunk)[ce]
    c_nt = jnp.clip(ntile[ce] - first * tpc, 0, tpc)
    c_nt = jnp.where(c < chunk_end[-1], c_nt, 0)
    c_ts = tile_start[ce] + first * tpc
    last_e = ce[jnp.maximum(chunk_end[-1] - 1, 0)]
    ce = jnp.where(c_nt > 0, ce, last_e)
    row_start = jnp.zeros((8, LANES), I32).at[0, :n_exp].set(tile_start * rt)
    used = jnp.sum(ntile).reshape(1).astype(I32)
    return row_start, cnt, tile_start * rt, ce.astype(I32), c_ts.astype(I32), c_nt.astype(I32), used


def _dispatch_kernel(cnt_ref, start_ref, dest_ref, hn_ref, xs_ref, zero_ref, sem, zsem, *, p):
    te, n_exp, rt, d = p["te"], p["n_exp"], p["rt"], p["d"]
    step = pl.program_id(0)

    @pl.when(step == 0)
    def _():
        zero_ref[...] = jnp.zeros_like(zero_ref)

        def zero_rows(dst0, size):
            cp = pltpu.make_async_copy(zero_ref.at[pl.ds(0, size)], xs_ref.at[pl.ds(dst0, size)], zsem)
            cp.start()
            cp.wait()

        def per_expert(e, _):
            cnt = cnt_ref[e]
            cur = start_ref[e] + cnt
            pad = (rt - (cnt & (rt - 1))) & (rt - 1)
            head = (SUBLANES - (cur & (SUBLANES - 1))) & (SUBLANES - 1)
            for j in range(SUBLANES - 1):
                @pl.when(j < head)
                def _(j=j):
                    zero_rows(cur + j, 1)
            cur = cur + head
            rem = pad - head
            size = SUBLANES
            while size < rt:
                @pl.when((rem & size) != 0)
                def _(cur=cur, size=size):
                    zero_rows(pl.multiple_of(cur, SUBLANES), size)
                cur = cur + (rem & size)
                size *= 2
            return 0

        lax.fori_loop(0, n_exp, per_expert, 0)

        half = rt // 2
        used = (start_ref[n_exp - 1] + cnt_ref[n_exp - 1] + rt - 1) // rt

        def tail_copy(t, j):
            r0 = pl.multiple_of(t * rt + j * half, half)
            return pltpu.make_async_copy(zero_ref, xs_ref.at[pl.ds(r0, half)], zsem)

        def tail_start(t, _):
            tail_copy(t, 0).start()
            tail_copy(t, 1).start()
            return 0

        def tail_wait(t, _):
            tail_copy(t, 0).wait()
            tail_copy(t, 1).wait()
            return 0

        lax.fori_loop(used, p["n_tiles_max"], tail_start, 0)
        lax.fori_loop(used, p["n_tiles_max"], tail_wait, 0)

    def row_copy(i, k):
        return pltpu.make_async_copy(hn_ref.at[pl.ds(i, 1)], xs_ref.at[pl.ds(dest_ref[i * TOP_K + k], 1)], sem)

    def issue(i, _):
        for k in range(TOP_K):
            row_copy(i, k).start(priority=k % 2)
        return 0

    lax.fori_loop(0, te, issue, 0)

    def drain(i, _):
        for k in range(TOP_K):
            row_copy(i, k).wait()
        return 0

    lax.fori_loop(0, te, drain, 0)


def _dispatch(hn2, dest_flat, cnt, start, p):
    te, d, r = p["te"], p["d"], p["r"]
    return pl.pallas_call(
        functools.partial(_dispatch_kernel, p=p),
        grid=(r // te,),
        in_specs=[
            pl.BlockSpec(memory_space=pltpu.SMEM),
            pl.BlockSpec(memory_space=pltpu.SMEM),
            pl.BlockSpec((te * TOP_K,), lambda i: (i,), memory_space=pltpu.SMEM),
            pl.BlockSpec((te, d), lambda i: (i, 0)),
        ],
        out_specs=pl.BlockSpec(memory_space=pl.ANY),
        out_shape=jax.ShapeDtypeStruct((p["n_slots"], d), F32),
        scratch_shapes=[pltpu.VMEM((p["rt"] // 2, d), F32), pltpu.SemaphoreType.DMA, pltpu.SemaphoreType.DMA],
        compiler_params=pltpu.CompilerParams(dimension_semantics=("arbitrary",)),
        name="dispatch",
    )(cnt, start, dest_flat, hn2)


def _moe_kernel(ce_ref, cts_ref, cnt_ref, used_ref, xs_ref, wg_ref, wu_ref, bg_ref, bu_ref, wd_ref, bd_ref,
                ys_ref, xbuf_ref, yacc_ref, stage_ref, wgb_ref, wub_ref, wdb_ref, xsem, ysem, *, p):
    rt, nf, nch = p["rt"], p["nf"], p["n_chunks_max"]
    c = pl.program_id(0)
    f = pl.program_id(1)
    nt = cnt_ref[c]
    c_next = jnp.minimum(c + 1, nch - 1)
    nt_next = jnp.where(c + 1 < nch, cnt_ref[c_next], 0)

    def x_copy(chunk, t, slot):
        src0 = pl.multiple_of((cts_ref[chunk] + t) * rt, rt)
        return pltpu.make_async_copy(xs_ref.at[pl.ds(src0, rt)], stage_ref.at[slot], xsem.at[slot])

    def y_copy(chunk, t):
        r0 = pl.multiple_of(t * rt, rt)
        dst0 = pl.multiple_of((cts_ref[chunk] + t) * rt, rt)
        return pltpu.make_async_copy(yacc_ref.at[pl.ds(r0, rt)], ys_ref.at[pl.ds(dst0, rt)], ysem)

    def start_first_two(chunk, n_tiles):
        x_copy(chunk, 0, 0).start()

        @pl.when(n_tiles > 1)
        def _():
            x_copy(chunk, 1, 1).start()

    @pl.when(nt > 0)
    def _():
        @pl.when(f == 0)
        def _():
            @pl.when(c == 0)
            def _():
                start_first_two(c, nt)

            def load(t, _):
                slot = t % 2
                x_copy(c, t, slot).wait()
                r0 = pl.multiple_of(t * rt, rt)
                xbuf_ref[pl.ds(r0, rt), :] = stage_ref[slot].astype(BF16)

                @pl.when(t + 2 < nt)
                def _():
                    x_copy(c, t + 2, slot).start()

                return 0

            lax.fori_loop(0, nt, load, 0)

        wgb_ref[...] = wg_ref[0].astype(BF16)
        wub_ref[...] = wu_ref[0].astype(BF16)
        wdb_ref[...] = wd_ref[0].astype(BF16)
        bg = bg_ref[0]
        bu = bu_ref[0]
        bd = bd_ref[0]

        @pl.when(f == 0)
        def _():
            @pl.when(c > 0)
            def _():
                def drain(t, _):
                    y_copy(c - 1, t).wait()
                    return 0

                lax.fori_loop(0, cnt_ref[jnp.maximum(c - 1, 0)], drain, 0)

            def init(t, _):
                r0 = pl.multiple_of(t * rt, rt)
                yacc_ref[pl.ds(r0, rt), :] = jnp.broadcast_to(bd, (rt, p["d"]))
                return 0

            lax.fori_loop(0, nt, init, 0)

        @pl.when((f == nf - 1) & (nt_next > 0))
        def _():
            start_first_two(c_next, nt_next)

        def sub(t):
            r0 = pl.multiple_of(t * rt, rt)
            xt = xbuf_ref[pl.ds(r0, rt), :]
            gate = jnp.dot(xt, wgb_ref[...], preferred_element_type=F32) + bg
            up = jnp.dot(xt, wub_ref[...], preferred_element_type=F32) + bu
            gate = jnp.minimum(gate, SWIGLU_LIMIT)
            up = jnp.clip(up, -SWIGLU_LIMIT, SWIGLU_LIMIT)
            act = (up + 1.0) * gate * jax.nn.sigmoid(SWIGLU_ALPHA * gate)
            yacc_ref[pl.ds(r0, rt), :] += jnp.dot(act.astype(BF16), wdb_ref[...], preferred_element_type=F32)

        def pair(i, _):
            sub(2 * i)
            sub(2 * i + 1)
            return 0

        lax.fori_loop(0, nt // 2, pair, 0)

        @pl.when(nt % 2 == 1)
        def _():
            sub(nt - 1)

        @pl.when(f == nf - 1)
        def _():
            def store(t, _):
                y_copy(c, t).start()
                return 0

            lax.fori_loop(0, nt, store, 0)

            @pl.when(nt_next == 0)
            def _():
                def drain(t, _):
                    y_copy(c, t).wait()
                    return 0

                lax.fori_loop(0, nt, drain, 0)

    @pl.when((c == nch - 1) & (f == nf - 1))
    def _():
        stage_ref[0] = jnp.zeros((rt, p["d"]), F32)

        def tail_copy(t):
            return pltpu.make_async_copy(stage_ref.at[0], ys_ref.at[pl.ds(pl.multiple_of(t * rt, rt), rt)], ysem)

        def tail_start(t, _):
            tail_copy(t).start()
            return 0

        def tail_wait(t, _):
            tail_copy(t).wait()
            return 0

        lax.fori_loop(used_ref[0], p["n_tiles_max"], tail_start, 0)
        lax.fori_loop(used_ref[0], p["n_tiles_max"], tail_wait, 0)


def _moe(xs, ce, cts, cnt, used, w_gate_up, b_gate_up, w_down, b_down, p):
    d, tf, nf, rt, cap = p["d"], p["tf"], p["nf"], p["rt"], p["cap"]
    n_exp = p["n_exp"]
    bgu = b_gate_up.reshape(n_exp, 1, 2 * p["d_ff"])
    bdn = b_down.reshape(n_exp, 1, d)

    def fi(c, f, cnt_r):
        return jnp.where(cnt_r[c] > 0, f, nf - 1)

    est = (2 * (2 * d * tf * 4 + tf * d * 4) + 3 * d * tf * 2 + cap * d * 2 + cap * d * 4 + 2 * rt * d * 4
           + 6 * rt * d * 4)
    grid_spec = pltpu.PrefetchScalarGridSpec(
        num_scalar_prefetch=4,
        grid=(p["n_chunks_max"], nf),
        in_specs=[
            pl.BlockSpec(memory_space=pl.ANY),
            pl.BlockSpec((1, d, tf), lambda c, f, ce_r, cts_r, cnt_r, u_r: (ce_r[c], 0, fi(c, f, cnt_r))),
            pl.BlockSpec((1, d, tf), lambda c, f, ce_r, cts_r, cnt_r, u_r: (ce_r[c], 0, nf + fi(c, f, cnt_r))),
            pl.BlockSpec((1, 1, tf), lambda c, f, ce_r, cts_r, cnt_r, u_r: (ce_r[c], 0, fi(c, f, cnt_r))),
            pl.BlockSpec((1, 1, tf), lambda c, f, ce_r, cts_r, cnt_r, u_r: (ce_r[c], 0, nf + fi(c, f, cnt_r))),
            pl.BlockSpec((1, tf, d), lambda c, f, ce_r, cts_r, cnt_r, u_r: (ce_r[c], fi(c, f, cnt_r), 0)),
            pl.BlockSpec((1, 1, d), lambda c, f, ce_r, cts_r, cnt_r, u_r: (ce_r[c], 0, 0)),
        ],
        out_specs=pl.BlockSpec(memory_space=pl.ANY),
        scratch_shapes=[
            pltpu.VMEM((cap, d), BF16), pltpu.VMEM((cap, d), F32), pltpu.VMEM((2, rt, d), F32),
            pltpu.VMEM((d, tf), BF16), pltpu.VMEM((d, tf), BF16), pltpu.VMEM((tf, d), BF16),
            pltpu.SemaphoreType.DMA((2,)), pltpu.SemaphoreType.DMA,
        ],
    )
    return pl.pallas_call(
        functools.partial(_moe_kernel, p=p),
        grid_spec=grid_spec,
        out_shape=jax.ShapeDtypeStruct((p["n_slots"], d), F32),
        compiler_params=pltpu.CompilerParams(
            dimension_semantics=("arbitrary", "arbitrary"), vmem_limit_bytes=_vmem_limit(est)),
        name="moe",
    )(ce, cts, cnt, used, xs, w_gate_up, w_gate_up, bgu, bgu, w_down, bdn)


def _combine_kernel(dest_ref, h1_ref, gate_ref, g_ref, ys_ref, o_ref, ybuf_ref, sem, *, p):
    tg = p["tg"]

    def row_copy(i, k):
        return pltpu.make_async_copy(ys_ref.at[pl.ds(dest_ref[i * TOP_K + k], 1)],
                                     ybuf_ref.at[k, pl.ds(i, 1)], sem)

    def issue(i, _):
        for k in range(TOP_K):
            row_copy(i, k).start(priority=k % 2)
        return 0

    lax.fori_loop(0, tg, issue, 0)

    def drain(i, _):
        for k in range(TOP_K):
            row_copy(i, k).wait()
        return 0

    lax.fori_loop(0, tg, drain, 0)

    lane = lax.broadcasted_iota(I32, (tg, LANES), 1)
    gates = gate_ref[...]
    h = h1_ref[...]
    for k in range(TOP_K):
        g_k = jnp.sum(jnp.where(lane == k, gates, 0.0), axis=1, keepdims=True)
        h = h + ybuf_ref[k] * g_k
    ms = jnp.mean(h * h, axis=-1, keepdims=True)
    o_ref[...] = (h * lax.rsqrt(ms + RMS_EPS)) * g_ref[...]


def _combine(ys, dest_flat, h1, gates, g_final, p):
    tg, d, r = p["tg"], p["d"], p["r"]
    return pl.pallas_call(
        functools.partial(_combine_kernel, p=p),
        grid=(r // tg,),
        in_specs=[
            pl.BlockSpec((tg * TOP_K,), lambda i: (i,), memory_space=pltpu.SMEM),
            pl.BlockSpec((tg, d), lambda i: (i, 0)),
            pl.BlockSpec((tg, LANES), lambda i: (i, 0)),
            pl.BlockSpec((1, d), lambda i: (0, 0)),
            pl.BlockSpec(memory_space=pl.ANY),
        ],
        out_specs=pl.BlockSpec((tg, d), lambda i: (i, 0)),
        out_shape=jax.ShapeDtypeStruct((r, d), F32),
        scratch_shapes=[pltpu.VMEM((TOP_K, tg, d), F32), pltpu.SemaphoreType.DMA],
        compiler_params=pltpu.CompilerParams(dimension_semantics=("arbitrary",)),
        name="combine",
    )(dest_flat, h1, gates, g_final, ys)


def kernel(x, meta_tokens, norm_mix_g, w_in, conv_w, sink, w_attn_o, w_conv_o, w_out, norm_ffn_g,
           router_w, router_b, w_gate_up, b_gate_up, w_down, b_down, norm_final_g):
    b, seq, d = x.shape
    n_exp, d_ff = w_down.shape[1], w_down.shape[2]
    assert norm_mix_g.shape[0] == 1 and meta_tokens.shape[0] == N_META
    p = _plan(b, seq, d, n_exp, d_ff)
    x2 = x.reshape(p["r"], d)

    qkv, rest, qkvm, restm = _inproj(x2, meta_tokens.astype(F32), norm_mix_g[0].reshape(1, d), w_in[0], p)
    attn_o = _attention(qkv, qkvm, sink[0].astype(F32), p)
    h1, hn2, ti, gates, rank, counts = _mixer(
        attn_o, rest, restm, x2, conv_w[0], w_attn_o[0].astype(BF16), w_conv_o[0].astype(BF16),
        w_out[0].astype(BF16), norm_ffn_g[0].reshape(1, d), router_w[0].astype(BF16),
        router_b[0].reshape(1, n_exp), p)
    row_start, cnt, start, ce, cts, cnt_tiles, used = _routing_tables(counts, p)
    dest = _dest(ti, rank, row_start, p)
    dest_flat = dest[:, :TOP_K].reshape(-1)
    xs = _dispatch(hn2, dest_flat, cnt, start, p)
    ys = _moe(xs, ce, cts, cnt_tiles, used, w_gate_up[0], b_gate_up[0], w_down[0], b_down[0], p)
    out = _combine(ys, dest_flat, h1, gates, norm_final_g.reshape(1, d), p)
    return out.reshape(b, seq, d)
```

```python
import functools

import numpy as np
import jax
import jax.numpy as jnp
from jax import lax
from jax.experimental import pallas as pl
from jax.experimental.pallas import tpu as pltpu

N_META = 16
BLK = 128
WINDOW = 128
HEAD_DIM = 64
Q_PER_KV = 4
ROT_DIM = HEAD_DIM // 4
ROPE_THETA = 500000.0
CONV_K = 3
TOP_K = 4
SWIGLU_ALPHA = 1.702
SWIGLU_LIMIT = 7.0
RMS_EPS = 1e-5

LANES = 128
SUBLANES = 8
VMEM_LIMIT_CAP = 60000 * 1024
MOE_ROW_TILE = 256
MOE_CHUNK_TILES = 5

F32 = jnp.float32
BF16 = jnp.bfloat16
I32 = jnp.int32


def _vmem_limit(nbytes):
    return int(min(VMEM_LIMIT_CAP, max(32 * 1024 * 1024, nbytes * 5 // 4 + (4 << 20))))


def _plan(b, seq, d, n_exp, d_ff):
    attn = (d // 128) * HEAD_DIM
    kvd = attn // Q_PER_KV
    conv = d // 2
    r = b * seq
    p = dict(b=b, seq=seq, d=d, n_exp=n_exp, d_ff=d_ff, attn=attn, kvd=kvd, conv=conv, r=r)
    p["nq"] = attn // HEAD_DIM
    p["nkv"] = kvd // HEAD_DIM
    p["in_dim"] = attn + 2 * kvd + 3 * conv + 2 * d
    p["tn"] = 2 * kvd
    p["tm"] = min(1024, seq)
    p["n_qkv_tiles"] = (attn + 2 * kvd) // p["tn"]
    p["n_col_tiles"] = p["in_dim"] // p["tn"]
    p["rest_w"] = 3 * conv + 2 * d
    p["tc"] = min(256, seq)
    p["te"] = min(256, seq)
    p["tg"] = min(128, seq)
    p["tf"] = min(512, d_ff)
    p["nf"] = d_ff // p["tf"]
    p["rt"] = MOE_ROW_TILE
    p["tpc"] = MOE_CHUNK_TILES
    p["cap"] = MOE_ROW_TILE * MOE_CHUNK_TILES
    p["n_tiles_max"] = (r * TOP_K) // p["rt"] + n_exp
    p["n_chunks_max"] = n_exp + (p["n_tiles_max"] - n_exp) // p["tpc"]
    p["n_slots"] = p["n_tiles_max"] * p["rt"]
    assert seq % p["tm"] == 0 and seq % p["tc"] == 0 and seq % BLK == 0
    assert attn % p["tn"] == 0 and p["in_dim"] % p["tn"] == 0 and kvd % LANES == 0
    assert (r * TOP_K) % p["rt"] == 0 and d_ff % p["tf"] == 0
    assert p["rt"] & (p["rt"] - 1) == 0, "the zero-fill decomposition needs a power-of-two row tile"
    return p


def _rope_tables(seq):
    half = ROT_DIM // 2
    pos = jnp.arange(N_META + seq, dtype=F32)
    inv_freq = ROPE_THETA ** (-jnp.arange(0, ROT_DIM, 2, dtype=F32) / ROT_DIM)
    ang = pos[:, None] * inv_freq[None, :]
    cos, sin = jnp.cos(ang), jnp.sin(ang)
    lane = np.arange(LANES) % HEAD_DIM
    idx = np.where(lane < ROT_DIM, lane % half, 0)
    rot = jnp.asarray(lane < ROT_DIM)
    sign = jnp.asarray(np.where(lane < half, -1.0, 1.0).astype(np.float32))
    cos_l = jnp.where(rot[None, :], cos[:, idx], 1.0)
    sin_l = jnp.where(rot[None, :], sin[:, idx] * sign[None, :], 0.0)
    tab = jnp.stack([cos_l, sin_l]).astype(F32)
    return tab[:, N_META:], tab[:, :N_META]


def _rope(t, cos, sin, n_cols):
    lane = lax.broadcasted_iota(I32, (t.shape[0], LANES), 1)
    first = (lane % HEAD_DIM) < (ROT_DIM // 2)
    outs = []
    for c in range(t.shape[1] // LANES):
        s = t[:, c * LANES:(c + 1) * LANES]
        if c * LANES < n_cols:
            partner = jnp.where(first, pltpu.roll(s, LANES - ROT_DIM // 2, 1), pltpu.roll(s, ROT_DIM // 2, 1))
            s = s * cos + partner * sin
        outs.append(s)
    return jnp.concatenate(outs, axis=1)


def _inproj_kernel(x_ref, meta_ref, g_ref, w_ref, csx_ref, csm_ref,
                   qkv_ref, rest_ref, qkvm_ref, restm_ref, hn_ref, *, p):
    tm, tn, kvd = p["tm"], p["tn"], p["kvd"]
    n_q_tiles = p["attn"] // tn
    n = pl.program_id(1)

    @pl.when(n == 0)
    def _():
        g = g_ref[...]

        def norm(v):
            ms = jnp.mean(v * v, axis=-1, keepdims=True)
            return ((v * lax.rsqrt(ms + RMS_EPS)) * g).astype(BF16)

        rows = min(128, tm)

        def body(i, _):
            r0 = pl.multiple_of(i * rows, rows)
            hn_ref[pl.ds(r0, rows), :] = norm(x_ref[pl.ds(r0, rows), :])
            return 0

        lax.fori_loop(0, tm // rows, body, 0)
        hn_ref[tm:tm + N_META, :] = norm(meta_ref[...])

    def project():
        res = jnp.dot(hn_ref[...], w_ref[...].astype(BF16), preferred_element_type=F32)
        return res[:tm], res[tm:]

    def store_qkv(n_cols, scale):
        res_x, res_m = project()
        rx = _rope(res_x, csx_ref[0], csx_ref[1], n_cols)
        rm = _rope(res_m, csm_ref[0], csm_ref[1], n_cols)
        if scale != 1.0:
            rx = rx * scale
        qkv_ref[...] = rx.astype(BF16)
        qkvm_ref[0] = rm.astype(BF16)

    @pl.when(n < n_q_tiles)
    def _():
        store_qkv(tn, HEAD_DIM ** -0.5)

    @pl.when(n == n_q_tiles)
    def _():
        store_qkv(kvd, 1.0)

    @pl.when(n >= p["n_qkv_tiles"])
    def _():
        res_x, res_m = project()
        rest_ref[...] = res_x.astype(BF16)
        restm_ref[0] = res_m.astype(BF16)


def _inproj(x2, meta, g, w_in, p):
    tm, tn, d, r = p["tm"], p["tn"], p["d"], p["r"]
    nm, nn, nqkv = r // tm, p["n_col_tiles"], p["n_qkv_tiles"]
    csx, csm = _rope_tables(p["seq"])
    spt = p["seq"] // tm
    qkv_w = nqkv * tn
    est = (2 * tm * d * 4 + (tm + 16) * d * 2 + 2 * d * tn * 4 + d * tn * 2 + 4 * tm * tn * 2
           + 4 * 2 * tm * LANES * 4 + 4 * (tm + 16) * tn * 4)
    return pl.pallas_call(
        functools.partial(_inproj_kernel, p=p),
        grid=(nm, nn),
        in_specs=[
            pl.BlockSpec((tm, d), lambda m, n: (m, 0)),
            pl.BlockSpec((N_META, d), lambda m, n: (0, 0)),
            pl.BlockSpec((1, d), lambda m, n: (0, 0)),
            pl.BlockSpec((d, tn), lambda m, n: (0, n)),
            pl.BlockSpec((2, tm, LANES), lambda m, n: (0, m % spt, 0)),
            pl.BlockSpec((2, N_META, LANES), lambda m, n: (0, 0, 0)),
        ],
        out_specs=[
            pl.BlockSpec((tm, tn), lambda m, n: (m, jnp.minimum(n, nqkv - 1))),
            pl.BlockSpec((tm, tn), lambda m, n: (m, jnp.maximum(n - nqkv, 0))),
            pl.BlockSpec((1, N_META, tn), lambda m, n: (m, 0, jnp.minimum(n, nqkv - 1))),
            pl.BlockSpec((1, N_META, tn), lambda m, n: (m, 0, jnp.maximum(n - nqkv, 0))),
        ],
        out_shape=[
            jax.ShapeDtypeStruct((r, qkv_w), BF16),
            jax.ShapeDtypeStruct((r, p["rest_w"]), BF16),
            jax.ShapeDtypeStruct((nm, N_META, qkv_w), BF16),
            jax.ShapeDtypeStruct((nm, N_META, p["rest_w"]), BF16),
        ],
        scratch_shapes=[pltpu.VMEM((tm + N_META, d), BF16)],
        compiler_params=pltpu.CompilerParams(
            dimension_semantics=("arbitrary", "arbitrary"), vmem_limit_bytes=_vmem_limit(est)),
        name="inproj",
    )(x2, meta, g, w_in, csx, csm)


def _attn_kernel(sink_ref, q_ref, k0_ref, k1_ref, k2_ref, v0_ref, v1_ref, v2_ref, km_ref, vm_ref,
                 o_ref, kcat_ref, vcat_ref, *, p):
    seq, nkv = p["seq"], p["nkv"]
    nband = 3 * BLK
    nkeys = nband + N_META
    n = pl.program_id(1)
    for j, (kr, vr) in enumerate(((k0_ref, v0_ref), (k1_ref, v1_ref), (k2_ref, v2_ref))):
        kcat_ref[j * BLK:(j + 1) * BLK, :] = kr[...]
        vcat_ref[j * BLK:(j + 1) * BLK, :] = vr[...]
    kcat_ref[nband:nkeys, :] = km_ref[...]
    vcat_ref[nband:nkeys, :] = vm_ref[...]

    qi = lax.broadcasted_iota(I32, (BLK, nkeys), 0)
    sj = lax.broadcasted_iota(I32, (BLK, nkeys), 1)
    kx = (n - 1) * BLK + sj
    dq = n * BLK + qi - kx
    visible = (sj >= nband) | ((jnp.abs(dq) <= WINDOW) & (kx >= 0) & (kx < seq))
    neg = jnp.finfo(F32).min
    ones = jnp.ones((nkeys, HEAD_DIM), BF16)
    gi = lax.broadcasted_iota(I32, (Q_PER_KV, 1, 1), 0)
    for h in range(nkv):
        kh = kcat_ref[:, h * HEAD_DIM:(h + 1) * HEAD_DIM]
        vh = jnp.concatenate([vcat_ref[:, h * HEAD_DIM:(h + 1) * HEAD_DIM], ones], axis=1)
        heads = [h * Q_PER_KV + g for g in range(Q_PER_KV)]
        qs = jnp.concatenate([q_ref[:, hd * HEAD_DIM:(hd + 1) * HEAD_DIM] for hd in heads], axis=0)
        s = lax.dot_general(qs, kh, (((1,), (1,)), ((), ())), preferred_element_type=F32)
        s = jnp.where(visible[None], s.reshape(Q_PER_KV, BLK, nkeys), neg)
        snk = jnp.zeros((Q_PER_KV, 1, 1), F32)
        for g, hd in enumerate(heads):
            snk = jnp.where(gi == g, sink_ref[hd], snk)
        m = jnp.maximum(jnp.max(s, axis=2, keepdims=True), snk)
        e = jnp.exp(s - m).reshape(Q_PER_KV * BLK, nkeys)
        pv = jnp.dot(e.astype(BF16), vh, preferred_element_type=F32)
        pv = pv.reshape(Q_PER_KV, BLK, 2 * HEAD_DIM)
        o = pv[:, :, :HEAD_DIM] / (pv[:, :, HEAD_DIM:HEAD_DIM + 1] + jnp.exp(snk - m))
        for g, hd in enumerate(heads):
            o_ref[:, hd * HEAD_DIM:(hd + 1) * HEAD_DIM] = o[g].astype(BF16)


def _attention(qkv, qkvm, sink, p):
    attn, kvd, seq, b = p["attn"], p["kvd"], p["seq"], p["b"]
    nbx = seq // BLK
    kc = attn // kvd
    nkeys = 3 * BLK + N_META

    def kv_spec(off, col):
        return pl.BlockSpec((BLK, kvd), lambda bi, n: (bi * nbx + jnp.clip(n + off, 0, nbx - 1), col))

    return pl.pallas_call(
        functools.partial(_attn_kernel, p=p),
        grid=(b, nbx),
        in_specs=[
            pl.BlockSpec(memory_space=pltpu.SMEM),
            pl.BlockSpec((BLK, attn), lambda bi, n: (bi * nbx + n, 0)),
            kv_spec(-1, kc), kv_spec(0, kc), kv_spec(1, kc),
            kv_spec(-1, kc + 1), kv_spec(0, kc + 1), kv_spec(1, kc + 1),
            pl.BlockSpec((None, N_META, kvd), lambda bi, n: (0, 0, kc)),
            pl.BlockSpec((None, N_META, kvd), lambda bi, n: (0, 0, kc + 1)),
        ],
        out_specs=pl.BlockSpec((BLK, attn), lambda bi, n: (bi * nbx + n, 0)),
        out_shape=jax.ShapeDtypeStruct((p["r"], attn), BF16),
        scratch_shapes=[pltpu.VMEM((nkeys, kvd), BF16), pltpu.VMEM((nkeys, kvd), BF16)],
        compiler_params=pltpu.CompilerParams(dimension_semantics=("arbitrary", "arbitrary")),
        name="attn",
    )(sink, qkv, qkv, qkv, qkv, qkv, qkv, qkv, qkvm, qkvm)


def _lane_pack(cols, rows, dtype):
    lane = lax.broadcasted_iota(I32, (rows, LANES), 1)
    out = jnp.zeros((rows, LANES), dtype)
    for k, c in enumerate(cols):
        out = jnp.where(lane == k, c.astype(dtype), out)
    return out


def _mixer_kernel(attn_ref, rest_ref, prev_ref, next_ref, restm_ref, x_ref, cw_ref, wao_ref, wco_ref,
                  wout_ref, g_ref, rw_ref, rb_ref,
                  h1_ref, hn2_ref, ti_ref, gate_ref, rank_ref, cnt_ref, carry_ref, *, p):
    tc, conv, d, n_exp, seq = p["tc"], p["conv"], p["d"], p["n_exp"], p["seq"]
    i = pl.program_id(0)
    tiles_per_seq = seq // tc
    is_first = (i % tiles_per_seq) == 0
    is_last = (i % tiles_per_seq) == tiles_per_seq - 1
    o_ch, o_cb, o_cc, o_ga, o_gc = 0, conv, 2 * conv, 3 * conv, 3 * conv + d

    @pl.when(i == 0)
    def _():
        carry_ref[...] = jnp.zeros_like(carry_ref)

    def u_of(ref):
        return ref[:, o_cc:o_cc + conv].astype(F32) * ref[:, o_ch:o_ch + conv].astype(F32)

    u = u_of(rest_ref)
    last = N_META - 1
    u_prev = jnp.where(is_first, u_of(restm_ref)[last:last + 1], u_of(prev_ref)[last:last + 1])
    u_next = jnp.where(is_last, 0.0, u_of(next_ref)[0:1])
    row = lax.broadcasted_iota(I32, (tc, conv), 0)
    u_m1 = jnp.where(row == 0, u_prev, pltpu.roll(u, 1, 0))
    u_p1 = jnp.where(row == tc - 1, u_next, pltpu.roll(u, tc - 1, 0))
    cw = cw_ref[...]
    cv = u_m1 * cw[0:1] + u * cw[1:2] + u_p1 * cw[2:3]
    yc_in = (rest_ref[:, o_cb:o_cb + conv].astype(F32) * cv).astype(BF16)
    y_conv = jnp.dot(yc_in, wco_ref[...], preferred_element_type=F32)
    y_attn = jnp.dot(attn_ref[...], wao_ref[...], preferred_element_type=F32)
    g_a = rest_ref[:, o_ga:o_ga + d].astype(F32)
    g_c = rest_ref[:, o_gc:o_gc + d].astype(F32)
    merged = jax.nn.sigmoid(g_a) * y_attn + jax.nn.sigmoid(g_c) * y_conv
    h1 = x_ref[...] + jnp.dot(merged.astype(BF16), wout_ref[...], preferred_element_type=F32)
    h1_ref[...] = h1
    ms = jnp.mean(h1 * h1, axis=-1, keepdims=True)
    hn2 = (h1 * lax.rsqrt(ms + RMS_EPS)) * g_ref[...]
    hn2_ref[...] = hn2

    logits = jnp.dot(hn2.astype(BF16), rw_ref[...], preferred_element_type=F32) + rb_ref[...]
    lane = lax.broadcasted_iota(I32, (tc, n_exp), 1).astype(F32)
    sel = jnp.zeros((tc, n_exp), F32)
    tv, ti = [], []
    cur = logits
    for _ in range(TOP_K):
        m = jnp.max(cur, axis=1, keepdims=True)
        idx = jnp.min(jnp.where(cur == m, lane, float(n_exp)), axis=1, keepdims=True)
        hit = lane == idx
        tv.append(m)
        ti.append(idx)
        sel = jnp.where(hit, 1.0, sel)
        cur = jnp.where(hit, -jnp.inf, cur)
    ex = [jnp.exp(v - tv[0]) for v in tv]
    tot = ex[0] + ex[1] + ex[2] + ex[3]
    gates = [e / tot for e in ex]

    r_i = lax.broadcasted_iota(I32, (tc, tc), 0)
    c_i = lax.broadcasted_iota(I32, (tc, tc), 1)
    lower = jnp.where(r_i > c_i, 1.0, 0.0).astype(BF16)
    before = jnp.dot(lower, sel.astype(BF16), preferred_element_type=F32) + carry_ref[0:1, 0:n_exp]
    ranks = [jnp.sum(jnp.where(lane == t, before, 0.0), axis=1, keepdims=True) for t in ti]
    carry_ref[0:1, 0:n_exp] = carry_ref[0:1, 0:n_exp] + jnp.sum(sel, axis=0, keepdims=True)

    ti_ref[...] = _lane_pack(ti, tc, I32)
    gate_ref[...] = _lane_pack(gates, tc, F32)
    rank_ref[...] = _lane_pack(ranks, tc, I32)
    cnt_ref[...] = carry_ref[...]


def _mixer(attn_o, rest, restm, x2, conv_w, wao, wco, wout, g_ffn, router_w, router_b, p):
    tc, d, r, rw, n_exp = p["tc"], p["d"], p["r"], p["rest_w"], p["n_exp"]
    nt = r // tc
    sub = tc // N_META
    n16 = r // N_META
    const = lambda shape: pl.BlockSpec(shape, lambda i: (0,) * len(shape))
    est = (2 * (tc * p["attn"] * 2 + tc * rw * 2 + tc * d * 4 + 3 * N_META * rw * 2)
           + 2 * (p["attn"] * d + p["conv"] * d + d * d + d * n_exp) * 2
           + 2 * (2 * tc * d * 4 + 3 * tc * LANES * 4) + 10 * tc * d * 4)
    return pl.pallas_call(
        functools.partial(_mixer_kernel, p=p),
        grid=(nt,),
        in_specs=[
            pl.BlockSpec((tc, p["attn"]), lambda i: (i, 0)),
            pl.BlockSpec((tc, rw), lambda i: (i, 0)),
            pl.BlockSpec((N_META, rw), lambda i: (jnp.maximum(i * sub - 1, 0), 0)),
            pl.BlockSpec((N_META, rw), lambda i: (jnp.minimum((i + 1) * sub, n16 - 1), 0)),
            pl.BlockSpec((None, N_META, rw), lambda i: (0, 0, 0)),
            pl.BlockSpec((tc, d), lambda i: (i, 0)),
            const((CONV_K, p["conv"])),
            const((p["attn"], d)), const((p["conv"], d)), const((d, d)),
            const((1, d)), const((d, n_exp)), const((1, n_exp)),
        ],
        out_specs=[
            pl.BlockSpec((tc, d), lambda i: (i, 0)),
            pl.BlockSpec((tc, d), lambda i: (i, 0)),
            pl.BlockSpec((tc, LANES), lambda i: (i, 0)),
            pl.BlockSpec((tc, LANES), lambda i: (i, 0)),
            pl.BlockSpec((tc, LANES), lambda i: (i, 0)),
            pl.BlockSpec((8, LANES), lambda i: (0, 0)),
        ],
        out_shape=[
            jax.ShapeDtypeStruct((r, d), F32),
            jax.ShapeDtypeStruct((r, d), F32),
            jax.ShapeDtypeStruct((r, LANES), I32),
            jax.ShapeDtypeStruct((r, LANES), F32),
            jax.ShapeDtypeStruct((r, LANES), I32),
            jax.ShapeDtypeStruct((8, LANES), F32),
        ],
        scratch_shapes=[pltpu.VMEM((8, LANES), F32)],
        compiler_params=pltpu.CompilerParams(
            dimension_semantics=("arbitrary",), vmem_limit_bytes=_vmem_limit(est)),
        name="mixer",
    )(attn_o, rest, rest, rest, restm, x2, conv_w, wao, wco, wout, g_ffn, router_w, router_b)


def _dest_kernel(ti_ref, rank_ref, start_ref, dest_ref):
    rows = ti_ref.shape[0]
    lane = lax.broadcasted_iota(I32, (rows, LANES), 1).astype(F32)
    ti = ti_ref[...].astype(F32)
    rank = rank_ref[...].astype(F32)
    start = start_ref[0:1, :].astype(F32)
    out = jnp.zeros((rows, LANES), F32)
    for k in range(TOP_K):
        e_k = jnp.sum(jnp.where(lane == k, ti, 0.0), axis=1, keepdims=True)
        r_k = jnp.sum(jnp.where(lane == k, rank, 0.0), axis=1, keepdims=True)
        s_k = jnp.sum(jnp.where(lane == e_k, start, 0.0), axis=1, keepdims=True)
        out = jnp.where(lane == k, s_k + r_k, out)
    dest_ref[...] = out.astype(I32)


def _dest(ti, rank, row_start, p):
    r = p["r"]
    rows = min(1024, r)
    return pl.pallas_call(
        _dest_kernel,
        grid=(r // rows,),
        in_specs=[pl.BlockSpec((rows, LANES), lambda i: (i, 0)),
                  pl.BlockSpec((rows, LANES), lambda i: (i, 0)),
                  pl.BlockSpec((8, LANES), lambda i: (0, 0))],
        out_specs=pl.BlockSpec((rows, LANES), lambda i: (i, 0)),
        out_shape=jax.ShapeDtypeStruct((r, LANES), I32),
        name="dest",
    )(ti, rank, row_start)


def _routing_tables(counts, p):
    rt, tpc, n_exp, nch = p["rt"], p["tpc"], p["n_exp"], p["n_chunks_max"]
    cnt = counts[0, :n_exp].astype(I32)
    ntile = (cnt + rt - 1) // rt
    tile_start = jnp.cumsum(ntile) - ntile
    nchunk = (ntile + tpc - 1) // tpc
    chunk_end = jnp.cumsum(nchunk)
    c = jnp.arange(nch, dtype=I32)
    ce = jnp.minimum(jnp.sum((chunk_end[None, :] <= c[:, None]).astype(I32), axis=1), n_exp - 1)
    first = c - (chunk_end - nchunk)[ce]
    c_nt = jnp.clip(ntile[ce] - first * tpc, 0, tpc)
    c_nt = jnp.where(c < chunk_end[-1], c_nt, 0)
    c_ts = tile_start[ce] + first * tpc
    last_e = ce[jnp.maximum(chunk_end[-1] - 1, 0)]
    ce = jnp.where(c_nt > 0, ce, last_e)
    row_start = jnp.zeros((8, LANES), I32).at[0, :n_exp].set(tile_start * rt)
    used = jnp.sum(ntile).reshape(1).astype(I32)
    return row_start, cnt, tile_start * rt, ce.astype(I32), c_ts.astype(I32), c_nt.astype(I32), used


def _dispatch_kernel(cnt_ref, start_ref, dest_ref, hn_ref, xs_ref, zero_ref, sem, zsem, *, p):
    te, n_exp, rt, d = p["te"], p["n_exp"], p["rt"], p["d"]
    step = pl.program_id(0)

    @pl.when(step == 0)
    def _():
        zero_ref[...] = jnp.zeros_like(zero_ref)

        def zero_rows(dst0, size):
            cp = pltpu.make_async_copy(zero_ref.at[pl.ds(0, size)], xs_ref.at[pl.ds(dst0, size)], zsem)
            cp.start()
            cp.wait()

        def per_expert(e, _):
            cnt = cnt_ref[e]
            cur = start_ref[e] + cnt
            pad = (rt - (cnt & (rt - 1))) & (rt - 1)
            head = (SUBLANES - (cur & (SUBLANES - 1))) & (SUBLANES - 1)
            for j in range(SUBLANES - 1):
                @pl.when(j < head)
                def _(j=j):
                    zero_rows(cur + j, 1)
            cur = cur + head
            rem = pad - head
            size = SUBLANES
            while size < rt:
                @pl.when((rem & size) != 0)
                def _(cur=cur, size=size):
                    zero_rows(pl.multiple_of(cur, SUBLANES), size)
                cur = cur + (rem & size)
                size *= 2
            return 0

        lax.fori_loop(0, n_exp, per_expert, 0)

        half = rt // 2
        used = (start_ref[n_exp - 1] + cnt_ref[n_exp - 1] + rt - 1) // rt

        def tail_copy(t, j):
            r0 = pl.multiple_of(t * rt + j * half, half)
            return pltpu.make_async_copy(zero_ref, xs_ref.at[pl.ds(r0, half)], zsem)

        def tail_start(t, _):
            tail_copy(t, 0).start()
            tail_copy(t, 1).start()
            return 0

        def tail_wait(t, _):
            tail_copy(t, 0).wait()
            tail_copy(t, 1).wait()
            return 0

        lax.fori_loop(used, p["n_tiles_max"], tail_start, 0)
        lax.fori_loop(used, p["n_tiles_max"], tail_wait, 0)

    def row_copy(i, k):
        return pltpu.make_async_copy(hn_ref.at[pl.ds(i, 1)], xs_ref.at[pl.ds(dest_ref[i * TOP_K + k], 1)], sem)

    def issue(i, _):
        for k in range(TOP_K):
            row_copy(i, k).start(priority=k % 2)
        return 0

    lax.fori_loop(0, te, issue, 0)

    def drain(i, _):
        for k in range(TOP_K):
            row_copy(i, k).wait()
        return 0

    lax.fori_loop(0, te, drain, 0)


def _dispatch(hn2, dest_flat, cnt, start, p):
    te, d, r = p["te"], p["d"], p["r"]
    return pl.pallas_call(
        functools.partial(_dispatch_kernel, p=p),
        grid=(r // te,),
        in_specs=[
            pl.BlockSpec(memory_space=pltpu.SMEM),
            pl.BlockSpec(memory_space=pltpu.SMEM),
            pl.BlockSpec((te * TOP_K,), lambda i: (i,), memory_space=pltpu.SMEM),
            pl.BlockSpec((te, d), lambda i: (i, 0)),
        ],
        out_specs=pl.BlockSpec(memory_space=pl.ANY),
        out_shape=jax.ShapeDtypeStruct((p["n_slots"], d), F32),
        scratch_shapes=[pltpu.VMEM((p["rt"] // 2, d), F32), pltpu.SemaphoreType.DMA, pltpu.SemaphoreType.DMA],
        compiler_params=pltpu.CompilerParams(dimension_semantics=("arbitrary",)),
        name="dispatch",
    )(cnt, start, dest_flat, hn2)


def _moe_kernel(ce_ref, cts_ref, cnt_ref, used_ref, xs_ref, wg_ref, wu_ref, bg_ref, bu_ref, wd_ref, bd_ref,
                ys_ref, xbuf_ref, yacc_ref, stage_ref, xsem, ysem, *, p):
    rt, nf, nch = p["rt"], p["nf"], p["n_chunks_max"]
    c = pl.program_id(0)
    f = pl.program_id(1)
    nt = cnt_ref[c]
    c_next = jnp.minimum(c + 1, nch - 1)
    nt_next = jnp.where(c + 1 < nch, cnt_ref[c_next], 0)

    def x_copy(chunk, t, slot):
        src0 = pl.multiple_of((cts_ref[chunk] + t) * rt, rt)
        return pltpu.make_async_copy(xs_ref.at[pl.ds(src0, rt)], stage_ref.at[slot], xsem.at[slot])

    def y_copy(chunk, t):
        r0 = pl.multiple_of(t * rt, rt)
        dst0 = pl.multiple_of((cts_ref[chunk] + t) * rt, rt)
        return pltpu.make_async_copy(yacc_ref.at[pl.ds(r0, rt)], ys_ref.at[pl.ds(dst0, rt)], ysem)

    def start_first_two(chunk, n_tiles):
        x_copy(chunk, 0, 0).start()

        @pl.when(n_tiles > 1)
        def _():
            x_copy(chunk, 1, 1).start()

    @pl.when(nt > 0)
    def _():
        @pl.when(f == 0)
        def _():
            @pl.when(c == 0)
            def _():
                start_first_two(c, nt)

            def load(t, _):
                slot = t % 2
                x_copy(c, t, slot).wait()
                r0 = pl.multiple_of(t * rt, rt)
                xbuf_ref[pl.ds(r0, rt), :] = stage_ref[slot].astype(BF16)

                @pl.when(t + 2 < nt)
                def _():
                    x_copy(c, t + 2, slot).start()

                return 0

            lax.fori_loop(0, nt, load, 0)

        bg = bg_ref[0]
        bu = bu_ref[0]
        bd = bd_ref[0]

        @pl.when(f == 0)
        def _():
            @pl.when(c > 0)
            def _():
                def drain(t, _):
                    y_copy(c - 1, t).wait()
                    return 0

                lax.fori_loop(0, cnt_ref[jnp.maximum(c - 1, 0)], drain, 0)

            def init(t, _):
                r0 = pl.multiple_of(t * rt, rt)
                yacc_ref[pl.ds(r0, rt), :] = jnp.broadcast_to(bd, (rt, p["d"]))
                return 0

            lax.fori_loop(0, nt, init, 0)

        @pl.when((f == nf - 1) & (nt_next > 0))
        def _():
            start_first_two(c_next, nt_next)

        def sub(t):
            r0 = pl.multiple_of(t * rt, rt)
            xt = xbuf_ref[pl.ds(r0, rt), :]
            gate = jnp.dot(xt, wg_ref[0].astype(BF16), preferred_element_type=F32) + bg
            up = jnp.dot(xt, wu_ref[0].astype(BF16), preferred_element_type=F32) + bu
            gate = jnp.minimum(gate, SWIGLU_LIMIT)
            up = jnp.clip(up, -SWIGLU_LIMIT, SWIGLU_LIMIT)
            act = (up + 1.0) * gate * jax.nn.sigmoid(SWIGLU_ALPHA * gate)
            yacc_ref[pl.ds(r0, rt), :] += jnp.dot(act.astype(BF16), wd_ref[0].astype(BF16),
                                                  preferred_element_type=F32)

        has_triple = (nt % 2 == 1) & (nt >= 3)
        n_pair_trips = (nt - jnp.where(has_triple, 3, 0)) // 2

        def pair(i, _):
            sub(2 * i)
            sub(2 * i + 1)
            return 0

        lax.fori_loop(0, n_pair_trips, pair, 0)

        @pl.when(has_triple)
        def _():
            sub(nt - 3)
            sub(nt - 2)
            sub(nt - 1)

        @pl.when(nt == 1)
        def _():
            sub(0)

        @pl.when(f == nf - 1)
        def _():
            def store(t, _):
                y_copy(c, t).start()
                return 0

            lax.fori_loop(0, nt, store, 0)

            @pl.when(nt_next == 0)
            def _():
                def drain(t, _):
                    y_copy(c, t).wait()
                    return 0

                lax.fori_loop(0, nt, drain, 0)

    @pl.when((c == nch - 1) & (f == nf - 1))
    def _():
        stage_ref[0] = jnp.zeros((rt, p["d"]), F32)

        def tail_copy(t):
            return pltpu.make_async_copy(stage_ref.at[0], ys_ref.at[pl.ds(pl.multiple_of(t * rt, rt), rt)], ysem)

        def tail_start(t, _):
            tail_copy(t).start()
            return 0

        def tail_wait(t, _):
            tail_copy(t).wait()
            return 0

        lax.fori_loop(used_ref[0], p["n_tiles_max"], tail_start, 0)
        lax.fori_loop(used_ref[0], p["n_tiles_max"], tail_wait, 0)


def _moe(xs, ce, cts, cnt, used, w_gate_up, b_gate_up, w_down, b_down, p):
    d, tf, nf, rt, cap = p["d"], p["tf"], p["nf"], p["rt"], p["cap"]
    n_exp = p["n_exp"]
    bgu = b_gate_up.reshape(n_exp, 1, 2 * p["d_ff"])
    bdn = b_down.reshape(n_exp, 1, d)

    def fi(c, f, cnt_r):
        return jnp.where(cnt_r[c] > 0, f, nf - 1)

    est = (2 * (2 * d * tf * 4 + tf * d * 4) + cap * d * 2 + cap * d * 4 + 2 * rt * d * 4 + 6 * rt * d * 4)
    grid_spec = pltpu.PrefetchScalarGridSpec(
        num_scalar_prefetch=4,
        grid=(p["n_chunks_max"], nf),
        in_specs=[
            pl.BlockSpec(memory_space=pl.ANY),
            pl.BlockSpec((1, d, tf), lambda c, f, ce_r, cts_r, cnt_r, u_r: (ce_r[c], 0, fi(c, f, cnt_r))),
            pl.BlockSpec((1, d, tf), lambda c, f, ce_r, cts_r, cnt_r, u_r: (ce_r[c], 0, nf + fi(c, f, cnt_r))),
            pl.BlockSpec((1, 1, tf), lambda c, f, ce_r, cts_r, cnt_r, u_r: (ce_r[c], 0, fi(c, f, cnt_r))),
            pl.BlockSpec((1, 1, tf), lambda c, f, ce_r, cts_r, cnt_r, u_r: (ce_r[c], 0, nf + fi(c, f, cnt_r))),
            pl.BlockSpec((1, tf, d), lambda c, f, ce_r, cts_r, cnt_r, u_r: (ce_r[c], fi(c, f, cnt_r), 0)),
            pl.BlockSpec((1, 1, d), lambda c, f, ce_r, cts_r, cnt_r, u_r: (ce_r[c], 0, 0)),
        ],
        out_specs=pl.BlockSpec(memory_space=pl.ANY),
        scratch_shapes=[
            pltpu.VMEM((cap, d), BF16), pltpu.VMEM((cap, d), F32), pltpu.VMEM((2, rt, d), F32),
            pltpu.SemaphoreType.DMA((2,)), pltpu.SemaphoreType.DMA,
        ],
    )
    return pl.pallas_call(
        functools.partial(_moe_kernel, p=p),
        grid_spec=grid_spec,
        out_shape=jax.ShapeDtypeStruct((p["n_slots"], d), F32),
        compiler_params=pltpu.CompilerParams(
            dimension_semantics=("arbitrary", "arbitrary"), vmem_limit_bytes=_vmem_limit(est)),
        name="moe",
    )(ce, cts, cnt, used, xs, w_gate_up, w_gate_up, bgu, bgu, w_down, bdn)


def _combine_kernel(dest_ref, h1_ref, gate_ref, g_ref, ys_ref, o_ref, ybuf_ref, sem, *, p):
    tg = p["tg"]

    def row_copy(i, k):
        return pltpu.make_async_copy(ys_ref.at[pl.ds(dest_ref[i * TOP_K + k], 1)],
                                     ybuf_ref.at[k, pl.ds(i, 1)], sem)

    def issue(i, _):
        for k in range(TOP_K):
            row_copy(i, k).start(priority=k % 2)
        return 0

    lax.fori_loop(0, tg, issue, 0)

    def drain(i, _):
        for k in range(TOP_K):
            row_copy(i, k).wait()
        return 0

    lax.fori_loop(0, tg, drain, 0)

    lane = lax.broadcasted_iota(I32, (tg, LANES), 1)
    gates = gate_ref[...]
    h = h1_ref[...]
    for k in range(TOP_K):
        g_k = jnp.sum(jnp.where(lane == k, gates, 0.0), axis=1, keepdims=True)
        h = h + ybuf_ref[k] * g_k
    ms = jnp.mean(h * h, axis=-1, keepdims=True)
    o_ref[...] = (h * lax.rsqrt(ms + RMS_EPS)) * g_ref[...]


def _combine(ys, dest_flat, h1, gates, g_final, p):
    tg, d, r = p["tg"], p["d"], p["r"]
    return pl.pallas_call(
        functools.partial(_combine_kernel, p=p),
        grid=(r // tg,),
        in_specs=[
            pl.BlockSpec((tg * TOP_K,), lambda i: (i,), memory_space=pltpu.SMEM),
            pl.BlockSpec((tg, d), lambda i: (i, 0)),
            pl.BlockSpec((tg, LANES), lambda i: (i, 0)),
            pl.BlockSpec((1, d), lambda i: (0, 0)),
            pl.BlockSpec(memory_space=pl.ANY),
        ],
        out_specs=pl.BlockSpec((tg, d), lambda i: (i, 0)),
        out_shape=jax.ShapeDtypeStruct((r, d), F32),
        scratch_shapes=[pltpu.VMEM((TOP_K, tg, d), F32), pltpu.SemaphoreType.DMA],
        compiler_params=pltpu.CompilerParams(dimension_semantics=("arbitrary",)),
        name="combine",
    )(dest_flat, h1, gates, g_final, ys)


def kernel(x, meta_tokens, norm_mix_g, w_in, conv_w, sink, w_attn_o, w_conv_o, w_out, norm_ffn_g,
           router_w, router_b, w_gate_up, b_gate_up, w_down, b_down, norm_final_g):
    b, seq, d = x.shape
    n_exp, d_ff = w_down.shape[1], w_down.shape[2]
    assert norm_mix_g.shape[0] == 1 and meta_tokens.shape[0] == N_META
    p = _plan(b, seq, d, n_exp, d_ff)
    x2 = x.reshape(p["r"], d)

    qkv, rest, qkvm, restm = _inproj(x2, meta_tokens.astype(F32), norm_mix_g[0].reshape(1, d), w_in[0], p)
    attn_o = _attention(qkv, qkvm, sink[0].astype(F32), p)
    h1, hn2, ti, gates, rank, counts = _mixer(
        attn_o, rest, restm, x2, conv_w[0], w_attn_o[0].astype(BF16), w_conv_o[0].astype(BF16),
        w_out[0].astype(BF16), norm_ffn_g[0].reshape(1, d), router_w[0].astype(BF16),
        router_b[0].reshape(1, n_exp), p)
    row_start, cnt, start, ce, cts, cnt_tiles, used = _routing_tables(counts, p)
    dest = _dest(ti, rank, row_start, p)
    dest_flat = dest[:, :TOP_K].reshape(-1)
    xs = _dispatch(hn2, dest_flat, cnt, start, p)
    ys = _moe(xs, ce, cts, cnt_tiles, used, w_gate_up[0], b_gate_up[0], w_down[0], b_down[0], p)
    out = _combine(ys, dest_flat, h1, gates, norm_final_g.reshape(1, d), p)
    return out.reshape(b, seq, d)
```

```python
import functools

import numpy as np
import jax
import jax.numpy as jnp
from jax import lax
from jax.experimental import pallas as pl
from jax.experimental.pallas import tpu as pltpu

N_META = 16
BLK = 128
WINDOW = 128
HEAD_DIM = 64
Q_PER_KV = 4
ROT_DIM = HEAD_DIM // 4
ROPE_THETA = 500000.0
CONV_K = 3
TOP_K = 4
SWIGLU_ALPHA = 1.702
SWIGLU_LIMIT = 7.0
RMS_EPS = 1e-5

LANES = 128
SUBLANES = 8
VMEM_LIMIT_CAP = 60000 * 1024
MOE_ROW_TILE = 256
MOE_CHUNK_TILES = 5

F32 = jnp.float32
BF16 = jnp.bfloat16
I32 = jnp.int32


def _vmem_limit(nbytes):
    return int(min(VMEM_LIMIT_CAP, max(32 * 1024 * 1024, nbytes * 5 // 4 + (4 << 20))))


def _plan(b, seq, d, n_exp, d_ff):
    attn = (d // 128) * HEAD_DIM
    kvd = attn // Q_PER_KV
    conv = d // 2
    r = b * seq
    p = dict(b=b, seq=seq, d=d, n_exp=n_exp, d_ff=d_ff, attn=attn, kvd=kvd, conv=conv, r=r)
    p["nq"] = attn // HEAD_DIM
    p["nkv"] = kvd // HEAD_DIM
    p["in_dim"] = attn + 2 * kvd + 3 * conv + 2 * d
    p["tn"] = 2 * kvd
    p["tm"] = min(1024, seq)
    p["n_qkv_tiles"] = (attn + 2 * kvd) // p["tn"]
    p["n_col_tiles"] = p["in_dim"] // p["tn"]
    p["rest_w"] = 3 * conv + 2 * d
    p["tc"] = min(256, seq)
    p["te"] = min(256, seq)
    p["tg"] = min(128, seq)
    p["tf"] = min(512, d_ff)
    p["nf"] = d_ff // p["tf"]
    p["rt"] = MOE_ROW_TILE
    p["tpc"] = MOE_CHUNK_TILES
    p["cap"] = MOE_ROW_TILE * MOE_CHUNK_TILES
    p["n_tiles_max"] = (r * TOP_K) // p["rt"] + n_exp
    p["n_chunks_max"] = n_exp + (p["n_tiles_max"] - n_exp) // p["tpc"]
    p["n_slots"] = p["n_tiles_max"] * p["rt"]
    assert seq % p["tm"] == 0 and seq % p["tc"] == 0 and seq % BLK == 0
    assert attn % p["tn"] == 0 and p["in_dim"] % p["tn"] == 0 and kvd % LANES == 0
    assert (r * TOP_K) % p["rt"] == 0 and d_ff % p["tf"] == 0
    assert p["rt"] & (p["rt"] - 1) == 0, "the zero-fill decomposition needs a power-of-two row tile"
    return p


def _rope_tables(seq):
    half = ROT_DIM // 2
    pos = jnp.arange(N_META + seq, dtype=F32)
    inv_freq = ROPE_THETA ** (-jnp.arange(0, ROT_DIM, 2, dtype=F32) / ROT_DIM)
    ang = pos[:, None] * inv_freq[None, :]
    cos, sin = jnp.cos(ang), jnp.sin(ang)
    lane = np.arange(LANES) % HEAD_DIM
    idx = np.where(lane < ROT_DIM, lane % half, 0)
    rot = jnp.asarray(lane < ROT_DIM)
    sign = jnp.asarray(np.where(lane < half, -1.0, 1.0).astype(np.float32))
    cos_l = jnp.where(rot[None, :], cos[:, idx], 1.0)
    sin_l = jnp.where(rot[None, :], sin[:, idx] * sign[None, :], 0.0)
    tab = jnp.stack([cos_l, sin_l]).astype(F32)
    return tab[:, N_META:], tab[:, :N_META]


def _rope(t, cos, sin, n_cols):
    lane = lax.broadcasted_iota(I32, (t.shape[0], LANES), 1)
    first = (lane % HEAD_DIM) < (ROT_DIM // 2)
    outs = []
    for c in range(t.shape[1] // LANES):
        s = t[:, c * LANES:(c + 1) * LANES]
        if c * LANES < n_cols:
            partner = jnp.where(first, pltpu.roll(s, LANES - ROT_DIM // 2, 1), pltpu.roll(s, ROT_DIM // 2, 1))
            s = s * cos + partner * sin
        outs.append(s)
    return jnp.concatenate(outs, axis=1)


def _inproj_kernel(x_ref, meta_ref, g_ref, w_ref, csx_ref, csm_ref,
                   qkv_ref, rest_ref, qkvm_ref, restm_ref, hn_ref, *, p):
    tm, tn, kvd = p["tm"], p["tn"], p["kvd"]
    n_q_tiles = p["attn"] // tn
    n = pl.program_id(1)

    @pl.when(n == 0)
    def _():
        g = g_ref[...]

        def norm(v):
            ms = jnp.mean(v * v, axis=-1, keepdims=True)
            return ((v * lax.rsqrt(ms + RMS_EPS)) * g).astype(BF16)

        rows = min(128, tm)

        def body(i, _):
            r0 = pl.multiple_of(i * rows, rows)
            hn_ref[pl.ds(r0, rows), :] = norm(x_ref[pl.ds(r0, rows), :])
            return 0

        lax.fori_loop(0, tm // rows, body, 0)
        hn_ref[tm:tm + N_META, :] = norm(meta_ref[...])

    def project():
        res = jnp.dot(hn_ref[...], w_ref[...].astype(BF16), preferred_element_type=F32)
        return res[:tm], res[tm:]

    def store_qkv(n_cols, scale):
        res_x, res_m = project()
        rx = _rope(res_x, csx_ref[0], csx_ref[1], n_cols)
        rm = _rope(res_m, csm_ref[0], csm_ref[1], n_cols)
        if scale != 1.0:
            rx = rx * scale
        qkv_ref[...] = rx.astype(BF16)
        qkvm_ref[0] = rm.astype(BF16)

    @pl.when(n < n_q_tiles)
    def _():
        store_qkv(tn, HEAD_DIM ** -0.5)

    @pl.when(n == n_q_tiles)
    def _():
        store_qkv(kvd, 1.0)

    @pl.when(n >= p["n_qkv_tiles"])
    def _():
        res_x, res_m = project()
        rest_ref[...] = res_x.astype(BF16)
        restm_ref[0] = res_m.astype(BF16)


def _inproj(x2, meta, g, w_in, p):
    tm, tn, d, r = p["tm"], p["tn"], p["d"], p["r"]
    nm, nn, nqkv = r // tm, p["n_col_tiles"], p["n_qkv_tiles"]
    csx, csm = _rope_tables(p["seq"])
    spt = p["seq"] // tm
    qkv_w = nqkv * tn
    est = (2 * tm * d * 4 + (tm + 16) * d * 2 + 2 * d * tn * 4 + d * tn * 2 + 4 * tm * tn * 2
           + 4 * 2 * tm * LANES * 4 + 4 * (tm + 16) * tn * 4)
    return pl.pallas_call(
        functools.partial(_inproj_kernel, p=p),
        grid=(nm, nn),
        in_specs=[
            pl.BlockSpec((tm, d), lambda m, n: (m, 0)),
            pl.BlockSpec((N_META, d), lambda m, n: (0, 0)),
            pl.BlockSpec((1, d), lambda m, n: (0, 0)),
            pl.BlockSpec((d, tn), lambda m, n: (0, n)),
            pl.BlockSpec((2, tm, LANES), lambda m, n: (0, m % spt, 0)),
            pl.BlockSpec((2, N_META, LANES), lambda m, n: (0, 0, 0)),
        ],
        out_specs=[
            pl.BlockSpec((tm, tn), lambda m, n: (m, jnp.minimum(n, nqkv - 1))),
            pl.BlockSpec((tm, tn), lambda m, n: (m, jnp.maximum(n - nqkv, 0))),
            pl.BlockSpec((1, N_META, tn), lambda m, n: (m, 0, jnp.minimum(n, nqkv - 1))),
            pl.BlockSpec((1, N_META, tn), lambda m, n: (m, 0, jnp.maximum(n - nqkv, 0))),
        ],
        out_shape=[
            jax.ShapeDtypeStruct((r, qkv_w), BF16),
            jax.ShapeDtypeStruct((r, p["rest_w"]), BF16),
            jax.ShapeDtypeStruct((nm, N_META, qkv_w), BF16),
            jax.ShapeDtypeStruct((nm, N_META, p["rest_w"]), BF16),
        ],
        scratch_shapes=[pltpu.VMEM((tm + N_META, d), BF16)],
        compiler_params=pltpu.CompilerParams(
            dimension_semantics=("arbitrary", "arbitrary"), vmem_limit_bytes=_vmem_limit(est)),
        name="inproj",
    )(x2, meta, g, w_in, csx, csm)


def _attn_kernel(sink_ref, q_ref, k0_ref, k1_ref, k2_ref, v0_ref, v1_ref, v2_ref, km_ref, vm_ref,
                 o_ref, kcat_ref, vcat_ref, *, p):
    seq, nkv = p["seq"], p["nkv"]
    nband = 3 * BLK
    nkeys = nband + N_META
    n = pl.program_id(1)
    for j, (kr, vr) in enumerate(((k0_ref, v0_ref), (k1_ref, v1_ref), (k2_ref, v2_ref))):
        kcat_ref[j * BLK:(j + 1) * BLK, :] = kr[...]
        vcat_ref[j * BLK:(j + 1) * BLK, :] = vr[...]
    kcat_ref[nband:nkeys, :] = km_ref[...]
    vcat_ref[nband:nkeys, :] = vm_ref[...]

    qi = lax.broadcasted_iota(I32, (BLK, nkeys), 0)
    sj = lax.broadcasted_iota(I32, (BLK, nkeys), 1)
    kx = (n - 1) * BLK + sj
    dq = n * BLK + qi - kx
    visible = (sj >= nband) | ((jnp.abs(dq) <= WINDOW) & (kx >= 0) & (kx < seq))
    neg = jnp.finfo(F32).min
    ones = jnp.ones((nkeys, HEAD_DIM), BF16)
    gi = lax.broadcasted_iota(I32, (Q_PER_KV, 1, 1), 0)
    for h in range(nkv):
        kh = kcat_ref[:, h * HEAD_DIM:(h + 1) * HEAD_DIM]
        vh = jnp.concatenate([vcat_ref[:, h * HEAD_DIM:(h + 1) * HEAD_DIM], ones], axis=1)
        heads = [h * Q_PER_KV + g for g in range(Q_PER_KV)]
        qs = jnp.concatenate([q_ref[:, hd * HEAD_DIM:(hd + 1) * HEAD_DIM] for hd in heads], axis=0)
        s = lax.dot_general(qs, kh, (((1,), (1,)), ((), ())), preferred_element_type=F32)
        s = jnp.where(visible[None], s.reshape(Q_PER_KV, BLK, nkeys), neg)
        snk = jnp.zeros((Q_PER_KV, 1, 1), F32)
        for g, hd in enumerate(heads):
            snk = jnp.where(gi == g, sink_ref[hd], snk)
        m = jnp.maximum(jnp.max(s, axis=2, keepdims=True), snk)
        e = jnp.exp(s - m).reshape(Q_PER_KV * BLK, nkeys)
        pv = jnp.dot(e.astype(BF16), vh, preferred_element_type=F32)
        pv = pv.reshape(Q_PER_KV, BLK, 2 * HEAD_DIM)
        o = pv[:, :, :HEAD_DIM] / (pv[:, :, HEAD_DIM:HEAD_DIM + 1] + jnp.exp(snk - m))
        for g, hd in enumerate(heads):
            o_ref[:, hd * HEAD_DIM:(hd + 1) * HEAD_DIM] = o[g].astype(BF16)


def _attention(qkv, qkvm, sink, p):
    attn, kvd, seq, b = p["attn"], p["kvd"], p["seq"], p["b"]
    nbx = seq // BLK
    kc = attn // kvd
    nkeys = 3 * BLK + N_META

    def kv_spec(off, col):
        return pl.BlockSpec((BLK, kvd), lambda bi, n: (bi * nbx + jnp.clip(n + off, 0, nbx - 1), col))

    return pl.pallas_call(
        functools.partial(_attn_kernel, p=p),
        grid=(b, nbx),
        in_specs=[
            pl.BlockSpec(memory_space=pltpu.SMEM),
            pl.BlockSpec((BLK, attn), lambda bi, n: (bi * nbx + n, 0)),
            kv_spec(-1, kc), kv_spec(0, kc), kv_spec(1, kc),
            kv_spec(-1, kc + 1), kv_spec(0, kc + 1), kv_spec(1, kc + 1),
            pl.BlockSpec((None, N_META, kvd), lambda bi, n: (0, 0, kc)),
            pl.BlockSpec((None, N_META, kvd), lambda bi, n: (0, 0, kc + 1)),
        ],
        out_specs=pl.BlockSpec((BLK, attn), lambda bi, n: (bi * nbx + n, 0)),
        out_shape=jax.ShapeDtypeStruct((p["r"], attn), BF16),
        scratch_shapes=[pltpu.VMEM((nkeys, kvd), BF16), pltpu.VMEM((nkeys, kvd), BF16)],
        compiler_params=pltpu.CompilerParams(dimension_semantics=("arbitrary", "arbitrary")),
        name="attn",
    )(sink, qkv, qkv, qkv, qkv, qkv, qkv, qkv, qkvm, qkvm)


def _lane_pack(cols, rows, dtype):
    lane = lax.broadcasted_iota(I32, (rows, LANES), 1)
    out = jnp.zeros((rows, LANES), dtype)
    for k, c in enumerate(cols):
        out = jnp.where(lane == k, c.astype(dtype), out)
    return out


def _mixer_kernel(attn_ref, rest_ref, prev_ref, next_ref, restm_ref, x_ref, cw_ref, wao_ref, wco_ref,
                  wout_ref, g_ref, rw_ref, rb_ref,
                  h1_ref, hn2_ref, ti_ref, gate_ref, rank_ref, cnt_ref, carry_ref, *, p):
    tc, conv, d, n_exp, seq = p["tc"], p["conv"], p["d"], p["n_exp"], p["seq"]
    i = pl.program_id(0)
    tiles_per_seq = seq // tc
    is_first = (i % tiles_per_seq) == 0
    is_last = (i % tiles_per_seq) == tiles_per_seq - 1
    o_ch, o_cb, o_cc, o_ga, o_gc = 0, conv, 2 * conv, 3 * conv, 3 * conv + d

    @pl.when(i == 0)
    def _():
        carry_ref[...] = jnp.zeros_like(carry_ref)

    def u_of(ref):
        return ref[:, o_cc:o_cc + conv].astype(F32) * ref[:, o_ch:o_ch + conv].astype(F32)

    u = u_of(rest_ref)
    last = N_META - 1
    u_prev = jnp.where(is_first, u_of(restm_ref)[last:last + 1], u_of(prev_ref)[last:last + 1])
    u_next = jnp.where(is_last, 0.0, u_of(next_ref)[0:1])
    row = lax.broadcasted_iota(I32, (tc, conv), 0)
    u_m1 = jnp.where(row == 0, u_prev, pltpu.roll(u, 1, 0))
    u_p1 = jnp.where(row == tc - 1, u_next, pltpu.roll(u, tc - 1, 0))
    cw = cw_ref[...]
    cv = u_m1 * cw[0:1] + u * cw[1:2] + u_p1 * cw[2:3]
    yc_in = (rest_ref[:, o_cb:o_cb + conv].astype(F32) * cv).astype(BF16)
    y_conv = jnp.dot(yc_in, wco_ref[...], preferred_element_type=F32)
    y_attn = jnp.dot(attn_ref[...], wao_ref[...], preferred_element_type=F32)
    g_a = rest_ref[:, o_ga:o_ga + d].astype(F32)
    g_c = rest_ref[:, o_gc:o_gc + d].astype(F32)
    merged = jax.nn.sigmoid(g_a) * y_attn + jax.nn.sigmoid(g_c) * y_conv
    h1 = x_ref[...] + jnp.dot(merged.astype(BF16), wout_ref[...], preferred_element_type=F32)
    h1_ref[...] = h1
    ms = jnp.mean(h1 * h1, axis=-1, keepdims=True)
    hn2 = (h1 * lax.rsqrt(ms + RMS_EPS)) * g_ref[...]
    hn2_ref[...] = hn2

    logits = jnp.dot(hn2.astype(BF16), rw_ref[...], preferred_element_type=F32) + rb_ref[...]
    lane = lax.broadcasted_iota(I32, (tc, n_exp), 1).astype(F32)
    sel = jnp.zeros((tc, n_exp), F32)
    tv, ti = [], []
    cur = logits
    for _ in range(TOP_K):
        m = jnp.max(cur, axis=1, keepdims=True)
        idx = jnp.min(jnp.where(cur == m, lane, float(n_exp)), axis=1, keepdims=True)
        hit = lane == idx
        tv.append(m)
        ti.append(idx)
        sel = jnp.where(hit, 1.0, sel)
        cur = jnp.where(hit, -jnp.inf, cur)
    ex = [jnp.exp(v - tv[0]) for v in tv]
    tot = ex[0] + ex[1] + ex[2] + ex[3]
    gates = [e / tot for e in ex]

    r_i = lax.broadcasted_iota(I32, (tc, tc), 0)
    c_i = lax.broadcasted_iota(I32, (tc, tc), 1)
    lower = jnp.where(r_i > c_i, 1.0, 0.0).astype(BF16)
    before = jnp.dot(lower, sel.astype(BF16), preferred_element_type=F32) + carry_ref[0:1, 0:n_exp]
    ranks = [jnp.sum(jnp.where(lane == t, before, 0.0), axis=1, keepdims=True) for t in ti]
    carry_ref[0:1, 0:n_exp] = carry_ref[0:1, 0:n_exp] + jnp.sum(sel, axis=0, keepdims=True)

    ti_ref[...] = _lane_pack(ti, tc, I32)
    gate_ref[...] = _lane_pack(gates, tc, F32)
    rank_ref[...] = _lane_pack(ranks, tc, I32)
    cnt_ref[...] = carry_ref[...]


def _mixer(attn_o, rest, restm, x2, conv_w, wao, wco, wout, g_ffn, router_w, router_b, p):
    tc, d, r, rw, n_exp = p["tc"], p["d"], p["r"], p["rest_w"], p["n_exp"]
    nt = r // tc
    sub = tc // N_META
    n16 = r // N_META
    const = lambda shape: pl.BlockSpec(shape, lambda i: (0,) * len(shape))
    est = (2 * (tc * p["attn"] * 2 + tc * rw * 2 + tc * d * 4 + 3 * N_META * rw * 2)
           + 2 * (p["attn"] * d + p["conv"] * d + d * d + d * n_exp) * 2
           + 2 * (2 * tc * d * 4 + 3 * tc * LANES * 4) + 10 * tc * d * 4)
    return pl.pallas_call(
        functools.partial(_mixer_kernel, p=p),
        grid=(nt,),
        in_specs=[
            pl.BlockSpec((tc, p["attn"]), lambda i: (i, 0)),
            pl.BlockSpec((tc, rw), lambda i: (i, 0)),
            pl.BlockSpec((N_META, rw), lambda i: (jnp.maximum(i * sub - 1, 0), 0)),
            pl.BlockSpec((N_META, rw), lambda i: (jnp.minimum((i + 1) * sub, n16 - 1), 0)),
            pl.BlockSpec((None, N_META, rw), lambda i: (0, 0, 0)),
            pl.BlockSpec((tc, d), lambda i: (i, 0)),
            const((CONV_K, p["conv"])),
            const((p["attn"], d)), const((p["conv"], d)), const((d, d)),
            const((1, d)), const((d, n_exp)), const((1, n_exp)),
        ],
        out_specs=[
            pl.BlockSpec((tc, d), lambda i: (i, 0)),
            pl.BlockSpec((tc, d), lambda i: (i, 0)),
            pl.BlockSpec((tc, LANES), lambda i: (i, 0)),
            pl.BlockSpec((tc, LANES), lambda i: (i, 0)),
            pl.BlockSpec((tc, LANES), lambda i: (i, 0)),
            pl.BlockSpec((8, LANES), lambda i: (0, 0)),
        ],
        out_shape=[
            jax.ShapeDtypeStruct((r, d), F32),
            jax.ShapeDtypeStruct((r, d), F32),
            jax.ShapeDtypeStruct((r, LANES), I32),
            jax.ShapeDtypeStruct((r, LANES), F32),
            jax.ShapeDtypeStruct((r, LANES), I32),
            jax.ShapeDtypeStruct((8, LANES), F32),
        ],
        scratch_shapes=[pltpu.VMEM((8, LANES), F32)],
        compiler_params=pltpu.CompilerParams(
            dimension_semantics=("arbitrary",), vmem_limit_bytes=_vmem_limit(est)),
        name="mixer",
    )(attn_o, rest, rest, rest, restm, x2, conv_w, wao, wco, wout, g_ffn, router_w, router_b)


def _dest_kernel(ti_ref, rank_ref, start_ref, dest_ref):
    rows = ti_ref.shape[0]
    lane = lax.broadcasted_iota(I32, (rows, LANES), 1).astype(F32)
    ti = ti_ref[...].astype(F32)
    rank = rank_ref[...].astype(F32)
    start = start_ref[0:1, :].astype(F32)
    out = jnp.zeros((rows, LANES), F32)
    for k in range(TOP_K):
        e_k = jnp.sum(jnp.where(lane == k, ti, 0.0), axis=1, keepdims=True)
        r_k = jnp.sum(jnp.where(lane == k, rank, 0.0), axis=1, keepdims=True)
        s_k = jnp.sum(jnp.where(lane == e_k, start, 0.0), axis=1, keepdims=True)
        out = jnp.where(lane == k, s_k + r_k, out)
    dest_ref[...] = out.astype(I32)


def _dest(ti, rank, row_start, p):
    r = p["r"]
    rows = min(1024, r)
    return pl.pallas_call(
        _dest_kernel,
        grid=(r // rows,),
        in_specs=[pl.BlockSpec((rows, LANES), lambda i: (i, 0)),
                  pl.BlockSpec((rows, LANES), lambda i: (i, 0)),
                  pl.BlockSpec((8, LANES), lambda i: (0, 0))],
        out_specs=pl.BlockSpec((rows, LANES), lambda i: (i, 0)),
        out_shape=jax.ShapeDtypeStruct((r, LANES), I32),
        name="dest",
    )(ti, rank, row_start)


def _routing_tables(counts, p):
    rt, tpc, n_exp, nch = p["rt"], p["tpc"], p["n_exp"], p["n_chunks_max"]
    cnt = counts[0, :n_exp].astype(I32)
    ntile = (cnt + rt - 1) // rt
    tile_start = jnp.cumsum(ntile) - ntile
    nchunk = (ntile + tpc - 1) // tpc
    chunk_end = jnp.cumsum(nchunk)
    c = jnp.arange(nch, dtype=I32)
    ce = jnp.minimum(jnp.sum((chunk_end[None, :] <= c[:, None]).astype(I32), axis=1), n_exp - 1)
    first = c - (chunk_end - nchunk)[ce]
    c_nt = jnp.clip(ntile[ce] - first * tpc, 0, tpc)
    c_nt = jnp.where(c < chunk_end[-1], c_nt, 0)
    c_ts = tile_start[ce] + first * tpc
    last_e = ce[jnp.maximum(chunk_end[-1] - 1, 0)]
    ce = jnp.where(c_nt > 0, ce, last_e)
    row_start = jnp.zeros((8, LANES), I32).at[0, :n_exp].set(tile_start * rt)
    used = jnp.sum(ntile).reshape(1).astype(I32)
    return row_start, cnt, tile_start * rt, ce.astype(I32), c_ts.astype(I32), c_nt.astype(I32), used


def _dispatch_kernel(cnt_ref, start_ref, dest_ref, hn_ref, xs_ref, zero_ref, sem, zsem, *, p):
    te, n_exp, rt, d = p["te"], p["n_exp"], p["rt"], p["d"]
    step = pl.program_id(0)

    @pl.when(step == 0)
    def _():
        zero_ref[...] = jnp.zeros_like(zero_ref)

        def per_expert(e, wait):
            def zero_rows(dst0, size):
                cp = pltpu.make_async_copy(zero_ref.at[pl.ds(0, size)], xs_ref.at[pl.ds(dst0, size)], zsem)
                cp.wait() if wait else cp.start()

            cnt = cnt_ref[e]
            cur = start_ref[e] + cnt
            pad = (rt - (cnt & (rt - 1))) & (rt - 1)
            head = (SUBLANES - (cur & (SUBLANES - 1))) & (SUBLANES - 1)
            for j in range(SUBLANES - 1):
                @pl.when(j < head)
                def _(j=j):
                    zero_rows(cur + j, 1)
            cur = cur + head
            rem = pad - head
            size = SUBLANES
            while size < rt:
                @pl.when((rem & size) != 0)
                def _(cur=cur, size=size):
                    zero_rows(pl.multiple_of(cur, SUBLANES), size)
                cur = cur + (rem & size)
                size *= 2

        lax.fori_loop(0, n_exp, lambda e, _: (per_expert(e, False), 0)[1], 0)
        lax.fori_loop(0, n_exp, lambda e, _: (per_expert(e, True), 0)[1], 0)

        half = rt // 2
        used = (start_ref[n_exp - 1] + cnt_ref[n_exp - 1] + rt - 1) // rt

        def tail_copy(t, j):
            r0 = pl.multiple_of(t * rt + j * half, half)
            return pltpu.make_async_copy(zero_ref, xs_ref.at[pl.ds(r0, half)], zsem)

        def tail_start(t, _):
            tail_copy(t, 0).start()
            tail_copy(t, 1).start()
            return 0

        def tail_wait(t, _):
            tail_copy(t, 0).wait()
            tail_copy(t, 1).wait()
            return 0

        lax.fori_loop(used, p["n_tiles_max"], tail_start, 0)
        lax.fori_loop(used, p["n_tiles_max"], tail_wait, 0)

    def row_copy(i, k):
        return pltpu.make_async_copy(hn_ref.at[pl.ds(i, 1)], xs_ref.at[pl.ds(dest_ref[i * TOP_K + k], 1)], sem)

    def issue(i, _):
        for k in range(TOP_K):
            row_copy(i, k).start(priority=k % 2)
        return 0

    lax.fori_loop(0, te, issue, 0)

    for k in range(TOP_K):
        pltpu.make_async_copy(hn_ref, xs_ref.at[pl.ds(0, te)], sem).wait()


def _dispatch(hn2, dest_flat, cnt, start, p):
    te, d, r = p["te"], p["d"], p["r"]
    return pl.pallas_call(
        functools.partial(_dispatch_kernel, p=p),
        grid=(r // te,),
        in_specs=[
            pl.BlockSpec(memory_space=pltpu.SMEM),
            pl.BlockSpec(memory_space=pltpu.SMEM),
            pl.BlockSpec((te * TOP_K,), lambda i: (i,), memory_space=pltpu.SMEM),
            pl.BlockSpec((te, d), lambda i: (i, 0)),
        ],
        out_specs=pl.BlockSpec(memory_space=pl.ANY),
        out_shape=jax.ShapeDtypeStruct((p["n_slots"], d), F32),
        scratch_shapes=[pltpu.VMEM((p["rt"] // 2, d), F32), pltpu.SemaphoreType.DMA, pltpu.SemaphoreType.DMA],
        compiler_params=pltpu.CompilerParams(dimension_semantics=("arbitrary",)),
        name="dispatch",
    )(cnt, start, dest_flat, hn2)


def _moe_kernel(ce_ref, cts_ref, cnt_ref, used_ref, xs_ref, wg_ref, wu_ref, bg_ref, bu_ref, wd_ref, bd_ref,
                ys_ref, xbuf_ref, yacc_ref, stage_ref, xsem, ysem, *, p):
    rt, nf, nch = p["rt"], p["nf"], p["n_chunks_max"]
    c = pl.program_id(0)
    f = pl.program_id(1)
    nt = cnt_ref[c]
    c_next = jnp.minimum(c + 1, nch - 1)
    nt_next = jnp.where(c + 1 < nch, cnt_ref[c_next], 0)

    def x_copy(chunk, t, slot):
        src0 = pl.multiple_of((cts_ref[chunk] + t) * rt, rt)
        return pltpu.make_async_copy(xs_ref.at[pl.ds(src0, rt)], stage_ref.at[slot], xsem.at[slot])

    def y_copy(chunk, t):
        r0 = pl.multiple_of(t * rt, rt)
        dst0 = pl.multiple_of((cts_ref[chunk] + t) * rt, rt)
        return pltpu.make_async_copy(yacc_ref.at[pl.ds(r0, rt)], ys_ref.at[pl.ds(dst0, rt)], ysem)

    def start_first_two(chunk, n_tiles):
        x_copy(chunk, 0, 0).start()

        @pl.when(n_tiles > 1)
        def _():
            x_copy(chunk, 1, 1).start()

    @pl.when(nt > 0)
    def _():
        @pl.when(f == 0)
        def _():
            @pl.when(c == 0)
            def _():
                start_first_two(c, nt)

            def load(t, _):
                slot = t % 2
                x_copy(c, t, slot).wait()
                r0 = pl.multiple_of(t * rt, rt)
                xbuf_ref[pl.ds(r0, rt), :] = stage_ref[slot].astype(BF16)

                @pl.when(t + 2 < nt)
                def _():
                    x_copy(c, t + 2, slot).start()

                return 0

            lax.fori_loop(0, nt, load, 0)

        bg = bg_ref[0]
        bu = bu_ref[0]
        bd = bd_ref[0]

        @pl.when(f == 0)
        def _():
            @pl.when(c > 0)
            def _():
                def drain(t, _):
                    y_copy(c - 1, t).wait()
                    return 0

                lax.fori_loop(0, cnt_ref[jnp.maximum(c - 1, 0)], drain, 0)

            def init(t, _):
                r0 = pl.multiple_of(t * rt, rt)
                yacc_ref[pl.ds(r0, rt), :] = jnp.broadcast_to(bd, (rt, p["d"]))
                return 0

            lax.fori_loop(0, nt, init, 0)

        @pl.when((f == nf - 1) & (nt_next > 0))
        def _():
            start_first_two(c_next, nt_next)

        def sub(t):
            r0 = pl.multiple_of(t * rt, rt)
            xt = xbuf_ref[pl.ds(r0, rt), :]
            gate = jnp.dot(xt, wg_ref[0].astype(BF16), preferred_element_type=F32) + bg
            up = jnp.dot(xt, wu_ref[0].astype(BF16), preferred_element_type=F32) + bu
            gate = jnp.minimum(gate, SWIGLU_LIMIT)
            up = jnp.clip(up, -SWIGLU_LIMIT, SWIGLU_LIMIT)
            act = (up + 1.0) * gate * jax.nn.sigmoid(SWIGLU_ALPHA * gate)
            yacc_ref[pl.ds(r0, rt), :] += jnp.dot(act.astype(BF16), wd_ref[0].astype(BF16),
                                                  preferred_element_type=F32)

        has_triple = (nt % 2 == 1) & (nt >= 3)
        n_pair_trips = (nt - jnp.where(has_triple, 3, 0)) // 2

        def pair(i, _):
            sub(2 * i)
            sub(2 * i + 1)
            return 0

        lax.fori_loop(0, n_pair_trips, pair, 0)

        @pl.when(has_triple)
        def _():
            sub(nt - 3)
            sub(nt - 2)
            sub(nt - 1)

        @pl.when(nt == 1)
        def _():
            sub(0)

        @pl.when(f == nf - 1)
        def _():
            def store(t, _):
                y_copy(c, t).start()
                return 0

            lax.fori_loop(0, nt, store, 0)

            @pl.when(nt_next == 0)
            def _():
                def drain(t, _):
                    y_copy(c, t).wait()
                    return 0

                lax.fori_loop(0, nt, drain, 0)

    @pl.when((c == nch - 1) & (f == nf - 1))
    def _():
        stage_ref[0] = jnp.zeros((rt, p["d"]), F32)

        def tail_copy(t):
            return pltpu.make_async_copy(stage_ref.at[0], ys_ref.at[pl.ds(pl.multiple_of(t * rt, rt), rt)], ysem)

        def tail_start(t, _):
            tail_copy(t).start()
            return 0

        def tail_wait(t, _):
            tail_copy(t).wait()
            return 0

        lax.fori_loop(used_ref[0], p["n_tiles_max"], tail_start, 0)
        lax.fori_loop(used_ref[0], p["n_tiles_max"], tail_wait, 0)


def _moe(xs, ce, cts, cnt, used, w_gate_up, b_gate_up, w_down, b_down, p):
    d, tf, nf, rt, cap = p["d"], p["tf"], p["nf"], p["rt"], p["cap"]
    n_exp = p["n_exp"]
    bgu = b_gate_up.reshape(n_exp, 1, 2 * p["d_ff"])
    bdn = b_down.reshape(n_exp, 1, d)

    def fi(c, f, cnt_r):
        return jnp.where(cnt_r[c] > 0, f, nf - 1)

    est = (2 * (2 * d * tf * 4 + tf * d * 4) + cap * d * 2 + cap * d * 4 + 2 * rt * d * 4 + 6 * rt * d * 4)
    grid_spec = pltpu.PrefetchScalarGridSpec(
        num_scalar_prefetch=4,
        grid=(p["n_chunks_max"], nf),
        in_specs=[
            pl.BlockSpec(memory_space=pl.ANY),
            pl.BlockSpec((1, d, tf), lambda c, f, ce_r, cts_r, cnt_r, u_r: (ce_r[c], 0, fi(c, f, cnt_r))),
            pl.BlockSpec((1, d, tf), lambda c, f, ce_r, cts_r, cnt_r, u_r: (ce_r[c], 0, nf + fi(c, f, cnt_r))),
            pl.BlockSpec((1, 1, tf), lambda c, f, ce_r, cts_r, cnt_r, u_r: (ce_r[c], 0, fi(c, f, cnt_r))),
            pl.BlockSpec((1, 1, tf), lambda c, f, ce_r, cts_r, cnt_r, u_r: (ce_r[c], 0, nf + fi(c, f, cnt_r))),
            pl.BlockSpec((1, tf, d), lambda c, f, ce_r, cts_r, cnt_r, u_r: (ce_r[c], fi(c, f, cnt_r), 0)),
            pl.BlockSpec((1, 1, d), lambda c, f, ce_r, cts_r, cnt_r, u_r: (ce_r[c], 0, 0)),
        ],
        out_specs=pl.BlockSpec(memory_space=pl.ANY),
        scratch_shapes=[
            pltpu.VMEM((cap, d), BF16), pltpu.VMEM((cap, d), F32), pltpu.VMEM((2, rt, d), F32),
            pltpu.SemaphoreType.DMA((2,)), pltpu.SemaphoreType.DMA,
        ],
    )
    return pl.pallas_call(
        functools.partial(_moe_kernel, p=p),
        grid_spec=grid_spec,
        out_shape=jax.ShapeDtypeStruct((p["n_slots"], d), F32),
        compiler_params=pltpu.CompilerParams(
            dimension_semantics=("arbitrary", "arbitrary"), vmem_limit_bytes=_vmem_limit(est)),
        name="moe",
    )(ce, cts, cnt, used, xs, w_gate_up, w_gate_up, bgu, bgu, w_down, bdn)


def _combine_kernel(dest_ref, dest_next_ref, h1_ref, gate_ref, g_ref, ys_ref, o_ref, ybuf_ref, sem, *, p):
    tg = p["tg"]
    step = pl.program_id(0)
    slot = step % 2

    def gather(d_ref, buf_slot):
        def issue(i, _):
            for k in range(TOP_K):
                pltpu.make_async_copy(ys_ref.at[pl.ds(d_ref[i * TOP_K + k], 1)],
                                      ybuf_ref.at[buf_slot, k, pl.ds(i, 1)],
                                      sem.at[buf_slot]).start(priority=k % 2)
            return 0

        lax.fori_loop(0, tg, issue, 0)

    @pl.when(step == 0)
    def _():
        gather(dest_ref, 0)

    @pl.when(step + 1 < pl.num_programs(0))
    def _():
        gather(dest_next_ref, 1 - slot)

    for k in range(TOP_K):
        pltpu.make_async_copy(ys_ref.at[pl.ds(0, tg)], ybuf_ref.at[slot, k], sem.at[slot]).wait()

    lane = lax.broadcasted_iota(I32, (tg, LANES), 1)
    gates = gate_ref[...]
    h = h1_ref[...]
    for k in range(TOP_K):
        g_k = jnp.sum(jnp.where(lane == k, gates, 0.0), axis=1, keepdims=True)
        h = h + ybuf_ref[slot, k] * g_k
    ms = jnp.mean(h * h, axis=-1, keepdims=True)
    o_ref[...] = (h * lax.rsqrt(ms + RMS_EPS)) * g_ref[...]


def _combine(ys, dest_flat, h1, gates, g_final, p):
    tg, d, r = p["tg"], p["d"], p["r"]
    nt = r // tg
    return pl.pallas_call(
        functools.partial(_combine_kernel, p=p),
        grid=(nt,),
        in_specs=[
            pl.BlockSpec((tg * TOP_K,), lambda i: (i,), memory_space=pltpu.SMEM),
            pl.BlockSpec((tg * TOP_K,), lambda i: (jnp.minimum(i + 1, nt - 1),), memory_space=pltpu.SMEM),
            pl.BlockSpec((tg, d), lambda i: (i, 0)),
            pl.BlockSpec((tg, LANES), lambda i: (i, 0)),
            pl.BlockSpec((1, d), lambda i: (0, 0)),
            pl.BlockSpec(memory_space=pl.ANY),
        ],
        out_specs=pl.BlockSpec((tg, d), lambda i: (i, 0)),
        out_shape=jax.ShapeDtypeStruct((r, d), F32),
        scratch_shapes=[pltpu.VMEM((2, TOP_K, tg, d), F32), pltpu.SemaphoreType.DMA((2,))],
        compiler_params=pltpu.CompilerParams(dimension_semantics=("arbitrary",)),
        name="combine",
    )(dest_flat, dest_flat, h1, gates, g_final, ys)


def kernel(x, meta_tokens, norm_mix_g, w_in, conv_w, sink, w_attn_o, w_conv_o, w_out, norm_ffn_g,
           router_w, router_b, w_gate_up, b_gate_up, w_down, b_down, norm_final_g):
    b, seq, d = x.shape
    n_exp, d_ff = w_down.shape[1], w_down.shape[2]
    assert norm_mix_g.shape[0] == 1 and meta_tokens.shape[0] == N_META
    p = _plan(b, seq, d, n_exp, d_ff)
    x2 = x.reshape(p["r"], d)

    qkv, rest, qkvm, restm = _inproj(x2, meta_tokens.astype(F32), norm_mix_g[0].reshape(1, d), w_in[0], p)
    attn_o = _attention(qkv, qkvm, sink[0].astype(F32), p)
    h1, hn2, ti, gates, rank, counts = _mixer(
        attn_o, rest, restm, x2, conv_w[0], w_attn_o[0].astype(BF16), w_conv_o[0].astype(BF16),
        w_out[0].astype(BF16), norm_ffn_g[0].reshape(1, d), router_w[0].astype(BF16),
        router_b[0].reshape(1, n_exp), p)
    row_start, cnt, start, ce, cts, cnt_tiles, used = _routing_tables(counts, p)
    dest = _dest(ti, rank, row_start, p)
    dest_flat = dest[:, :TOP_K].reshape(-1)
    xs = _dispatch(hn2, dest_flat, cnt, start, p)
    ys = _moe(xs, ce, cts, cnt_tiles, used, w_gate_up[0], b_gate_up[0], w_down[0], b_down[0], p)
    out = _combine(ys, dest_flat, h1, gates, norm_final_g.reshape(1, d), p)
    return out.reshape(b, seq, d)
```

```python
import functools

import numpy as np
import jax
import jax.numpy as jnp
from jax import lax
from jax.experimental import pallas as pl
from jax.experimental.pallas import tpu as pltpu

N_META = 16
BLK = 128
WINDOW = 128
HEAD_DIM = 64
Q_PER_KV = 4
ROT_DIM = HEAD_DIM // 4
ROPE_THETA = 500000.0
CONV_K = 3
TOP_K = 4
SWIGLU_ALPHA = 1.702
SWIGLU_LIMIT = 7.0
RMS_EPS = 1e-5

LANES = 128
SUBLANES = 8
VMEM_LIMIT_CAP = 60000 * 1024
MOE_ROW_TILE = 256
MOE_CHUNK_TILES = 5
MOE_W_SLOTS = 3

F32 = jnp.float32
BF16 = jnp.bfloat16
I32 = jnp.int32


def _vmem_limit(nbytes):
    return int(min(VMEM_LIMIT_CAP, max(32 * 1024 * 1024, nbytes * 5 // 4 + (4 << 20))))


def _plan(b, seq, d, n_exp, d_ff):
    attn = (d // 128) * HEAD_DIM
    kvd = attn // Q_PER_KV
    conv = d // 2
    r = b * seq
    p = dict(b=b, seq=seq, d=d, n_exp=n_exp, d_ff=d_ff, attn=attn, kvd=kvd, conv=conv, r=r)
    p["nq"] = attn // HEAD_DIM
    p["nkv"] = kvd // HEAD_DIM
    p["in_dim"] = attn + 2 * kvd + 3 * conv + 2 * d
    p["tn"] = 2 * kvd
    p["tm"] = min(1024, seq)
    p["n_qkv_tiles"] = (attn + 2 * kvd) // p["tn"]
    p["n_col_tiles"] = p["in_dim"] // p["tn"]
    p["rest_w"] = 3 * conv + 2 * d
    p["tc"] = min(256, seq)
    p["te"] = min(256, seq)
    p["tg"] = min(128, seq)
    p["tf"] = min(256, d_ff)
    p["nf"] = d_ff // p["tf"]
    p["rt"] = MOE_ROW_TILE
    p["tpc"] = MOE_CHUNK_TILES
    p["cap"] = MOE_ROW_TILE * MOE_CHUNK_TILES
    p["n_tiles_max"] = (r * TOP_K) // p["rt"] + n_exp
    p["n_chunks_max"] = n_exp + (p["n_tiles_max"] - n_exp) // p["tpc"]
    p["n_slots"] = p["n_tiles_max"] * p["rt"]
    assert seq % p["tm"] == 0 and seq % p["tc"] == 0 and seq % BLK == 0
    assert attn % p["tn"] == 0 and p["in_dim"] % p["tn"] == 0 and kvd % LANES == 0
    assert (r * TOP_K) % p["rt"] == 0 and d_ff % p["tf"] == 0
    assert p["rt"] & (p["rt"] - 1) == 0, "the zero-fill decomposition needs a power-of-two row tile"
    return p


def _rope_tables(seq):
    half = ROT_DIM // 2
    pos = jnp.arange(N_META + seq, dtype=F32)
    inv_freq = ROPE_THETA ** (-jnp.arange(0, ROT_DIM, 2, dtype=F32) / ROT_DIM)
    ang = pos[:, None] * inv_freq[None, :]
    cos, sin = jnp.cos(ang), jnp.sin(ang)
    lane = np.arange(LANES) % HEAD_DIM
    idx = np.where(lane < ROT_DIM, lane % half, 0)
    rot = jnp.asarray(lane < ROT_DIM)
    sign = jnp.asarray(np.where(lane < half, -1.0, 1.0).astype(np.float32))
    cos_l = jnp.where(rot[None, :], cos[:, idx], 1.0)
    sin_l = jnp.where(rot[None, :], sin[:, idx] * sign[None, :], 0.0)
    tab = jnp.stack([cos_l, sin_l]).astype(F32)
    return tab[:, N_META:], tab[:, :N_META]


def _rope(t, cos, sin, n_cols):
    lane = lax.broadcasted_iota(I32, (t.shape[0], LANES), 1)
    first = (lane % HEAD_DIM) < (ROT_DIM // 2)
    outs = []
    for c in range(t.shape[1] // LANES):
        s = t[:, c * LANES:(c + 1) * LANES]
        if c * LANES < n_cols:
            partner = jnp.where(first, pltpu.roll(s, LANES - ROT_DIM // 2, 1), pltpu.roll(s, ROT_DIM // 2, 1))
            s = s * cos + partner * sin
        outs.append(s)
    return jnp.concatenate(outs, axis=1)


def _inproj_kernel(x_ref, meta_ref, g_ref, w_ref, csx_ref, csm_ref,
                   qkv_ref, rest_ref, qkvm_ref, restm_ref, hn_ref, *, p):
    tm, tn, kvd = p["tm"], p["tn"], p["kvd"]
    n_q_tiles = p["attn"] // tn
    n = pl.program_id(1)

    @pl.when(n == 0)
    def _():
        g = g_ref[...]

        def norm(v):
            ms = jnp.mean(v * v, axis=-1, keepdims=True)
            return ((v * lax.rsqrt(ms + RMS_EPS)) * g).astype(BF16)

        rows = min(128, tm)

        def body(i, _):
            r0 = pl.multiple_of(i * rows, rows)
            hn_ref[pl.ds(r0, rows), :] = norm(x_ref[pl.ds(r0, rows), :])
            return 0

        lax.fori_loop(0, tm // rows, body, 0)
        hn_ref[tm:tm + N_META, :] = norm(meta_ref[...])

    def project():
        res = jnp.dot(hn_ref[...], w_ref[...].astype(BF16), preferred_element_type=F32)
        return res[:tm], res[tm:]

    def store_qkv(n_cols, scale):
        res_x, res_m = project()
        rx = _rope(res_x, csx_ref[0], csx_ref[1], n_cols)
        rm = _rope(res_m, csm_ref[0], csm_ref[1], n_cols)
        if scale != 1.0:
            rx = rx * scale
        qkv_ref[...] = rx.astype(BF16)
        qkvm_ref[0] = rm.astype(BF16)

    @pl.when(n < n_q_tiles)
    def _():
        store_qkv(tn, HEAD_DIM ** -0.5)

    @pl.when(n == n_q_tiles)
    def _():
        store_qkv(kvd, 1.0)

    @pl.when(n >= p["n_qkv_tiles"])
    def _():
        res_x, res_m = project()
        rest_ref[...] = res_x.astype(BF16)
        restm_ref[0] = res_m.astype(BF16)


def _inproj(x2, meta, g, w_in, p):
    tm, tn, d, r = p["tm"], p["tn"], p["d"], p["r"]
    nm, nn, nqkv = r // tm, p["n_col_tiles"], p["n_qkv_tiles"]
    csx, csm = _rope_tables(p["seq"])
    spt = p["seq"] // tm
    qkv_w = nqkv * tn
    est = (2 * tm * d * 4 + (tm + 16) * d * 2 + 2 * d * tn * 4 + d * tn * 2 + 4 * tm * tn * 2
           + 4 * 2 * tm * LANES * 4 + 4 * (tm + 16) * tn * 4)
    return pl.pallas_call(
        functools.partial(_inproj_kernel, p=p),
        grid=(nm, nn),
        in_specs=[
            pl.BlockSpec((tm, d), lambda m, n: (m, 0)),
            pl.BlockSpec((N_META, d), lambda m, n: (0, 0)),
            pl.BlockSpec((1, d), lambda m, n: (0, 0)),
            pl.BlockSpec((d, tn), lambda m, n: (0, n)),
            pl.BlockSpec((2, tm, LANES), lambda m, n: (0, m % spt, 0)),
            pl.BlockSpec((2, N_META, LANES), lambda m, n: (0, 0, 0)),
        ],
        out_specs=[
            pl.BlockSpec((tm, tn), lambda m, n: (m, jnp.minimum(n, nqkv - 1))),
            pl.BlockSpec((tm, tn), lambda m, n: (m, jnp.maximum(n - nqkv, 0))),
            pl.BlockSpec((1, N_META, tn), lambda m, n: (m, 0, jnp.minimum(n, nqkv - 1))),
            pl.BlockSpec((1, N_META, tn), lambda m, n: (m, 0, jnp.maximum(n - nqkv, 0))),
        ],
        out_shape=[
            jax.ShapeDtypeStruct((r, qkv_w), BF16),
            jax.ShapeDtypeStruct((r, p["rest_w"]), BF16),
            jax.ShapeDtypeStruct((nm, N_META, qkv_w), BF16),
            jax.ShapeDtypeStruct((nm, N_META, p["rest_w"]), BF16),
        ],
        scratch_shapes=[pltpu.VMEM((tm + N_META, d), BF16)],
        compiler_params=pltpu.CompilerParams(
            dimension_semantics=("arbitrary", "arbitrary"), vmem_limit_bytes=_vmem_limit(est)),
        name="inproj",
    )(x2, meta, g, w_in, csx, csm)


def _attn_kernel(sink_ref, q_ref, k0_ref, k1_ref, k2_ref, v0_ref, v1_ref, v2_ref, km_ref, vm_ref,
                 o_ref, kcat_ref, vcat_ref, *, p):
    seq, nkv = p["seq"], p["nkv"]
    nband = 3 * BLK
    nkeys = nband + N_META
    n = pl.program_id(1)
    for j, (kr, vr) in enumerate(((k0_ref, v0_ref), (k1_ref, v1_ref), (k2_ref, v2_ref))):
        kcat_ref[j * BLK:(j + 1) * BLK, :] = kr[...]
        vcat_ref[j * BLK:(j + 1) * BLK, :] = vr[...]
    kcat_ref[nband:nkeys, :] = km_ref[...]
    vcat_ref[nband:nkeys, :] = vm_ref[...]

    qi = lax.broadcasted_iota(I32, (BLK, nkeys), 0)
    sj = lax.broadcasted_iota(I32, (BLK, nkeys), 1)
    kx = (n - 1) * BLK + sj
    dq = n * BLK + qi - kx
    visible = (sj >= nband) | ((jnp.abs(dq) <= WINDOW) & (kx >= 0) & (kx < seq))
    neg = jnp.finfo(F32).min
    ones = jnp.ones((nkeys, HEAD_DIM), BF16)
    gi = lax.broadcasted_iota(I32, (Q_PER_KV, 1, 1), 0)
    for h in range(nkv):
        kh = kcat_ref[:, h * HEAD_DIM:(h + 1) * HEAD_DIM]
        vh = jnp.concatenate([vcat_ref[:, h * HEAD_DIM:(h + 1) * HEAD_DIM], ones], axis=1)
        heads = [h * Q_PER_KV + g for g in range(Q_PER_KV)]
        qs = jnp.concatenate([q_ref[:, hd * HEAD_DIM:(hd + 1) * HEAD_DIM] for hd in heads], axis=0)
        s = lax.dot_general(qs, kh, (((1,), (1,)), ((), ())), preferred_element_type=F32)
        s = jnp.where(visible[None], s.reshape(Q_PER_KV, BLK, nkeys), neg)
        snk = jnp.zeros((Q_PER_KV, 1, 1), F32)
        for g, hd in enumerate(heads):
            snk = jnp.where(gi == g, sink_ref[hd], snk)
        m = jnp.maximum(jnp.max(s, axis=2, keepdims=True), snk)
        e = jnp.exp(s - m).reshape(Q_PER_KV * BLK, nkeys)
        pv = jnp.dot(e.astype(BF16), vh, preferred_element_type=F32)
        pv = pv.reshape(Q_PER_KV, BLK, 2 * HEAD_DIM)
        o = pv[:, :, :HEAD_DIM] / (pv[:, :, HEAD_DIM:HEAD_DIM + 1] + jnp.exp(snk - m))
        for g, hd in enumerate(heads):
            o_ref[:, hd * HEAD_DIM:(hd + 1) * HEAD_DIM] = o[g].astype(BF16)


def _attention(qkv, qkvm, sink, p):
    attn, kvd, seq, b = p["attn"], p["kvd"], p["seq"], p["b"]
    nbx = seq // BLK
    kc = attn // kvd
    nkeys = 3 * BLK + N_META

    def kv_spec(off, col):
        return pl.BlockSpec((BLK, kvd), lambda bi, n: (bi * nbx + jnp.clip(n + off, 0, nbx - 1), col))

    return pl.pallas_call(
        functools.partial(_attn_kernel, p=p),
        grid=(b, nbx),
        in_specs=[
            pl.BlockSpec(memory_space=pltpu.SMEM),
            pl.BlockSpec((BLK, attn), lambda bi, n: (bi * nbx + n, 0)),
            kv_spec(-1, kc), kv_spec(0, kc), kv_spec(1, kc),
            kv_spec(-1, kc + 1), kv_spec(0, kc + 1), kv_spec(1, kc + 1),
            pl.BlockSpec((None, N_META, kvd), lambda bi, n: (0, 0, kc)),
            pl.BlockSpec((None, N_META, kvd), lambda bi, n: (0, 0, kc + 1)),
        ],
        out_specs=pl.BlockSpec((BLK, attn), lambda bi, n: (bi * nbx + n, 0)),
        out_shape=jax.ShapeDtypeStruct((p["r"], attn), BF16),
        scratch_shapes=[pltpu.VMEM((nkeys, kvd), BF16), pltpu.VMEM((nkeys, kvd), BF16)],
        compiler_params=pltpu.CompilerParams(dimension_semantics=("arbitrary", "arbitrary")),
        name="attn",
    )(sink, qkv, qkv, qkv, qkv, qkv, qkv, qkv, qkvm, qkvm)


def _lane_pack(cols, rows, dtype):
    lane = lax.broadcasted_iota(I32, (rows, LANES), 1)
    out = jnp.zeros((rows, LANES), dtype)
    for k, c in enumerate(cols):
        out = jnp.where(lane == k, c.astype(dtype), out)
    return out


def _mixer_kernel(attn_ref, rest_ref, prev_ref, next_ref, restm_ref, x_ref, cw_ref, wao_ref, wco_ref,
                  wout_ref, g_ref, rw_ref, rb_ref,
                  h1_ref, hn2_ref, ti_ref, gate_ref, rank_ref, cnt_ref, carry_ref, *, p):
    tc, conv, d, n_exp, seq = p["tc"], p["conv"], p["d"], p["n_exp"], p["seq"]
    i = pl.program_id(0)
    tiles_per_seq = seq // tc
    is_first = (i % tiles_per_seq) == 0
    is_last = (i % tiles_per_seq) == tiles_per_seq - 1
    o_ch, o_cb, o_cc, o_ga, o_gc = 0, conv, 2 * conv, 3 * conv, 3 * conv + d

    @pl.when(i == 0)
    def _():
        carry_ref[...] = jnp.zeros_like(carry_ref)

    def u_of(ref):
        return ref[:, o_cc:o_cc + conv].astype(F32) * ref[:, o_ch:o_ch + conv].astype(F32)

    u = u_of(rest_ref)
    last = N_META - 1
    u_prev = jnp.where(is_first, u_of(restm_ref)[last:last + 1], u_of(prev_ref)[last:last + 1])
    u_next = jnp.where(is_last, 0.0, u_of(next_ref)[0:1])
    row = lax.broadcasted_iota(I32, (tc, conv), 0)
    u_m1 = jnp.where(row == 0, u_prev, pltpu.roll(u, 1, 0))
    u_p1 = jnp.where(row == tc - 1, u_next, pltpu.roll(u, tc - 1, 0))
    cw = cw_ref[...]
    cv = u_m1 * cw[0:1] + u * cw[1:2] + u_p1 * cw[2:3]
    yc_in = (rest_ref[:, o_cb:o_cb + conv].astype(F32) * cv).astype(BF16)
    y_conv = jnp.dot(yc_in, wco_ref[...], preferred_element_type=F32)
    y_attn = jnp.dot(attn_ref[...], wao_ref[...], preferred_element_type=F32)
    g_a = rest_ref[:, o_ga:o_ga + d].astype(F32)
    g_c = rest_ref[:, o_gc:o_gc + d].astype(F32)
    merged = jax.nn.sigmoid(g_a) * y_attn + jax.nn.sigmoid(g_c) * y_conv
    h1 = x_ref[...] + jnp.dot(merged.astype(BF16), wout_ref[...], preferred_element_type=F32)
    h1_ref[...] = h1
    ms = jnp.mean(h1 * h1, axis=-1, keepdims=True)
    hn2 = (h1 * lax.rsqrt(ms + RMS_EPS)) * g_ref[...]
    hn2_ref[...] = hn2

    logits = jnp.dot(hn2.astype(BF16), rw_ref[...], preferred_element_type=F32) + rb_ref[...]
    lane = lax.broadcasted_iota(I32, (tc, n_exp), 1).astype(F32)
    sel = jnp.zeros((tc, n_exp), F32)
    tv, ti = [], []
    cur = logits
    for _ in range(TOP_K):
        m = jnp.max(cur, axis=1, keepdims=True)
        idx = jnp.min(jnp.where(cur == m, lane, float(n_exp)), axis=1, keepdims=True)
        hit = lane == idx
        tv.append(m)
        ti.append(idx)
        sel = jnp.where(hit, 1.0, sel)
        cur = jnp.where(hit, -jnp.inf, cur)
    ex = [jnp.exp(v - tv[0]) for v in tv]
    tot = ex[0] + ex[1] + ex[2] + ex[3]
    gates = [e / tot for e in ex]

    r_i = lax.broadcasted_iota(I32, (tc, tc), 0)
    c_i = lax.broadcasted_iota(I32, (tc, tc), 1)
    lower = jnp.where(r_i > c_i, 1.0, 0.0).astype(BF16)
    before = jnp.dot(lower, sel.astype(BF16), preferred_element_type=F32) + carry_ref[0:1, 0:n_exp]
    ranks = [jnp.sum(jnp.where(lane == t, before, 0.0), axis=1, keepdims=True) for t in ti]
    carry_ref[0:1, 0:n_exp] = carry_ref[0:1, 0:n_exp] + jnp.sum(sel, axis=0, keepdims=True)

    ti_ref[...] = _lane_pack(ti, tc, I32)
    gate_ref[...] = _lane_pack(gates, tc, F32)
    rank_ref[...] = _lane_pack(ranks, tc, I32)
    cnt_ref[...] = carry_ref[...]


def _mixer(attn_o, rest, restm, x2, conv_w, wao, wco, wout, g_ffn, router_w, router_b, p):
    tc, d, r, rw, n_exp = p["tc"], p["d"], p["r"], p["rest_w"], p["n_exp"]
    nt = r // tc
    sub = tc // N_META
    n16 = r // N_META
    const = lambda shape: pl.BlockSpec(shape, lambda i: (0,) * len(shape))
    est = (2 * (tc * p["attn"] * 2 + tc * rw * 2 + tc * d * 4 + 3 * N_META * rw * 2)
           + 2 * (p["attn"] * d + p["conv"] * d + d * d + d * n_exp) * 2
           + 2 * (2 * tc * d * 4 + 3 * tc * LANES * 4) + 10 * tc * d * 4)
    return pl.pallas_call(
        functools.partial(_mixer_kernel, p=p),
        grid=(nt,),
        in_specs=[
            pl.BlockSpec((tc, p["attn"]), lambda i: (i, 0)),
            pl.BlockSpec((tc, rw), lambda i: (i, 0)),
            pl.BlockSpec((N_META, rw), lambda i: (jnp.maximum(i * sub - 1, 0), 0)),
            pl.BlockSpec((N_META, rw), lambda i: (jnp.minimum((i + 1) * sub, n16 - 1), 0)),
            pl.BlockSpec((None, N_META, rw), lambda i: (0, 0, 0)),
            pl.BlockSpec((tc, d), lambda i: (i, 0)),
            const((CONV_K, p["conv"])),
            const((p["attn"], d)), const((p["conv"], d)), const((d, d)),
            const((1, d)), const((d, n_exp)), const((1, n_exp)),
        ],
        out_specs=[
            pl.BlockSpec((tc, d), lambda i: (i, 0)),
            pl.BlockSpec((tc, d), lambda i: (i, 0)),
            pl.BlockSpec((tc, LANES), lambda i: (i, 0)),
            pl.BlockSpec((tc, LANES), lambda i: (i, 0)),
            pl.BlockSpec((tc, LANES), lambda i: (i, 0)),
            pl.BlockSpec((8, LANES), lambda i: (0, 0)),
        ],
        out_shape=[
            jax.ShapeDtypeStruct((r, d), F32),
            jax.ShapeDtypeStruct((r, d), F32),
            jax.ShapeDtypeStruct((r, LANES), I32),
            jax.ShapeDtypeStruct((r, LANES), F32),
            jax.ShapeDtypeStruct((r, LANES), I32),
            jax.ShapeDtypeStruct((8, LANES), F32),
        ],
        scratch_shapes=[pltpu.VMEM((8, LANES), F32)],
        compiler_params=pltpu.CompilerParams(
            dimension_semantics=("arbitrary",), vmem_limit_bytes=_vmem_limit(est)),
        name="mixer",
    )(attn_o, rest, rest, rest, restm, x2, conv_w, wao, wco, wout, g_ffn, router_w, router_b)


def _dest_kernel(ti_ref, rank_ref, start_ref, dest_ref):
    rows = ti_ref.shape[0]
    lane = lax.broadcasted_iota(I32, (rows, LANES), 1).astype(F32)
    ti = ti_ref[...].astype(F32)
    rank = rank_ref[...].astype(F32)
    start = start_ref[0:1, :].astype(F32)
    out = jnp.zeros((rows, LANES), F32)
    for k in range(TOP_K):
        e_k = jnp.sum(jnp.where(lane == k, ti, 0.0), axis=1, keepdims=True)
        r_k = jnp.sum(jnp.where(lane == k, rank, 0.0), axis=1, keepdims=True)
        s_k = jnp.sum(jnp.where(lane == e_k, start, 0.0), axis=1, keepdims=True)
        out = jnp.where(lane == k, s_k + r_k, out)
    dest_ref[...] = out.astype(I32)


def _dest(ti, rank, row_start, p):
    r = p["r"]
    rows = min(1024, r)
    return pl.pallas_call(
        _dest_kernel,
        grid=(r // rows,),
        in_specs=[pl.BlockSpec((rows, LANES), lambda i: (i, 0)),
                  pl.BlockSpec((rows, LANES), lambda i: (i, 0)),
                  pl.BlockSpec((8, LANES), lambda i: (0, 0))],
        out_specs=pl.BlockSpec((rows, LANES), lambda i: (i, 0)),
        out_shape=jax.ShapeDtypeStruct((r, LANES), I32),
        name="dest",
    )(ti, rank, row_start)


def _routing_tables(counts, p):
    rt, tpc, n_exp, nch = p["rt"], p["tpc"], p["n_exp"], p["n_chunks_max"]
    cnt = counts[0, :n_exp].astype(I32)
    ntile = (cnt + rt - 1) // rt
    tile_start = jnp.cumsum(ntile) - ntile
    nchunk = (ntile + tpc - 1) // tpc
    chunk_end = jnp.cumsum(nchunk)
    c = jnp.arange(nch, dtype=I32)
    ce = jnp.minimum(jnp.sum((chunk_end[None, :] <= c[:, None]).astype(I32), axis=1), n_exp - 1)
    first = c - (chunk_end - nchunk)[ce]
    c_nt = jnp.clip(ntile[ce] - first * tpc, 0, tpc)
    c_nt = jnp.where(c < chunk_end[-1], c_nt, 0)
    c_ts = tile_start[ce] + first * tpc
    last_e = ce[jnp.maximum(chunk_end[-1] - 1, 0)]
    ce = jnp.where(c_nt > 0, ce, last_e)
    row_start = jnp.zeros((8, LANES), I32).at[0, :n_exp].set(tile_start * rt)
    used = jnp.stack([jnp.sum(ntile), chunk_end[-1]]).astype(I32)
    return row_start, cnt, tile_start * rt, ce.astype(I32), c_ts.astype(I32), c_nt.astype(I32), used


def _dispatch_kernel(cnt_ref, start_ref, dest_ref, hn_ref, xs_ref, zero_ref, sem, zsem, *, p):
    te, n_exp, rt, d = p["te"], p["n_exp"], p["rt"], p["d"]
    step = pl.program_id(0)

    @pl.when(step == 0)
    def _():
        zero_ref[...] = jnp.zeros_like(zero_ref)

        def per_expert(e, wait):
            def zero_rows(dst0, size):
                cp = pltpu.make_async_copy(zero_ref.at[pl.ds(0, size)], xs_ref.at[pl.ds(dst0, size)], zsem)
                cp.wait() if wait else cp.start()

            cnt = cnt_ref[e]
            cur = start_ref[e] + cnt
            pad = (rt - (cnt & (rt - 1))) & (rt - 1)
            head = (SUBLANES - (cur & (SUBLANES - 1))) & (SUBLANES - 1)
            for j in range(SUBLANES - 1):
                @pl.when(j < head)
                def _(j=j):
                    zero_rows(cur + j, 1)
            cur = cur + head
            rem = pad - head
            size = SUBLANES
            while size < rt:
                @pl.when((rem & size) != 0)
                def _(cur=cur, size=size):
                    zero_rows(pl.multiple_of(cur, SUBLANES), size)
                cur = cur + (rem & size)
                size *= 2

        lax.fori_loop(0, n_exp, lambda e, _: (per_expert(e, False), 0)[1], 0)
        lax.fori_loop(0, n_exp, lambda e, _: (per_expert(e, True), 0)[1], 0)

        half = rt // 2
        used = (start_ref[n_exp - 1] + cnt_ref[n_exp - 1] + rt - 1) // rt

        def tail_copy(t, j):
            r0 = pl.multiple_of(t * rt + j * half, half)
            return pltpu.make_async_copy(zero_ref, xs_ref.at[pl.ds(r0, half)], zsem)

        def tail_start(t, _):
            tail_copy(t, 0).start()
            tail_copy(t, 1).start()
            return 0

        def tail_wait(t, _):
            tail_copy(t, 0).wait()
            tail_copy(t, 1).wait()
            return 0

        lax.fori_loop(used, p["n_tiles_max"], tail_start, 0)
        lax.fori_loop(used, p["n_tiles_max"], tail_wait, 0)

    def row_copy(i, k):
        return pltpu.make_async_copy(hn_ref.at[pl.ds(i, 1)], xs_ref.at[pl.ds(dest_ref[i * TOP_K + k], 1)], sem)

    def issue(i, _):
        for k in range(TOP_K):
            row_copy(i, k).start(priority=k % 2)
        return 0

    lax.fori_loop(0, te, issue, 0)

    for k in range(TOP_K):
        pltpu.make_async_copy(hn_ref, xs_ref.at[pl.ds(0, te)], sem).wait()


def _dispatch(hn2, dest_flat, cnt, start, p):
    te, d, r = p["te"], p["d"], p["r"]
    return pl.pallas_call(
        functools.partial(_dispatch_kernel, p=p),
        grid=(r // te,),
        in_specs=[
            pl.BlockSpec(memory_space=pltpu.SMEM),
            pl.BlockSpec(memory_space=pltpu.SMEM),
            pl.BlockSpec((te * TOP_K,), lambda i: (i,), memory_space=pltpu.SMEM),
            pl.BlockSpec((te, d), lambda i: (i, 0)),
        ],
        out_specs=pl.BlockSpec(memory_space=pl.ANY),
        out_shape=jax.ShapeDtypeStruct((p["n_slots"], d), F32),
        scratch_shapes=[pltpu.VMEM((p["rt"] // 2, d), F32), pltpu.SemaphoreType.DMA, pltpu.SemaphoreType.DMA],
        compiler_params=pltpu.CompilerParams(dimension_semantics=("arbitrary",)),
        name="dispatch",
    )(cnt, start, dest_flat, hn2)


def _moe_kernel(ce_ref, cts_ref, cnt_ref, used_ref, xs_ref, wgu_ref, bgu_ref, wd_ref, bd_ref,
                ys_ref, xbuf_ref, yacc_ref, stage_ref, wg_ring, wu_ring, wd_ring, xsem, ysem, wsem, *, p):
    rt, nf, nch, tf, d_ff, nb = p["rt"], p["nf"], p["n_chunks_max"], p["tf"], p["d_ff"], MOE_W_SLOTS
    c = pl.program_id(0)
    nt = cnt_ref[c]
    c_next = jnp.minimum(c + 1, nch - 1)
    nt_next = jnp.where(c + 1 < nch, cnt_ref[c_next], 0)
    n_w_tiles = used_ref[1] * nf

    def w_copies(g):
        chunk = lax.div(g, nf)
        fg = lax.rem(g, nf)
        e = ce_ref[chunk]
        slot = lax.rem(g, nb)
        c0 = pl.multiple_of(fg * tf, tf)
        return (
            pltpu.make_async_copy(wgu_ref.at[e, :, pl.ds(c0, tf)], wg_ring.at[slot], wsem.at[slot]),
            pltpu.make_async_copy(wgu_ref.at[e, :, pl.ds(d_ff + c0, tf)], wu_ring.at[slot], wsem.at[slot]),
            pltpu.make_async_copy(wd_ref.at[e, pl.ds(c0, tf), :], wd_ring.at[slot], wsem.at[slot]),
        )

    def w_start(g):
        @pl.when(g < n_w_tiles)
        def _():
            for cp in w_copies(g):
                cp.start()

    acc = lax.rem(c, 2)

    def x_copy(chunk, t):
        src0 = pl.multiple_of((cts_ref[chunk] + t) * rt, rt)
        return pltpu.make_async_copy(xs_ref.at[pl.ds(src0, rt)], stage_ref.at[t], xsem.at[t])

    def y_copy(chunk, t):
        a = lax.rem(chunk, 2)
        r0 = pl.multiple_of(t * rt, rt)
        dst0 = pl.multiple_of((cts_ref[chunk] + t) * rt, rt)
        return pltpu.make_async_copy(yacc_ref.at[a, pl.ds(r0, rt)], ys_ref.at[pl.ds(dst0, rt)], ysem.at[a])

    def x_start_all(chunk, n_tiles):
        for t in range(p["tpc"]):
            @pl.when(t < n_tiles)
            def _(t=t):
                x_copy(chunk, t).start()

    @pl.when(nt > 0)
    def _():
        @pl.when(c == 0)
        def _():
            for g in range(nb):
                w_start(g)
            x_start_all(c, nt)

        def load(t, _):
            x_copy(c, t).wait()
            r0 = pl.multiple_of(t * rt, rt)
            xbuf_ref[pl.ds(r0, rt), :] = stage_ref[t].astype(BF16)
            return 0

        lax.fori_loop(0, nt, load, 0)

        @pl.when(nt_next > 0)
        def _():
            x_start_all(c_next, nt_next)

        bd = bd_ref[0]

        def init(t, _):
            r0 = pl.multiple_of(t * rt, rt)
            yacc_ref[acc, pl.ds(r0, rt), :] = jnp.broadcast_to(bd, (rt, p["d"]))
            return 0

        lax.fori_loop(0, nt, init, 0)

        def hidden_tile(f, _):
            g = c * nf + f
            slot = lax.rem(g, nb)
            for cp in w_copies(g):
                cp.wait()

            bg = bgu_ref[0, pl.ds(f, 1), :]
            bu = bgu_ref[0, pl.ds(nf + f, 1), :]

            def ffn(rows):
                xt = xbuf_ref[0:rows, :]
                gate = jnp.dot(xt, wg_ring[slot].astype(BF16), preferred_element_type=F32) + bg
                up = jnp.dot(xt, wu_ring[slot].astype(BF16), preferred_element_type=F32) + bu
                gate = jnp.minimum(gate, SWIGLU_LIMIT)
                up = jnp.clip(up, -SWIGLU_LIMIT, SWIGLU_LIMIT)
                act = (up + 1.0) * gate * jax.nn.sigmoid(SWIGLU_ALPHA * gate)
                yacc_ref[acc, 0:rows, :] += jnp.dot(act.astype(BF16), wd_ring[slot].astype(BF16),
                                                    preferred_element_type=F32)

            for n in range(1, p["tpc"] + 1):
                @pl.when(nt == n)
                def _(n=n):
                    ffn(n * rt)

            w_start(g + nb)
            return 0

        lax.fori_loop(0, nf, hidden_tile, 0)

        def drain(chunk):
            def body(t, _):
                y_copy(chunk, t).wait()
                return 0

            lax.fori_loop(0, cnt_ref[chunk], body, 0)

        @pl.when(c > 0)
        def _():
            drain(jnp.maximum(c - 1, 0))

        def store(t, _):
            y_copy(c, t).start()
            return 0

        lax.fori_loop(0, nt, store, 0)

        @pl.when(nt_next == 0)
        def _():
            drain(c)

    @pl.when(c == nch - 1)
    def _():
        stage_ref[0] = jnp.zeros((rt, p["d"]), F32)

        def tail_copy(t):
            return pltpu.make_async_copy(stage_ref.at[0], ys_ref.at[pl.ds(pl.multiple_of(t * rt, rt), rt)],
                                         ysem.at[0])

        def tail_start(t, _):
            tail_copy(t).start()
            return 0

        def tail_wait(t, _):
            tail_copy(t).wait()
            return 0

        lax.fori_loop(used_ref[0], p["n_tiles_max"], tail_start, 0)
        lax.fori_loop(used_ref[0], p["n_tiles_max"], tail_wait, 0)


def _moe(xs, ce, cts, cnt, used, w_gate_up, b_gate_up, w_down, b_down, p):
    d, tf, nf, rt, cap = p["d"], p["tf"], p["nf"], p["rt"], p["cap"]
    n_exp = p["n_exp"]
    nb = MOE_W_SLOTS
    bgu = b_gate_up.reshape(n_exp, 2 * nf, tf)
    bdn = b_down.reshape(n_exp, 1, d)

    est = (nb * 3 * d * tf * 4 + cap * d * 2 + 2 * cap * d * 4 + cap * d * 4 + 4 * rt * d * 4)
    grid_spec = pltpu.PrefetchScalarGridSpec(
        num_scalar_prefetch=4,
        grid=(p["n_chunks_max"],),
        in_specs=[
            pl.BlockSpec(memory_space=pl.ANY),
            pl.BlockSpec(memory_space=pl.ANY),
            pl.BlockSpec((1, 2 * nf, tf), lambda c, ce_r, cts_r, cnt_r, u_r: (ce_r[c], 0, 0)),
            pl.BlockSpec(memory_space=pl.ANY),
            pl.BlockSpec((1, 1, d), lambda c, ce_r, cts_r, cnt_r, u_r: (ce_r[c], 0, 0)),
        ],
        out_specs=pl.BlockSpec(memory_space=pl.ANY),
        scratch_shapes=[
            pltpu.VMEM((cap, d), BF16), pltpu.VMEM((2, cap, d), F32), pltpu.VMEM((p["tpc"], rt, d), F32),
            pltpu.VMEM((nb, d, tf), F32), pltpu.VMEM((nb, d, tf), F32), pltpu.VMEM((nb, tf, d), F32),
            pltpu.SemaphoreType.DMA((p["tpc"],)), pltpu.SemaphoreType.DMA((2,)), pltpu.SemaphoreType.DMA((nb,)),
        ],
    )
    return pl.pallas_call(
        functools.partial(_moe_kernel, p=p),
        grid_spec=grid_spec,
        out_shape=jax.ShapeDtypeStruct((p["n_slots"], d), F32),
        compiler_params=pltpu.CompilerParams(
            dimension_semantics=("arbitrary",), vmem_limit_bytes=_vmem_limit(est)),
        name="moe",
    )(ce, cts, cnt, used, xs, w_gate_up, bgu, w_down, bdn)


def _combine_kernel(dest_ref, dest_next_ref, h1_ref, gate_ref, g_ref, ys_ref, o_ref, ybuf_ref, sem, *, p):
    tg = p["tg"]
    step = pl.program_id(0)
    slot = step % 2

    def gather(d_ref, buf_slot):
        def issue(i, _):
            for k in range(TOP_K):
                pltpu.make_async_copy(ys_ref.at[pl.ds(d_ref[i * TOP_K + k], 1)],
                                      ybuf_ref.at[buf_slot, k, pl.ds(i, 1)],
                                      sem.at[buf_slot]).start(priority=k % 2)
            return 0

        lax.fori_loop(0, tg, issue, 0)

    @pl.when(step == 0)
    def _():
        gather(dest_ref, 0)

    @pl.when(step + 1 < pl.num_programs(0))
    def _():
        gather(dest_next_ref, 1 - slot)

    for k in range(TOP_K):
        pltpu.make_async_copy(ys_ref.at[pl.ds(0, tg)], ybuf_ref.at[slot, k], sem.at[slot]).wait()

    lane = lax.broadcasted_iota(I32, (tg, LANES), 1)
    gates = gate_ref[...]
    h = h1_ref[...]
    for k in range(TOP_K):
        g_k = jnp.sum(jnp.where(lane == k, gates, 0.0), axis=1, keepdims=True)
        h = h + ybuf_ref[slot, k] * g_k
    ms = jnp.mean(h * h, axis=-1, keepdims=True)
    o_ref[...] = (h * lax.rsqrt(ms + RMS_EPS)) * g_ref[...]


def _combine(ys, dest_flat, h1, gates, g_final, p):
    tg, d, r = p["tg"], p["d"], p["r"]
    nt = r // tg
    return pl.pallas_call(
        functools.partial(_combine_kernel, p=p),
        grid=(nt,),
        in_specs=[
            pl.BlockSpec((tg * TOP_K,), lambda i: (i,), memory_space=pltpu.SMEM),
            pl.BlockSpec((tg * TOP_K,), lambda i: (jnp.minimum(i + 1, nt - 1),), memory_space=pltpu.SMEM),
            pl.BlockSpec((tg, d), lambda i: (i, 0)),
            pl.BlockSpec((tg, LANES), lambda i: (i, 0)),
            pl.BlockSpec((1, d), lambda i: (0, 0)),
            pl.BlockSpec(memory_space=pl.ANY),
        ],
        out_specs=pl.BlockSpec((tg, d), lambda i: (i, 0)),
        out_shape=jax.ShapeDtypeStruct((r, d), F32),
        scratch_shapes=[pltpu.VMEM((2, TOP_K, tg, d), F32), pltpu.SemaphoreType.DMA((2,))],
        compiler_params=pltpu.CompilerParams(dimension_semantics=("arbitrary",)),
        name="combine",
    )(dest_flat, dest_flat, h1, gates, g_final, ys)


def kernel(x, meta_tokens, norm_mix_g, w_in, conv_w, sink, w_attn_o, w_conv_o, w_out, norm_ffn_g,
           router_w, router_b, w_gate_up, b_gate_up, w_down, b_down, norm_final_g):
    b, seq, d = x.shape
    n_exp, d_ff = w_down.shape[1], w_down.shape[2]
    assert norm_mix_g.shape[0] == 1 and meta_tokens.shape[0] == N_META
    p = _plan(b, seq, d, n_exp, d_ff)
    x2 = x.reshape(p["r"], d)

    qkv, rest, qkvm, restm = _inproj(x2, meta_tokens.astype(F32), norm_mix_g[0].reshape(1, d), w_in[0], p)
    attn_o = _attention(qkv, qkvm, sink[0].astype(F32), p)
    h1, hn2, ti, gates, rank, counts = _mixer(
        attn_o, rest, restm, x2, conv_w[0], w_attn_o[0].astype(BF16), w_conv_o[0].astype(BF16),
        w_out[0].astype(BF16), norm_ffn_g[0].reshape(1, d), router_w[0].astype(BF16),
        router_b[0].reshape(1, n_exp), p)
    row_start, cnt, start, ce, cts, cnt_tiles, used = _routing_tables(counts, p)
    dest = _dest(ti, rank, row_start, p)
    dest_flat = dest[:, :TOP_K].reshape(-1)
    xs = _dispatch(hn2, dest_flat, cnt, start, p)
    ys = _moe(xs, ce, cts, cnt_tiles, used, w_gate_up[0], b_gate_up[0], w_down[0], b_down[0], p)
    out = _combine(ys, dest_flat, h1, gates, norm_final_g.reshape(1, d), p)
    return out.reshape(b, seq, d)
```

```python
import functools

import numpy as np
import jax
import jax.numpy as jnp
from jax import lax
from jax.experimental import pallas as pl
from jax.experimental.pallas import tpu as pltpu

N_META = 16
BLK = 128
WINDOW = 128
HEAD_DIM = 64
Q_PER_KV = 4
ROT_DIM = HEAD_DIM // 4
ROPE_THETA = 500000.0
CONV_K = 3
TOP_K = 4
SWIGLU_ALPHA = 1.702
SWIGLU_LIMIT = 7.0
RMS_EPS = 1e-5

LANES = 128
SUBLANES = 8
VMEM_LIMIT_CAP = 60000 * 1024
MOE_ROW_TILE = 256
MOE_CHUNK_TILES = 5
MOE_W_SLOTS = 3
MOE_HALF_STEPS = (7, 9)

F32 = jnp.float32
BF16 = jnp.bfloat16
I32 = jnp.int32


def _vmem_limit(nbytes):
    return int(min(VMEM_LIMIT_CAP, max(32 * 1024 * 1024, nbytes * 5 // 4 + (4 << 20))))


def _plan(b, seq, d, n_exp, d_ff):
    attn = (d // 128) * HEAD_DIM
    kvd = attn // Q_PER_KV
    conv = d // 2
    r = b * seq
    p = dict(b=b, seq=seq, d=d, n_exp=n_exp, d_ff=d_ff, attn=attn, kvd=kvd, conv=conv, r=r)
    p["nq"] = attn // HEAD_DIM
    p["nkv"] = kvd // HEAD_DIM
    p["in_dim"] = attn + 2 * kvd + 3 * conv + 2 * d
    p["tn"] = 2 * kvd
    p["tm"] = min(1024, seq)
    p["n_qkv_tiles"] = (attn + 2 * kvd) // p["tn"]
    p["n_col_tiles"] = p["in_dim"] // p["tn"]
    p["rest_w"] = 3 * conv + 2 * d
    p["tc"] = min(256, seq)
    p["te"] = min(256, seq)
    p["tg"] = min(128, seq)
    p["tf"] = min(256, d_ff)
    p["nf"] = d_ff // p["tf"]
    p["rt"] = MOE_ROW_TILE
    p["tpc"] = MOE_CHUNK_TILES
    p["cap"] = MOE_ROW_TILE * MOE_CHUNK_TILES
    p["n_tiles_max"] = (r * TOP_K) // p["rt"] + n_exp
    p["n_chunks_max"] = n_exp + (p["n_tiles_max"] - n_exp) // p["tpc"]
    p["n_slots"] = p["n_tiles_max"] * p["rt"]
    assert seq % p["tm"] == 0 and seq % p["tc"] == 0 and seq % BLK == 0
    assert attn % p["tn"] == 0 and p["in_dim"] % p["tn"] == 0 and kvd % LANES == 0
    assert (r * TOP_K) % p["rt"] == 0 and d_ff % p["tf"] == 0
    assert p["rt"] & (p["rt"] - 1) == 0, "the zero-fill decomposition needs a power-of-two row tile"
    return p


def _rope_tables(seq):
    half = ROT_DIM // 2
    pos = jnp.arange(N_META + seq, dtype=F32)
    inv_freq = ROPE_THETA ** (-jnp.arange(0, ROT_DIM, 2, dtype=F32) / ROT_DIM)
    ang = pos[:, None] * inv_freq[None, :]
    cos, sin = jnp.cos(ang), jnp.sin(ang)
    lane = np.arange(LANES) % HEAD_DIM
    idx = np.where(lane < ROT_DIM, lane % half, 0)
    rot = jnp.asarray(lane < ROT_DIM)
    sign = jnp.asarray(np.where(lane < half, -1.0, 1.0).astype(np.float32))
    cos_l = jnp.where(rot[None, :], cos[:, idx], 1.0)
    sin_l = jnp.where(rot[None, :], sin[:, idx] * sign[None, :], 0.0)
    tab = jnp.stack([cos_l, sin_l]).astype(F32)
    return tab[:, N_META:], tab[:, :N_META]


def _rope(t, cos, sin, n_cols):
    lane = lax.broadcasted_iota(I32, (t.shape[0], LANES), 1)
    first = (lane % HEAD_DIM) < (ROT_DIM // 2)
    outs = []
    for c in range(t.shape[1] // LANES):
        s = t[:, c * LANES:(c + 1) * LANES]
        if c * LANES < n_cols:
            partner = jnp.where(first, pltpu.roll(s, LANES - ROT_DIM // 2, 1), pltpu.roll(s, ROT_DIM // 2, 1))
            s = s * cos + partner * sin
        outs.append(s)
    return jnp.concatenate(outs, axis=1)


def _inproj_kernel(x_ref, meta_ref, g_ref, w_ref, csx_ref, csm_ref,
                   qkv_ref, rest_ref, qkvm_ref, restm_ref, hn_ref, *, p):
    tm, tn, kvd = p["tm"], p["tn"], p["kvd"]
    n_q_tiles = p["attn"] // tn
    n = pl.program_id(1)

    @pl.when(n == 0)
    def _():
        g = g_ref[...]

        def norm(v):
            ms = jnp.mean(v * v, axis=-1, keepdims=True)
            return ((v * lax.rsqrt(ms + RMS_EPS)) * g).astype(BF16)

        rows = min(128, tm)

        def body(i, _):
            r0 = pl.multiple_of(i * rows, rows)
            hn_ref[pl.ds(r0, rows), :] = norm(x_ref[pl.ds(r0, rows), :])
            return 0

        lax.fori_loop(0, tm // rows, body, 0)
        hn_ref[tm:tm + N_META, :] = norm(meta_ref[...])

    def project():
        res = jnp.dot(hn_ref[...], w_ref[...].astype(BF16), preferred_element_type=F32)
        return res[:tm], res[tm:]

    def store_qkv(n_cols, scale):
        res_x, res_m = project()
        rx = _rope(res_x, csx_ref[0], csx_ref[1], n_cols)
        rm = _rope(res_m, csm_ref[0], csm_ref[1], n_cols)
        if scale != 1.0:
            rx = rx * scale
        qkv_ref[...] = rx.astype(BF16)
        qkvm_ref[0] = rm.astype(BF16)

    @pl.when(n < n_q_tiles)
    def _():
        store_qkv(tn, HEAD_DIM ** -0.5)

    @pl.when(n == n_q_tiles)
    def _():
        store_qkv(kvd, 1.0)

    @pl.when(n >= p["n_qkv_tiles"])
    def _():
        res_x, res_m = project()
        rest_ref[...] = res_x.astype(BF16)
        restm_ref[0] = res_m.astype(BF16)


def _inproj(x2, meta, g, w_in, p):
    tm, tn, d, r = p["tm"], p["tn"], p["d"], p["r"]
    nm, nn, nqkv = r // tm, p["n_col_tiles"], p["n_qkv_tiles"]
    csx, csm = _rope_tables(p["seq"])
    spt = p["seq"] // tm
    qkv_w = nqkv * tn
    est = (2 * tm * d * 4 + (tm + 16) * d * 2 + 2 * d * tn * 4 + d * tn * 2 + 4 * tm * tn * 2
           + 4 * 2 * tm * LANES * 4 + 4 * (tm + 16) * tn * 4)
    return pl.pallas_call(
        functools.partial(_inproj_kernel, p=p),
        grid=(nm, nn),
        in_specs=[
            pl.BlockSpec((tm, d), lambda m, n: (m, 0)),
            pl.BlockSpec((N_META, d), lambda m, n: (0, 0)),
            pl.BlockSpec((1, d), lambda m, n: (0, 0)),
            pl.BlockSpec((d, tn), lambda m, n: (0, n)),
            pl.BlockSpec((2, tm, LANES), lambda m, n: (0, m % spt, 0)),
            pl.BlockSpec((2, N_META, LANES), lambda m, n: (0, 0, 0)),
        ],
        out_specs=[
            pl.BlockSpec((tm, tn), lambda m, n: (m, jnp.minimum(n, nqkv - 1))),
            pl.BlockSpec((tm, tn), lambda m, n: (m, jnp.maximum(n - nqkv, 0))),
            pl.BlockSpec((1, N_META, tn), lambda m, n: (m, 0, jnp.minimum(n, nqkv - 1))),
            pl.BlockSpec((1, N_META, tn), lambda m, n: (m, 0, jnp.maximum(n - nqkv, 0))),
        ],
        out_shape=[
            jax.ShapeDtypeStruct((r, qkv_w), BF16),
            jax.ShapeDtypeStruct((r, p["rest_w"]), BF16),
            jax.ShapeDtypeStruct((nm, N_META, qkv_w), BF16),
            jax.ShapeDtypeStruct((nm, N_META, p["rest_w"]), BF16),
        ],
        scratch_shapes=[pltpu.VMEM((tm + N_META, d), BF16)],
        compiler_params=pltpu.CompilerParams(
            dimension_semantics=("arbitrary", "arbitrary"), vmem_limit_bytes=_vmem_limit(est)),
        name="inproj",
    )(x2, meta, g, w_in, csx, csm)


def _attn_kernel(sink_ref, q_ref, k0_ref, k1_ref, k2_ref, v0_ref, v1_ref, v2_ref, km_ref, vm_ref,
                 o_ref, kcat_ref, vcat_ref, *, p):
    seq, nkv = p["seq"], p["nkv"]
    nband = 3 * BLK
    nkeys = nband + N_META
    n = pl.program_id(1)
    for j, (kr, vr) in enumerate(((k0_ref, v0_ref), (k1_ref, v1_ref), (k2_ref, v2_ref))):
        kcat_ref[j * BLK:(j + 1) * BLK, :] = kr[...]
        vcat_ref[j * BLK:(j + 1) * BLK, :] = vr[...]
    kcat_ref[nband:nkeys, :] = km_ref[...]
    vcat_ref[nband:nkeys, :] = vm_ref[...]

    qi = lax.broadcasted_iota(I32, (BLK, nkeys), 0)
    sj = lax.broadcasted_iota(I32, (BLK, nkeys), 1)
    kx = (n - 1) * BLK + sj
    dq = n * BLK + qi - kx
    visible = (sj >= nband) | ((jnp.abs(dq) <= WINDOW) & (kx >= 0) & (kx < seq))
    neg = jnp.finfo(F32).min
    ones = jnp.ones((nkeys, HEAD_DIM), BF16)
    gi = lax.broadcasted_iota(I32, (Q_PER_KV, 1, 1), 0)
    for h in range(nkv):
        kh = kcat_ref[:, h * HEAD_DIM:(h + 1) * HEAD_DIM]
        vh = jnp.concatenate([vcat_ref[:, h * HEAD_DIM:(h + 1) * HEAD_DIM], ones], axis=1)
        heads = [h * Q_PER_KV + g for g in range(Q_PER_KV)]
        qs = jnp.concatenate([q_ref[:, hd * HEAD_DIM:(hd + 1) * HEAD_DIM] for hd in heads], axis=0)
        s = lax.dot_general(qs, kh, (((1,), (1,)), ((), ())), preferred_element_type=F32)
        s = jnp.where(visible[None], s.reshape(Q_PER_KV, BLK, nkeys), neg)
        snk = jnp.zeros((Q_PER_KV, 1, 1), F32)
        for g, hd in enumerate(heads):
            snk = jnp.where(gi == g, sink_ref[hd], snk)
        m = jnp.maximum(jnp.max(s, axis=2, keepdims=True), snk)
        e = jnp.exp(s - m).reshape(Q_PER_KV * BLK, nkeys)
        pv = jnp.dot(e.astype(BF16), vh, preferred_element_type=F32)
        pv = pv.reshape(Q_PER_KV, BLK, 2 * HEAD_DIM)
        o = pv[:, :, :HEAD_DIM] / (pv[:, :, HEAD_DIM:HEAD_DIM + 1] + jnp.exp(snk - m))
        for g, hd in enumerate(heads):
            o_ref[:, hd * HEAD_DIM:(hd + 1) * HEAD_DIM] = o[g].astype(BF16)


def _attention(qkv, qkvm, sink, p):
    attn, kvd, seq, b = p["attn"], p["kvd"], p["seq"], p["b"]
    nbx = seq // BLK
    kc = attn // kvd
    nkeys = 3 * BLK + N_META

    def kv_spec(off, col):
        return pl.BlockSpec((BLK, kvd), lambda bi, n: (bi * nbx + jnp.clip(n + off, 0, nbx - 1), col))

    return pl.pallas_call(
        functools.partial(_attn_kernel, p=p),
        grid=(b, nbx),
        in_specs=[
            pl.BlockSpec(memory_space=pltpu.SMEM),
            pl.BlockSpec((BLK, attn), lambda bi, n: (bi * nbx + n, 0)),
            kv_spec(-1, kc), kv_spec(0, kc), kv_spec(1, kc),
            kv_spec(-1, kc + 1), kv_spec(0, kc + 1), kv_spec(1, kc + 1),
            pl.BlockSpec((None, N_META, kvd), lambda bi, n: (0, 0, kc)),
            pl.BlockSpec((None, N_META, kvd), lambda bi, n: (0, 0, kc + 1)),
        ],
        out_specs=pl.BlockSpec((BLK, attn), lambda bi, n: (bi * nbx + n, 0)),
        out_shape=jax.ShapeDtypeStruct((p["r"], attn), BF16),
        scratch_shapes=[pltpu.VMEM((nkeys, kvd), BF16), pltpu.VMEM((nkeys, kvd), BF16)],
        compiler_params=pltpu.CompilerParams(dimension_semantics=("arbitrary", "arbitrary")),
        name="attn",
    )(sink, qkv, qkv, qkv, qkv, qkv, qkv, qkv, qkvm, qkvm)


def _lane_pack(cols, rows, dtype):
    lane = lax.broadcasted_iota(I32, (rows, LANES), 1)
    out = jnp.zeros((rows, LANES), dtype)
    for k, c in enumerate(cols):
        out = jnp.where(lane == k, c.astype(dtype), out)
    return out


def _mixer_kernel(attn_ref, rest_ref, prev_ref, next_ref, restm_ref, x_ref, cw_ref, wao_ref, wco_ref,
                  wout_ref, g_ref, rw_ref, rb_ref,
                  h1_ref, hn2_ref, ti_ref, gate_ref, rank_ref, cnt_ref, carry_ref, *, p):
    tc, conv, d, n_exp, seq = p["tc"], p["conv"], p["d"], p["n_exp"], p["seq"]
    i = pl.program_id(0)
    tiles_per_seq = seq // tc
    is_first = (i % tiles_per_seq) == 0
    is_last = (i % tiles_per_seq) == tiles_per_seq - 1
    o_ch, o_cb, o_cc, o_ga, o_gc = 0, conv, 2 * conv, 3 * conv, 3 * conv + d

    @pl.when(i == 0)
    def _():
        carry_ref[...] = jnp.zeros_like(carry_ref)

    def u_of(ref):
        return ref[:, o_cc:o_cc + conv].astype(F32) * ref[:, o_ch:o_ch + conv].astype(F32)

    u = u_of(rest_ref)
    last = N_META - 1
    u_prev = jnp.where(is_first, u_of(restm_ref)[last:last + 1], u_of(prev_ref)[last:last + 1])
    u_next = jnp.where(is_last, 0.0, u_of(next_ref)[0:1])
    row = lax.broadcasted_iota(I32, (tc, conv), 0)
    u_m1 = jnp.where(row == 0, u_prev, pltpu.roll(u, 1, 0))
    u_p1 = jnp.where(row == tc - 1, u_next, pltpu.roll(u, tc - 1, 0))
    cw = cw_ref[...]
    cv = u_m1 * cw[0:1] + u * cw[1:2] + u_p1 * cw[2:3]
    yc_in = (rest_ref[:, o_cb:o_cb + conv].astype(F32) * cv).astype(BF16)
    y_conv = jnp.dot(yc_in, wco_ref[...], preferred_element_type=F32)
    y_attn = jnp.dot(attn_ref[...], wao_ref[...], preferred_element_type=F32)
    g_a = rest_ref[:, o_ga:o_ga + d].astype(F32)
    g_c = rest_ref[:, o_gc:o_gc + d].astype(F32)
    merged = jax.nn.sigmoid(g_a) * y_attn + jax.nn.sigmoid(g_c) * y_conv
    h1 = x_ref[...] + jnp.dot(merged.astype(BF16), wout_ref[...], preferred_element_type=F32)
    h1_ref[...] = h1
    ms = jnp.mean(h1 * h1, axis=-1, keepdims=True)
    hn2 = (h1 * lax.rsqrt(ms + RMS_EPS)) * g_ref[...]
    hn2_ref[...] = hn2

    logits = jnp.dot(hn2.astype(BF16), rw_ref[...], preferred_element_type=F32) + rb_ref[...]
    lane = lax.broadcasted_iota(I32, (tc, n_exp), 1).astype(F32)
    sel = jnp.zeros((tc, n_exp), F32)
    tv, ti = [], []
    cur = logits
    for _ in range(TOP_K):
        m = jnp.max(cur, axis=1, keepdims=True)
        idx = jnp.min(jnp.where(cur == m, lane, float(n_exp)), axis=1, keepdims=True)
        hit = lane == idx
        tv.append(m)
        ti.append(idx)
        sel = jnp.where(hit, 1.0, sel)
        cur = jnp.where(hit, -jnp.inf, cur)
    ex = [jnp.exp(v - tv[0]) for v in tv]
    tot = ex[0] + ex[1] + ex[2] + ex[3]
    gates = [e / tot for e in ex]

    r_i = lax.broadcasted_iota(I32, (tc, tc), 0)
    c_i = lax.broadcasted_iota(I32, (tc, tc), 1)
    lower = jnp.where(r_i > c_i, 1.0, 0.0).astype(BF16)
    before = jnp.dot(lower, sel.astype(BF16), preferred_element_type=F32) + carry_ref[0:1, 0:n_exp]
    ranks = [jnp.sum(jnp.where(lane == t, before, 0.0), axis=1, keepdims=True) for t in ti]
    carry_ref[0:1, 0:n_exp] = carry_ref[0:1, 0:n_exp] + jnp.sum(sel, axis=0, keepdims=True)

    ti_ref[...] = _lane_pack(ti, tc, I32)
    gate_ref[...] = _lane_pack(gates, tc, F32)
    rank_ref[...] = _lane_pack(ranks, tc, I32)
    cnt_ref[...] = carry_ref[...]


def _mixer(attn_o, rest, restm, x2, conv_w, wao, wco, wout, g_ffn, router_w, router_b, p):
    tc, d, r, rw, n_exp = p["tc"], p["d"], p["r"], p["rest_w"], p["n_exp"]
    nt = r // tc
    sub = tc // N_META
    n16 = r // N_META
    const = lambda shape: pl.BlockSpec(shape, lambda i: (0,) * len(shape))
    est = (2 * (tc * p["attn"] * 2 + tc * rw * 2 + tc * d * 4 + 3 * N_META * rw * 2)
           + 2 * (p["attn"] * d + p["conv"] * d + d * d + d * n_exp) * 2
           + 2 * (2 * tc * d * 4 + 3 * tc * LANES * 4) + 10 * tc * d * 4)
    return pl.pallas_call(
        functools.partial(_mixer_kernel, p=p),
        grid=(nt,),
        in_specs=[
            pl.BlockSpec((tc, p["attn"]), lambda i: (i, 0)),
            pl.BlockSpec((tc, rw), lambda i: (i, 0)),
            pl.BlockSpec((N_META, rw), lambda i: (jnp.maximum(i * sub - 1, 0), 0)),
            pl.BlockSpec((N_META, rw), lambda i: (jnp.minimum((i + 1) * sub, n16 - 1), 0)),
            pl.BlockSpec((None, N_META, rw), lambda i: (0, 0, 0)),
            pl.BlockSpec((tc, d), lambda i: (i, 0)),
            const((CONV_K, p["conv"])),
            const((p["attn"], d)), const((p["conv"], d)), const((d, d)),
            const((1, d)), const((d, n_exp)), const((1, n_exp)),
        ],
        out_specs=[
            pl.BlockSpec((tc, d), lambda i: (i, 0)),
            pl.BlockSpec((tc, d), lambda i: (i, 0)),
            pl.BlockSpec((tc, LANES), lambda i: (i, 0)),
            pl.BlockSpec((tc, LANES), lambda i: (i, 0)),
            pl.BlockSpec((tc, LANES), lambda i: (i, 0)),
            pl.BlockSpec((8, LANES), lambda i: (0, 0)),
        ],
        out_shape=[
            jax.ShapeDtypeStruct((r, d), F32),
            jax.ShapeDtypeStruct((r, d), F32),
            jax.ShapeDtypeStruct((r, LANES), I32),
            jax.ShapeDtypeStruct((r, LANES), F32),
            jax.ShapeDtypeStruct((r, LANES), I32),
            jax.ShapeDtypeStruct((8, LANES), F32),
        ],
        scratch_shapes=[pltpu.VMEM((8, LANES), F32)],
        compiler_params=pltpu.CompilerParams(
            dimension_semantics=("arbitrary",), vmem_limit_bytes=_vmem_limit(est)),
        name="mixer",
    )(attn_o, rest, rest, rest, restm, x2, conv_w, wao, wco, wout, g_ffn, router_w, router_b)


def _dest_kernel(ti_ref, rank_ref, start_ref, dest_ref):
    rows = ti_ref.shape[0]
    lane = lax.broadcasted_iota(I32, (rows, LANES), 1).astype(F32)
    ti = ti_ref[...].astype(F32)
    rank = rank_ref[...].astype(F32)
    start = start_ref[0:1, :].astype(F32)
    out = jnp.zeros((rows, LANES), F32)
    for k in range(TOP_K):
        e_k = jnp.sum(jnp.where(lane == k, ti, 0.0), axis=1, keepdims=True)
        r_k = jnp.sum(jnp.where(lane == k, rank, 0.0), axis=1, keepdims=True)
        s_k = jnp.sum(jnp.where(lane == e_k, start, 0.0), axis=1, keepdims=True)
        out = jnp.where(lane == k, s_k + r_k, out)
    dest_ref[...] = out.astype(I32)


def _dest(ti, rank, row_start, p):
    r = p["r"]
    rows = min(1024, r)
    return pl.pallas_call(
        _dest_kernel,
        grid=(r // rows,),
        in_specs=[pl.BlockSpec((rows, LANES), lambda i: (i, 0)),
                  pl.BlockSpec((rows, LANES), lambda i: (i, 0)),
                  pl.BlockSpec((8, LANES), lambda i: (0, 0))],
        out_specs=pl.BlockSpec((rows, LANES), lambda i: (i, 0)),
        out_shape=jax.ShapeDtypeStruct((r, LANES), I32),
        name="dest",
    )(ti, rank, row_start)


def _moe_row_steps(tpc):
    return sorted(set(range(2, 2 * tpc + 1, 2)) | {h for h in MOE_HALF_STEPS if h < 2 * tpc})


def _routing_tables(counts, p):
    rt, tpc, n_exp, nch = p["rt"], p["tpc"], p["n_exp"], p["n_chunks_max"]
    cnt = counts[0, :n_exp].astype(I32)
    ntile = (cnt + rt - 1) // rt
    tile_start = jnp.cumsum(ntile) - ntile
    nchunk = (ntile + tpc - 1) // tpc
    chunk_end = jnp.cumsum(nchunk)
    c = jnp.arange(nch, dtype=I32)
    ce = jnp.minimum(jnp.sum((chunk_end[None, :] <= c[:, None]).astype(I32), axis=1), n_exp - 1)
    first = c - (chunk_end - nchunk)[ce]
    c_nt = jnp.clip(ntile[ce] - first * tpc, 0, tpc)
    c_nt = jnp.where(c < chunk_end[-1], c_nt, 0)
    c_ts = tile_start[ce] + first * tpc
    last_e = ce[jnp.maximum(chunk_end[-1] - 1, 0)]
    ce = jnp.where(c_nt > 0, ce, last_e)
    row_start = jnp.zeros((8, LANES), I32).at[0, :n_exp].set(tile_start * rt)
    used = jnp.stack([jnp.sum(ntile), chunk_end[-1]]).astype(I32)
    valid = jnp.clip(cnt[ce] - first * (tpc * rt), 0, c_nt * rt)
    halves = (valid + rt // 2 - 1) // (rt // 2)
    steps = jnp.asarray(_moe_row_steps(tpc), I32)
    c_sel = steps[jnp.minimum(jnp.searchsorted(steps, halves), len(_moe_row_steps(tpc)) - 1)]
    c_sel = jnp.where(c_nt > 0, c_sel, 0).astype(I32)
    return row_start, cnt, tile_start * rt, ce.astype(I32), c_ts.astype(I32), c_nt.astype(I32), used, c_sel


def _dispatch_kernel(cnt_ref, start_ref, dest_ref, hn_ref, xs_ref, zero_ref, sem, zsem, *, p):
    te, n_exp, rt, d = p["te"], p["n_exp"], p["rt"], p["d"]
    step = pl.program_id(0)

    @pl.when(step == 0)
    def _():
        zero_ref[...] = jnp.zeros_like(zero_ref)

        def per_expert(e, wait):
            def zero_rows(dst0, size):
                cp = pltpu.make_async_copy(zero_ref.at[pl.ds(0, size)], xs_ref.at[pl.ds(dst0, size)], zsem)
                cp.wait() if wait else cp.start()

            cnt = cnt_ref[e]
            cur = start_ref[e] + cnt
            pad = (rt - (cnt & (rt - 1))) & (rt - 1)
            head = (SUBLANES - (cur & (SUBLANES - 1))) & (SUBLANES - 1)
            for j in range(SUBLANES - 1):
                @pl.when(j < head)
                def _(j=j):
                    zero_rows(cur + j, 1)
            cur = cur + head
            rem = pad - head
            size = SUBLANES
            while size < rt:
                @pl.when((rem & size) != 0)
                def _(cur=cur, size=size):
                    zero_rows(pl.multiple_of(cur, SUBLANES), size)
                cur = cur + (rem & size)
                size *= 2

        lax.fori_loop(0, n_exp, lambda e, _: (per_expert(e, False), 0)[1], 0)
        lax.fori_loop(0, n_exp, lambda e, _: (per_expert(e, True), 0)[1], 0)

        half = rt // 2
        used = (start_ref[n_exp - 1] + cnt_ref[n_exp - 1] + rt - 1) // rt

        def tail_copy(t, j):
            r0 = pl.multiple_of(t * rt + j * half, half)
            return pltpu.make_async_copy(zero_ref, xs_ref.at[pl.ds(r0, half)], zsem)

        def tail_start(t, _):
            tail_copy(t, 0).start()
            tail_copy(t, 1).start()
            return 0

        def tail_wait(t, _):
            tail_copy(t, 0).wait()
            tail_copy(t, 1).wait()
            return 0

        lax.fori_loop(used, p["n_tiles_max"], tail_start, 0)
        lax.fori_loop(used, p["n_tiles_max"], tail_wait, 0)

    def row_copy(i, k):
        return pltpu.make_async_copy(hn_ref.at[pl.ds(i, 1)], xs_ref.at[pl.ds(dest_ref[i * TOP_K + k], 1)], sem)

    def issue(i, _):
        for k in range(TOP_K):
            row_copy(i, k).start(priority=k % 2)
        return 0

    lax.fori_loop(0, te, issue, 0)

    for k in range(TOP_K):
        pltpu.make_async_copy(hn_ref, xs_ref.at[pl.ds(0, te)], sem).wait()


def _dispatch(hn2, dest_flat, cnt, start, p):
    te, d, r = p["te"], p["d"], p["r"]
    return pl.pallas_call(
        functools.partial(_dispatch_kernel, p=p),
        grid=(r // te,),
        in_specs=[
            pl.BlockSpec(memory_space=pltpu.SMEM),
            pl.BlockSpec(memory_space=pltpu.SMEM),
            pl.BlockSpec((te * TOP_K,), lambda i: (i,), memory_space=pltpu.SMEM),
            pl.BlockSpec((te, d), lambda i: (i, 0)),
        ],
        out_specs=pl.BlockSpec(memory_space=pl.ANY),
        out_shape=jax.ShapeDtypeStruct((p["n_slots"], d), F32),
        scratch_shapes=[pltpu.VMEM((p["rt"] // 2, d), F32), pltpu.SemaphoreType.DMA, pltpu.SemaphoreType.DMA],
        compiler_params=pltpu.CompilerParams(dimension_semantics=("arbitrary",)),
        name="dispatch",
    )(cnt, start, dest_flat, hn2)


def _moe_kernel(ce_ref, cts_ref, cnt_ref, used_ref, csel_ref, xs_ref, wgu_ref, bgu_ref, wd_ref, bd_ref,
                ys_ref, xbuf_ref, yacc_ref, stage_ref, wg_ring, wu_ring, wd_ring, xsem, ysem, wsem, *, p):
    rt, nf, nch, tf, d_ff, nb = p["rt"], p["nf"], p["n_chunks_max"], p["tf"], p["d_ff"], MOE_W_SLOTS
    c = pl.program_id(0)
    nt = cnt_ref[c]
    c_next = jnp.minimum(c + 1, nch - 1)
    nt_next = jnp.where(c + 1 < nch, cnt_ref[c_next], 0)
    n_w_tiles = used_ref[1] * nf

    def w_copies(g):
        chunk = lax.div(g, nf)
        fg = lax.rem(g, nf)
        e = ce_ref[chunk]
        slot = lax.rem(g, nb)
        c0 = pl.multiple_of(fg * tf, tf)
        return (
            pltpu.make_async_copy(wgu_ref.at[e, :, pl.ds(c0, tf)], wg_ring.at[slot], wsem.at[slot]),
            pltpu.make_async_copy(wgu_ref.at[e, :, pl.ds(d_ff + c0, tf)], wu_ring.at[slot], wsem.at[slot]),
            pltpu.make_async_copy(wd_ref.at[e, pl.ds(c0, tf), :], wd_ring.at[slot], wsem.at[slot]),
        )

    def w_start(g):
        @pl.when(g < n_w_tiles)
        def _():
            for cp in w_copies(g):
                cp.start()

    acc = lax.rem(c, 2)

    def x_copy(chunk, t):
        src0 = pl.multiple_of((cts_ref[chunk] + t) * rt, rt)
        return pltpu.make_async_copy(xs_ref.at[pl.ds(src0, rt)], stage_ref.at[t], xsem.at[t])

    def y_copy(chunk, t):
        a = lax.rem(chunk, 2)
        r0 = pl.multiple_of(t * rt, rt)
        dst0 = pl.multiple_of((cts_ref[chunk] + t) * rt, rt)
        return pltpu.make_async_copy(yacc_ref.at[a, pl.ds(r0, rt)], ys_ref.at[pl.ds(dst0, rt)], ysem.at[a])

    def x_start_all(chunk, n_tiles):
        for t in range(p["tpc"]):
            @pl.when(t < n_tiles)
            def _(t=t):
                x_copy(chunk, t).start()

    @pl.when(nt > 0)
    def _():
        @pl.when(c == 0)
        def _():
            for g in range(nb):
                w_start(g)
            x_start_all(c, nt)

        def load(t, _):
            x_copy(c, t).wait()
            r0 = pl.multiple_of(t * rt, rt)
            xbuf_ref[pl.ds(r0, rt), :] = stage_ref[t].astype(BF16)
            return 0

        lax.fori_loop(0, nt, load, 0)

        @pl.when(nt_next > 0)
        def _():
            x_start_all(c_next, nt_next)

        bd = bd_ref[0]

        def init(t, _):
            r0 = pl.multiple_of(t * rt, rt)
            yacc_ref[acc, pl.ds(r0, rt), :] = jnp.broadcast_to(bd, (rt, p["d"]))
            return 0

        lax.fori_loop(0, nt, init, 0)

        def hidden_tile(f, _):
            g = c * nf + f
            slot = lax.rem(g, nb)
            for cp in w_copies(g):
                cp.wait()

            bg = bgu_ref[0, pl.ds(f, 1), :]
            bu = bgu_ref[0, pl.ds(nf + f, 1), :]

            def ffn(rows):
                xt = xbuf_ref[0:rows, :]
                gate = jnp.dot(xt, wg_ring[slot].astype(BF16), preferred_element_type=F32) + bg
                up = jnp.dot(xt, wu_ring[slot].astype(BF16), preferred_element_type=F32) + bu
                gate = jnp.minimum(gate, SWIGLU_LIMIT)
                up = jnp.clip(up, -SWIGLU_LIMIT, SWIGLU_LIMIT)
                act = (up + 1.0) * gate * jax.nn.sigmoid(SWIGLU_ALPHA * gate)
                yacc_ref[acc, 0:rows, :] += jnp.dot(act.astype(BF16), wd_ring[slot].astype(BF16),
                                                    preferred_element_type=F32)

            for hs in _moe_row_steps(p["tpc"]):
                @pl.when(csel_ref[c] == hs)
                def _(hs=hs):
                    ffn(hs * (rt // 2))

            w_start(g + nb)
            return 0

        lax.fori_loop(0, nf, hidden_tile, 0)

        def drain(chunk):
            def body(t, _):
                y_copy(chunk, t).wait()
                return 0

            lax.fori_loop(0, cnt_ref[chunk], body, 0)

        @pl.when(c > 0)
        def _():
            drain(jnp.maximum(c - 1, 0))

        def store(t, _):
            y_copy(c, t).start()
            return 0

        lax.fori_loop(0, nt, store, 0)

        @pl.when(nt_next == 0)
        def _():
            drain(c)

    @pl.when(c == nch - 1)
    def _():
        stage_ref[0] = jnp.zeros((rt, p["d"]), F32)

        def tail_copy(t):
            return pltpu.make_async_copy(stage_ref.at[0], ys_ref.at[pl.ds(pl.multiple_of(t * rt, rt), rt)],
                                         ysem.at[0])

        def tail_start(t, _):
            tail_copy(t).start()
            return 0

        def tail_wait(t, _):
            tail_copy(t).wait()
            return 0

        lax.fori_loop(used_ref[0], p["n_tiles_max"], tail_start, 0)
        lax.fori_loop(used_ref[0], p["n_tiles_max"], tail_wait, 0)


def _moe(xs, ce, cts, cnt, used, csel, w_gate_up, b_gate_up, w_down, b_down, p):
    d, tf, nf, rt, cap = p["d"], p["tf"], p["nf"], p["rt"], p["cap"]
    n_exp = p["n_exp"]
    nb = MOE_W_SLOTS
    bgu = b_gate_up.reshape(n_exp, 2 * nf, tf)
    bdn = b_down.reshape(n_exp, 1, d)

    est = (nb * 3 * d * tf * 4 + cap * d * 2 + 2 * cap * d * 4 + cap * d * 4 + 4 * rt * d * 4)
    grid_spec = pltpu.PrefetchScalarGridSpec(
        num_scalar_prefetch=5,
        grid=(p["n_chunks_max"],),
        in_specs=[
            pl.BlockSpec(memory_space=pl.ANY),
            pl.BlockSpec(memory_space=pl.ANY),
            pl.BlockSpec((1, 2 * nf, tf), lambda c, ce_r, cts_r, cnt_r, u_r, s_r: (ce_r[c], 0, 0)),
            pl.BlockSpec(memory_space=pl.ANY),
            pl.BlockSpec((1, 1, d), lambda c, ce_r, cts_r, cnt_r, u_r, s_r: (ce_r[c], 0, 0)),
        ],
        out_specs=pl.BlockSpec(memory_space=pl.ANY),
        scratch_shapes=[
            pltpu.VMEM((cap, d), BF16), pltpu.VMEM((2, cap, d), F32), pltpu.VMEM((p["tpc"], rt, d), F32),
            pltpu.VMEM((nb, d, tf), F32), pltpu.VMEM((nb, d, tf), F32), pltpu.VMEM((nb, tf, d), F32),
            pltpu.SemaphoreType.DMA((p["tpc"],)), pltpu.SemaphoreType.DMA((2,)), pltpu.SemaphoreType.DMA((nb,)),
        ],
    )
    return pl.pallas_call(
        functools.partial(_moe_kernel, p=p),
        grid_spec=grid_spec,
        out_shape=jax.ShapeDtypeStruct((p["n_slots"], d), F32),
        compiler_params=pltpu.CompilerParams(
            dimension_semantics=("arbitrary",), vmem_limit_bytes=_vmem_limit(est)),
        name="moe",
    )(ce, cts, cnt, used, csel, xs, w_gate_up, bgu, w_down, bdn)


def _combine_kernel(dest_ref, dest_next_ref, h1_ref, gate_ref, g_ref, ys_ref, o_ref, ybuf_ref, sem, *, p):
    tg = p["tg"]
    step = pl.program_id(0)
    slot = step % 2

    def gather(d_ref, buf_slot):
        def issue(i, _):
            for k in range(TOP_K):
                pltpu.make_async_copy(ys_ref.at[pl.ds(d_ref[i * TOP_K + k], 1)],
                                      ybuf_ref.at[buf_slot, k, pl.ds(i, 1)],
                                      sem.at[buf_slot]).start(priority=k % 2)
            return 0

        lax.fori_loop(0, tg, issue, 0)

    @pl.when(step == 0)
    def _():
        gather(dest_ref, 0)

    def reduce_tile(cur):
        @pl.when(step + 1 < pl.num_programs(0))
        def _():
            gather(dest_next_ref, 1 - cur)

        for k in range(TOP_K):
            pltpu.make_async_copy(ys_ref.at[pl.ds(0, tg)], ybuf_ref.at[cur, k], sem.at[cur]).wait()

        lane = lax.broadcasted_iota(I32, (tg, LANES), 1)
        gates = gate_ref[...]
        h = h1_ref[...]
        for k in range(TOP_K):
            g_k = jnp.sum(jnp.where(lane == k, gates, 0.0), axis=1, keepdims=True)
            h = h + ybuf_ref[cur, k] * g_k
        ms = jnp.mean(h * h, axis=-1, keepdims=True)
        o_ref[...] = (h * lax.rsqrt(ms + RMS_EPS)) * g_ref[...]

    for cur in range(2):
        @pl.when(slot == cur)
        def _(cur=cur):
            reduce_tile(cur)


def _combine(ys, dest_flat, h1, gates, g_final, p):
    tg, d, r = p["tg"], p["d"], p["r"]
    nt = r // tg
    return pl.pallas_call(
        functools.partial(_combine_kernel, p=p),
        grid=(nt,),
        in_specs=[
            pl.BlockSpec((tg * TOP_K,), lambda i: (i,), memory_space=pltpu.SMEM),
            pl.BlockSpec((tg * TOP_K,), lambda i: (jnp.minimum(i + 1, nt - 1),), memory_space=pltpu.SMEM),
            pl.BlockSpec((tg, d), lambda i: (i, 0)),
            pl.BlockSpec((tg, LANES), lambda i: (i, 0)),
            pl.BlockSpec((1, d), lambda i: (0, 0)),
            pl.BlockSpec(memory_space=pl.ANY),
        ],
        out_specs=pl.BlockSpec((tg, d), lambda i: (i, 0)),
        out_shape=jax.ShapeDtypeStruct((r, d), F32),
        scratch_shapes=[pltpu.VMEM((2, TOP_K, tg, d), F32), pltpu.SemaphoreType.DMA((2,))],
        compiler_params=pltpu.CompilerParams(dimension_semantics=("arbitrary",)),
        name="combine",
    )(dest_flat, dest_flat, h1, gates, g_final, ys)


def kernel(x, meta_tokens, norm_mix_g, w_in, conv_w, sink, w_attn_o, w_conv_o, w_out, norm_ffn_g,
           router_w, router_b, w_gate_up, b_gate_up, w_down, b_down, norm_final_g):
    b, seq, d = x.shape
    n_exp, d_ff = w_down.shape[1], w_down.shape[2]
    assert norm_mix_g.shape[0] == 1 and meta_tokens.shape[0] == N_META
    p = _plan(b, seq, d, n_exp, d_ff)
    x2 = x.reshape(p["r"], d)

    qkv, rest, qkvm, restm = _inproj(x2, meta_tokens.astype(F32), norm_mix_g[0].reshape(1, d), w_in[0], p)
    attn_o = _attention(qkv, qkvm, sink[0].astype(F32), p)
    h1, hn2, ti, gates, rank, counts = _mixer(
        attn_o, rest, restm, x2, conv_w[0], w_attn_o[0].astype(BF16), w_conv_o[0].astype(BF16),
        w_out[0].astype(BF16), norm_ffn_g[0].reshape(1, d), router_w[0].astype(BF16),
        router_b[0].reshape(1, n_exp), p)
    row_start, cnt, start, ce, cts, cnt_tiles, used, csel = _routing_tables(counts, p)
    dest = _dest(ti, rank, row_start, p)
    dest_flat = dest[:, :TOP_K].reshape(-1)
    xs = _dispatch(hn2, dest_flat, cnt, start, p)
    ys = _moe(xs, ce, cts, cnt_tiles, used, csel, w_gate_up[0], b_gate_up[0], w_down[0], b_down[0], p)
    out = _combine(ys, dest_flat, h1, gates, norm_final_g.reshape(1, d), p)
    return out.reshape(b, seq, d)
```

```python
import functools

import numpy as np
import jax
import jax.numpy as jnp
from jax import lax
from jax.experimental import pallas as pl
from jax.experimental.pallas import tpu as pltpu

N_META = 16
BLK = 128
WINDOW = 128
HEAD_DIM = 64
Q_PER_KV = 4
ROT_DIM = HEAD_DIM // 4
ROPE_THETA = 500000.0
CONV_K = 3
TOP_K = 4
SWIGLU_ALPHA = 1.702
SWIGLU_LIMIT = 7.0
RMS_EPS = 1e-5

LANES = 128
SUBLANES = 8
VMEM_LIMIT_CAP = 60000 * 1024
MOE_ROW_TILE = 256
MOE_CHUNK_TILES = 5
MOE_W_SLOTS = 3
MOE_HALF_STEPS = (7, 9)

F32 = jnp.float32
BF16 = jnp.bfloat16
I32 = jnp.int32


def _vmem_limit(nbytes):
    return int(min(VMEM_LIMIT_CAP, max(32 * 1024 * 1024, nbytes * 5 // 4 + (4 << 20))))


def _plan(b, seq, d, n_exp, d_ff):
    attn = (d // 128) * HEAD_DIM
    kvd = attn // Q_PER_KV
    conv = d // 2
    r = b * seq
    p = dict(b=b, seq=seq, d=d, n_exp=n_exp, d_ff=d_ff, attn=attn, kvd=kvd, conv=conv, r=r)
    p["nq"] = attn // HEAD_DIM
    p["nkv"] = kvd // HEAD_DIM
    p["in_dim"] = attn + 2 * kvd + 3 * conv + 2 * d
    p["tn"] = 2 * kvd
    p["tm"] = min(1024, seq)
    p["n_qkv_tiles"] = (attn + 2 * kvd) // p["tn"]
    p["n_col_tiles"] = p["in_dim"] // p["tn"]
    p["rest_w"] = 3 * conv + 2 * d
    p["tc"] = min(256, seq)
    p["te"] = min(256, seq)
    p["tg"] = min(128, seq)
    p["tf"] = min(256, d_ff)
    p["nf"] = d_ff // p["tf"]
    p["rt"] = MOE_ROW_TILE
    p["tpc"] = MOE_CHUNK_TILES
    p["cap"] = MOE_ROW_TILE * MOE_CHUNK_TILES
    p["n_tiles_max"] = (r * TOP_K) // p["rt"] + n_exp
    p["n_chunks_max"] = n_exp + (p["n_tiles_max"] - n_exp) // p["tpc"]
    p["n_slots"] = p["n_tiles_max"] * p["rt"]
    assert seq % p["tm"] == 0 and seq % p["tc"] == 0 and seq % BLK == 0
    assert attn % p["tn"] == 0 and p["in_dim"] % p["tn"] == 0 and kvd % LANES == 0
    assert (r * TOP_K) % p["rt"] == 0 and d_ff % p["tf"] == 0
    assert p["rt"] & (p["rt"] - 1) == 0, "the zero-fill decomposition needs a power-of-two row tile"
    return p


def _rope_tables(seq):
    half = ROT_DIM // 2
    pos = jnp.arange(N_META + seq, dtype=F32)
    inv_freq = ROPE_THETA ** (-jnp.arange(0, ROT_DIM, 2, dtype=F32) / ROT_DIM)
    ang = pos[:, None] * inv_freq[None, :]
    cos, sin = jnp.cos(ang), jnp.sin(ang)
    lane = np.arange(LANES) % HEAD_DIM
    idx = np.where(lane < ROT_DIM, lane % half, 0)
    rot = jnp.asarray(lane < ROT_DIM)
    sign = jnp.asarray(np.where(lane < half, -1.0, 1.0).astype(np.float32))
    cos_l = jnp.where(rot[None, :], cos[:, idx], 1.0)
    sin_l = jnp.where(rot[None, :], sin[:, idx] * sign[None, :], 0.0)
    tab = jnp.stack([cos_l, sin_l]).astype(F32)
    return tab[:, N_META:], tab[:, :N_META]


def _rope(t, cos, sin, n_cols):
    lane = lax.broadcasted_iota(I32, (t.shape[0], LANES), 1)
    first = (lane % HEAD_DIM) < (ROT_DIM // 2)
    outs = []
    for c in range(t.shape[1] // LANES):
        s = t[:, c * LANES:(c + 1) * LANES]
        if c * LANES < n_cols:
            partner = jnp.where(first, pltpu.roll(s, LANES - ROT_DIM // 2, 1), pltpu.roll(s, ROT_DIM // 2, 1))
            s = s * cos + partner * sin
        outs.append(s)
    return jnp.concatenate(outs, axis=1)


def _inproj_kernel(x_ref, meta_ref, g_ref, w_ref, csx_ref, csm_ref,
                   qkv_ref, rest_ref, qkvm_ref, restm_ref, hn_ref, *, p):
    tm, tn, kvd = p["tm"], p["tn"], p["kvd"]
    n_q_tiles = p["attn"] // tn
    n = pl.program_id(1)

    @pl.when(n == 0)
    def _():
        g = g_ref[...]

        def norm(v):
            ms = jnp.mean(v * v, axis=-1, keepdims=True)
            return ((v * lax.rsqrt(ms + RMS_EPS)) * g).astype(BF16)

        rows = min(128, tm)

        def body(i, _):
            r0 = pl.multiple_of(i * rows, rows)
            hn_ref[pl.ds(r0, rows), :] = norm(x_ref[pl.ds(r0, rows), :])
            return 0

        lax.fori_loop(0, tm // rows, body, 0)
        hn_ref[tm:tm + N_META, :] = norm(meta_ref[...])

    def project():
        res = jnp.dot(hn_ref[...], w_ref[...].astype(BF16), preferred_element_type=F32)
        return res[:tm], res[tm:]

    def store_qkv(n_cols, scale):
        res_x, res_m = project()
        rx = _rope(res_x, csx_ref[0], csx_ref[1], n_cols)
        rm = _rope(res_m, csm_ref[0], csm_ref[1], n_cols)
        if scale != 1.0:
            rx = rx * scale
        qkv_ref[...] = rx.astype(BF16)
        qkvm_ref[0] = rm.astype(BF16)

    @pl.when(n < n_q_tiles)
    def _():
        store_qkv(tn, HEAD_DIM ** -0.5)

    @pl.when(n == n_q_tiles)
    def _():
        store_qkv(kvd, 1.0)

    @pl.when(n >= p["n_qkv_tiles"])
    def _():
        res_x, res_m = project()
        rest_ref[...] = res_x.astype(BF16)
        restm_ref[0] = res_m.astype(BF16)


def _inproj(x2, meta, g, w_in, p):
    tm, tn, d, r = p["tm"], p["tn"], p["d"], p["r"]
    nm, nn, nqkv = r // tm, p["n_col_tiles"], p["n_qkv_tiles"]
    csx, csm = _rope_tables(p["seq"])
    spt = p["seq"] // tm
    qkv_w = nqkv * tn
    est = (2 * tm * d * 4 + (tm + 16) * d * 2 + 2 * d * tn * 4 + d * tn * 2 + 4 * tm * tn * 2
           + 4 * 2 * tm * LANES * 4 + 4 * (tm + 16) * tn * 4)
    return pl.pallas_call(
        functools.partial(_inproj_kernel, p=p),
        grid=(nm, nn),
        in_specs=[
            pl.BlockSpec((tm, d), lambda m, n: (m, 0)),
            pl.BlockSpec((N_META, d), lambda m, n: (0, 0)),
            pl.BlockSpec((1, d), lambda m, n: (0, 0)),
            pl.BlockSpec((d, tn), lambda m, n: (0, n)),
            pl.BlockSpec((2, tm, LANES), lambda m, n: (0, m % spt, 0)),
            pl.BlockSpec((2, N_META, LANES), lambda m, n: (0, 0, 0)),
        ],
        out_specs=[
            pl.BlockSpec((tm, tn), lambda m, n: (m, jnp.minimum(n, nqkv - 1))),
            pl.BlockSpec((tm, tn), lambda m, n: (m, jnp.maximum(n - nqkv, 0))),
            pl.BlockSpec((1, N_META, tn), lambda m, n: (m, 0, jnp.minimum(n, nqkv - 1))),
            pl.BlockSpec((1, N_META, tn), lambda m, n: (m, 0, jnp.maximum(n - nqkv, 0))),
        ],
        out_shape=[
            jax.ShapeDtypeStruct((r, qkv_w), BF16),
            jax.ShapeDtypeStruct((r, p["rest_w"]), BF16),
            jax.ShapeDtypeStruct((nm, N_META, qkv_w), BF16),
            jax.ShapeDtypeStruct((nm, N_META, p["rest_w"]), BF16),
        ],
        scratch_shapes=[pltpu.VMEM((tm + N_META, d), BF16)],
        compiler_params=pltpu.CompilerParams(
            dimension_semantics=("arbitrary", "arbitrary"), vmem_limit_bytes=_vmem_limit(est)),
        name="inproj",
    )(x2, meta, g, w_in, csx, csm)


def _attn_kernel(sink_ref, q_ref, k0_ref, k1_ref, k2_ref, v0_ref, v1_ref, v2_ref, km_ref, vm_ref,
                 o_ref, kcat_ref, vcat_ref, *, p):
    seq, nkv = p["seq"], p["nkv"]
    nband = 3 * BLK
    nkeys = nband + N_META
    n = pl.program_id(1)
    for j, (kr, vr) in enumerate(((k0_ref, v0_ref), (k1_ref, v1_ref), (k2_ref, v2_ref))):
        kcat_ref[j * BLK:(j + 1) * BLK, :] = kr[...]
        vcat_ref[j * BLK:(j + 1) * BLK, :] = vr[...]
    kcat_ref[nband:nkeys, :] = km_ref[...]
    vcat_ref[nband:nkeys, :] = vm_ref[...]

    qi = lax.broadcasted_iota(I32, (BLK, nkeys), 0)
    sj = lax.broadcasted_iota(I32, (BLK, nkeys), 1)
    kx = (n - 1) * BLK + sj
    dq = n * BLK + qi - kx
    visible = (sj >= nband) | ((jnp.abs(dq) <= WINDOW) & (kx >= 0) & (kx < seq))
    neg = jnp.finfo(F32).min
    ones = jnp.ones((nkeys, HEAD_DIM), BF16)
    gi = lax.broadcasted_iota(I32, (Q_PER_KV, 1, 1), 0)
    for h in range(nkv):
        kh = kcat_ref[:, h * HEAD_DIM:(h + 1) * HEAD_DIM]
        vh = jnp.concatenate([vcat_ref[:, h * HEAD_DIM:(h + 1) * HEAD_DIM], ones], axis=1)
        heads = [h * Q_PER_KV + g for g in range(Q_PER_KV)]
        qs = jnp.concatenate([q_ref[:, hd * HEAD_DIM:(hd + 1) * HEAD_DIM] for hd in heads], axis=0)
        s = lax.dot_general(qs, kh, (((1,), (1,)), ((), ())), preferred_element_type=F32)
        s = jnp.where(visible[None], s.reshape(Q_PER_KV, BLK, nkeys), neg)
        snk = jnp.zeros((Q_PER_KV, 1, 1), F32)
        for g, hd in enumerate(heads):
            snk = jnp.where(gi == g, sink_ref[hd], snk)
        m = jnp.maximum(jnp.max(s, axis=2, keepdims=True), snk)
        e = jnp.exp(s - m).reshape(Q_PER_KV * BLK, nkeys)
        pv = jnp.dot(e.astype(BF16), vh, preferred_element_type=F32)
        pv = pv.reshape(Q_PER_KV, BLK, 2 * HEAD_DIM)
        o = pv[:, :, :HEAD_DIM] / (pv[:, :, HEAD_DIM:HEAD_DIM + 1] + jnp.exp(snk - m))
        for g, hd in enumerate(heads):
            o_ref[:, hd * HEAD_DIM:(hd + 1) * HEAD_DIM] = o[g].astype(BF16)


def _attention(qkv, qkvm, sink, p):
    attn, kvd, seq, b = p["attn"], p["kvd"], p["seq"], p["b"]
    nbx = seq // BLK
    kc = attn // kvd
    nkeys = 3 * BLK + N_META

    def kv_spec(off, col):
        return pl.BlockSpec((BLK, kvd), lambda bi, n: (bi * nbx + jnp.clip(n + off, 0, nbx - 1), col))

    return pl.pallas_call(
        functools.partial(_attn_kernel, p=p),
        grid=(b, nbx),
        in_specs=[
            pl.BlockSpec(memory_space=pltpu.SMEM),
            pl.BlockSpec((BLK, attn), lambda bi, n: (bi * nbx + n, 0)),
            kv_spec(-1, kc), kv_spec(0, kc), kv_spec(1, kc),
            kv_spec(-1, kc + 1), kv_spec(0, kc + 1), kv_spec(1, kc + 1),
            pl.BlockSpec((None, N_META, kvd), lambda bi, n: (0, 0, kc)),
            pl.BlockSpec((None, N_META, kvd), lambda bi, n: (0, 0, kc + 1)),
        ],
        out_specs=pl.BlockSpec((BLK, attn), lambda bi, n: (bi * nbx + n, 0)),
        out_shape=jax.ShapeDtypeStruct((p["r"], attn), BF16),
        scratch_shapes=[pltpu.VMEM((nkeys, kvd), BF16), pltpu.VMEM((nkeys, kvd), BF16)],
        compiler_params=pltpu.CompilerParams(dimension_semantics=("arbitrary", "arbitrary")),
        name="attn",
    )(sink, qkv, qkv, qkv, qkv, qkv, qkv, qkv, qkvm, qkvm)


def _lane_pack(cols, rows, dtype):
    lane = lax.broadcasted_iota(I32, (rows, LANES), 1)
    out = jnp.zeros((rows, LANES), dtype)
    for k, c in enumerate(cols):
        out = jnp.where(lane == k, c.astype(dtype), out)
    return out


def _mixer_kernel(attn_ref, rest_ref, prev_ref, next_ref, restm_ref, x_ref, cw_ref, wao_ref, wco_ref,
                  wout_ref, g_ref, rw_ref, rb_ref,
                  h1_ref, hn2_ref, ti_ref, gate_ref, rank_ref, cnt_ref, carry_ref, *, p):
    tc, conv, d, n_exp, seq = p["tc"], p["conv"], p["d"], p["n_exp"], p["seq"]
    i = pl.program_id(0)
    tiles_per_seq = seq // tc
    is_first = (i % tiles_per_seq) == 0
    is_last = (i % tiles_per_seq) == tiles_per_seq - 1
    o_ch, o_cb, o_cc, o_ga, o_gc = 0, conv, 2 * conv, 3 * conv, 3 * conv + d

    @pl.when(i == 0)
    def _():
        carry_ref[...] = jnp.zeros_like(carry_ref)

    def u_of(ref):
        return ref[:, o_cc:o_cc + conv].astype(F32) * ref[:, o_ch:o_ch + conv].astype(F32)

    u = u_of(rest_ref)
    last = N_META - 1
    u_prev = jnp.where(is_first, u_of(restm_ref)[last:last + 1], u_of(prev_ref)[last:last + 1])
    u_next = jnp.where(is_last, 0.0, u_of(next_ref)[0:1])
    row = lax.broadcasted_iota(I32, (tc, conv), 0)
    u_m1 = jnp.where(row == 0, u_prev, pltpu.roll(u, 1, 0))
    u_p1 = jnp.where(row == tc - 1, u_next, pltpu.roll(u, tc - 1, 0))
    cw = cw_ref[...]
    cv = u_m1 * cw[0:1] + u * cw[1:2] + u_p1 * cw[2:3]
    yc_in = (rest_ref[:, o_cb:o_cb + conv].astype(F32) * cv).astype(BF16)
    y_conv = jnp.dot(yc_in, wco_ref[...], preferred_element_type=F32)
    y_attn = jnp.dot(attn_ref[...], wao_ref[...], preferred_element_type=F32)
    g_a = rest_ref[:, o_ga:o_ga + d].astype(F32)
    g_c = rest_ref[:, o_gc:o_gc + d].astype(F32)
    merged = jax.nn.sigmoid(g_a) * y_attn + jax.nn.sigmoid(g_c) * y_conv
    h1 = x_ref[...] + jnp.dot(merged.astype(BF16), wout_ref[...], preferred_element_type=F32)
    h1_ref[...] = h1
    ms = jnp.mean(h1 * h1, axis=-1, keepdims=True)
    hn2 = (h1 * lax.rsqrt(ms + RMS_EPS)) * g_ref[...]
    hn2_ref[...] = hn2

    logits = jnp.dot(hn2.astype(BF16), rw_ref[...], preferred_element_type=F32) + rb_ref[...]
    lane = lax.broadcasted_iota(I32, (tc, n_exp), 1).astype(F32)
    sel = jnp.zeros((tc, n_exp), F32)
    tv, ti = [], []
    cur = logits
    for _ in range(TOP_K):
        m = jnp.max(cur, axis=1, keepdims=True)
        idx = jnp.min(jnp.where(cur == m, lane, float(n_exp)), axis=1, keepdims=True)
        hit = lane == idx
        tv.append(m)
        ti.append(idx)
        sel = jnp.where(hit, 1.0, sel)
        cur = jnp.where(hit, -jnp.inf, cur)
    ex = [jnp.exp(v - tv[0]) for v in tv]
    tot = ex[0] + ex[1] + ex[2] + ex[3]
    gates = [e / tot for e in ex]

    r_i = lax.broadcasted_iota(I32, (tc, tc), 0)
    c_i = lax.broadcasted_iota(I32, (tc, tc), 1)
    lower = jnp.where(r_i > c_i, 1.0, 0.0).astype(BF16)
    before = jnp.dot(lower, sel.astype(BF16), preferred_element_type=F32) + carry_ref[0:1, 0:n_exp]
    ranks = [jnp.sum(jnp.where(lane == t, before, 0.0), axis=1, keepdims=True) for t in ti]
    carry_ref[0:1, 0:n_exp] = carry_ref[0:1, 0:n_exp] + jnp.sum(sel, axis=0, keepdims=True)

    ti_ref[...] = _lane_pack(ti, tc, I32)
    gate_ref[...] = _lane_pack(gates, tc, F32)
    rank_ref[...] = _lane_pack(ranks, tc, I32)
    cnt_ref[...] = carry_ref[...]


def _mixer(attn_o, rest, restm, x2, conv_w, wao, wco, wout, g_ffn, router_w, router_b, p):
    tc, d, r, rw, n_exp = p["tc"], p["d"], p["r"], p["rest_w"], p["n_exp"]
    nt = r // tc
    sub = tc // N_META
    n16 = r // N_META
    const = lambda shape: pl.BlockSpec(shape, lambda i: (0,) * len(shape))
    est = (2 * (tc * p["attn"] * 2 + tc * rw * 2 + tc * d * 4 + 3 * N_META * rw * 2)
           + 2 * (p["attn"] * d + p["conv"] * d + d * d + d * n_exp) * 2
           + 2 * (2 * tc * d * 4 + 3 * tc * LANES * 4) + 10 * tc * d * 4)
    return pl.pallas_call(
        functools.partial(_mixer_kernel, p=p),
        grid=(nt,),
        in_specs=[
            pl.BlockSpec((tc, p["attn"]), lambda i: (i, 0)),
            pl.BlockSpec((tc, rw), lambda i: (i, 0)),
            pl.BlockSpec((N_META, rw), lambda i: (jnp.maximum(i * sub - 1, 0), 0)),
            pl.BlockSpec((N_META, rw), lambda i: (jnp.minimum((i + 1) * sub, n16 - 1), 0)),
            pl.BlockSpec((None, N_META, rw), lambda i: (0, 0, 0)),
            pl.BlockSpec((tc, d), lambda i: (i, 0)),
            const((CONV_K, p["conv"])),
            const((p["attn"], d)), const((p["conv"], d)), const((d, d)),
            const((1, d)), const((d, n_exp)), const((1, n_exp)),
        ],
        out_specs=[
            pl.BlockSpec((tc, d), lambda i: (i, 0)),
            pl.BlockSpec((tc, d), lambda i: (i, 0)),
            pl.BlockSpec((tc, LANES), lambda i: (i, 0)),
            pl.BlockSpec((tc, LANES), lambda i: (i, 0)),
            pl.BlockSpec((tc, LANES), lambda i: (i, 0)),
            pl.BlockSpec((8, LANES), lambda i: (0, 0)),
        ],
        out_shape=[
            jax.ShapeDtypeStruct((r, d), F32),
            jax.ShapeDtypeStruct((r, d), F32),
            jax.ShapeDtypeStruct((r, LANES), I32),
            jax.ShapeDtypeStruct((r, LANES), F32),
            jax.ShapeDtypeStruct((r, LANES), I32),
            jax.ShapeDtypeStruct((8, LANES), F32),
        ],
        scratch_shapes=[pltpu.VMEM((8, LANES), F32)],
        compiler_params=pltpu.CompilerParams(
            dimension_semantics=("arbitrary",), vmem_limit_bytes=_vmem_limit(est)),
        name="mixer",
    )(attn_o, rest, rest, rest, restm, x2, conv_w, wao, wco, wout, g_ffn, router_w, router_b)


def _dest_kernel(ti_ref, rank_ref, start_ref, dest_ref):
    rows = ti_ref.shape[0]
    lane = lax.broadcasted_iota(I32, (rows, LANES), 1).astype(F32)
    ti = ti_ref[...].astype(F32)
    rank = rank_ref[...].astype(F32)
    start = start_ref[0:1, :].astype(F32)
    out = jnp.zeros((rows, LANES), F32)
    for k in range(TOP_K):
        e_k = jnp.sum(jnp.where(lane == k, ti, 0.0), axis=1, keepdims=True)
        r_k = jnp.sum(jnp.where(lane == k, rank, 0.0), axis=1, keepdims=True)
        s_k = jnp.sum(jnp.where(lane == e_k, start, 0.0), axis=1, keepdims=True)
        out = jnp.where(lane == k, s_k + r_k, out)
    dest_ref[...] = out.astype(I32)


def _dest(ti, rank, row_start, p):
    r = p["r"]
    rows = min(1024, r)
    return pl.pallas_call(
        _dest_kernel,
        grid=(r // rows,),
        in_specs=[pl.BlockSpec((rows, LANES), lambda i: (i, 0)),
                  pl.BlockSpec((rows, LANES), lambda i: (i, 0)),
                  pl.BlockSpec((8, LANES), lambda i: (0, 0))],
        out_specs=pl.BlockSpec((rows, LANES), lambda i: (i, 0)),
        out_shape=jax.ShapeDtypeStruct((r, LANES), I32),
        name="dest",
    )(ti, rank, row_start)


def _moe_row_steps(tpc):
    return sorted(set(range(2, 2 * tpc + 1, 2)) | {h for h in MOE_HALF_STEPS if h < 2 * tpc})


def _routing_tables(counts, p):
    rt, tpc, n_exp, nch = p["rt"], p["tpc"], p["n_exp"], p["n_chunks_max"]
    cnt = counts[0, :n_exp].astype(I32)
    ntile = (cnt + rt - 1) // rt
    tile_start = jnp.cumsum(ntile) - ntile
    nchunk = (ntile + tpc - 1) // tpc
    chunk_end = jnp.cumsum(nchunk)
    c = jnp.arange(nch, dtype=I32)
    ce = jnp.minimum(jnp.sum((chunk_end[None, :] <= c[:, None]).astype(I32), axis=1), n_exp - 1)
    first = c - (chunk_end - nchunk)[ce]
    c_nt = jnp.clip(ntile[ce] - first * tpc, 0, tpc)
    c_nt = jnp.where(c < chunk_end[-1], c_nt, 0)
    c_ts = tile_start[ce] + first * tpc
    last_e = ce[jnp.maximum(chunk_end[-1] - 1, 0)]
    ce = jnp.where(c_nt > 0, ce, last_e)
    row_start = jnp.zeros((8, LANES), I32).at[0, :n_exp].set(tile_start * rt)
    used = jnp.stack([jnp.sum(ntile), chunk_end[-1]]).astype(I32)
    valid = jnp.clip(cnt[ce] - first * (tpc * rt), 0, c_nt * rt)
    halves = (valid + rt // 2 - 1) // (rt // 2)
    steps = jnp.asarray(_moe_row_steps(tpc), I32)
    first_ge = jnp.sum((steps[None, :] < halves[:, None]).astype(I32), axis=1)
    c_sel = steps[jnp.minimum(first_ge, steps.shape[0] - 1)]
    c_sel = jnp.where(c_nt > 0, c_sel, 0).astype(I32)
    return row_start, cnt, tile_start * rt, ce.astype(I32), c_ts.astype(I32), c_nt.astype(I32), used, c_sel


def _dispatch_kernel(cnt_ref, start_ref, dest_ref, hn_ref, xs_ref, zero_ref, sem, zsem, *, p):
    te, n_exp, rt, d = p["te"], p["n_exp"], p["rt"], p["d"]
    step = pl.program_id(0)

    @pl.when(step == 0)
    def _():
        zero_ref[...] = jnp.zeros_like(zero_ref)

        def per_expert(e, wait):
            def zero_rows(dst0, size):
                cp = pltpu.make_async_copy(zero_ref.at[pl.ds(0, size)], xs_ref.at[pl.ds(dst0, size)], zsem)
                cp.wait() if wait else cp.start()

            cnt = cnt_ref[e]
            cur = start_ref[e] + cnt
            pad = (rt - (cnt & (rt - 1))) & (rt - 1)
            head = (SUBLANES - (cur & (SUBLANES - 1))) & (SUBLANES - 1)
            for j in range(SUBLANES - 1):
                @pl.when(j < head)
                def _(j=j):
                    zero_rows(cur + j, 1)
            cur = cur + head
            rem = pad - head
            size = SUBLANES
            while size < rt:
                @pl.when((rem & size) != 0)
                def _(cur=cur, size=size):
                    zero_rows(pl.multiple_of(cur, SUBLANES), size)
                cur = cur + (rem & size)
                size *= 2

        lax.fori_loop(0, n_exp, lambda e, _: (per_expert(e, False), 0)[1], 0)
        lax.fori_loop(0, n_exp, lambda e, _: (per_expert(e, True), 0)[1], 0)

        half = rt // 2
        used = (start_ref[n_exp - 1] + cnt_ref[n_exp - 1] + rt - 1) // rt

        def tail_copy(t, j):
            r0 = pl.multiple_of(t * rt + j * half, half)
            return pltpu.make_async_copy(zero_ref, xs_ref.at[pl.ds(r0, half)], zsem)

        def tail_start(t, _):
            tail_copy(t, 0).start()
            tail_copy(t, 1).start()
            return 0

        def tail_wait(t, _):
            tail_copy(t, 0).wait()
            tail_copy(t, 1).wait()
            return 0

        lax.fori_loop(used, p["n_tiles_max"], tail_start, 0)
        lax.fori_loop(used, p["n_tiles_max"], tail_wait, 0)

    def row_copy(i, k):
        return pltpu.make_async_copy(hn_ref.at[pl.ds(i, 1)], xs_ref.at[pl.ds(dest_ref[i * TOP_K + k], 1)], sem)

    def issue(i, _):
        for k in range(TOP_K):
            row_copy(i, k).start(priority=k % 2)
        return 0

    lax.fori_loop(0, te, issue, 0)

    for k in range(TOP_K):
        pltpu.make_async_copy(hn_ref, xs_ref.at[pl.ds(0, te)], sem).wait()


def _dispatch(hn2, dest_flat, cnt, start, p):
    te, d, r = p["te"], p["d"], p["r"]
    return pl.pallas_call(
        functools.partial(_dispatch_kernel, p=p),
        grid=(r // te,),
        in_specs=[
            pl.BlockSpec(memory_space=pltpu.SMEM),
            pl.BlockSpec(memory_space=pltpu.SMEM),
            pl.BlockSpec((te * TOP_K,), lambda i: (i,), memory_space=pltpu.SMEM),
            pl.BlockSpec((te, d), lambda i: (i, 0)),
        ],
        out_specs=pl.BlockSpec(memory_space=pl.ANY),
        out_shape=jax.ShapeDtypeStruct((p["n_slots"], d), F32),
        scratch_shapes=[pltpu.VMEM((p["rt"] // 2, d), F32), pltpu.SemaphoreType.DMA, pltpu.SemaphoreType.DMA],
        compiler_params=pltpu.CompilerParams(dimension_semantics=("arbitrary",)),
        name="dispatch",
    )(cnt, start, dest_flat, hn2)


def _moe_kernel(ce_ref, cts_ref, cnt_ref, used_ref, csel_ref, xs_ref, wgu_ref, bgu_ref, wd_ref, bd_ref,
                ys_ref, xbuf_ref, yacc_ref, stage_ref, wg_ring, wu_ring, wd_ring, xsem, ysem, wsem, *, p):
    rt, nf, nch, tf, d_ff, nb = p["rt"], p["nf"], p["n_chunks_max"], p["tf"], p["d_ff"], MOE_W_SLOTS
    c = pl.program_id(0)
    nt = cnt_ref[c]
    c_next = jnp.minimum(c + 1, nch - 1)
    nt_next = jnp.where(c + 1 < nch, cnt_ref[c_next], 0)
    n_w_tiles = used_ref[1] * nf

    def w_copies(g):
        chunk = lax.div(g, nf)
        fg = lax.rem(g, nf)
        e = ce_ref[chunk]
        slot = lax.rem(g, nb)
        c0 = pl.multiple_of(fg * tf, tf)
        return (
            pltpu.make_async_copy(wgu_ref.at[e, :, pl.ds(c0, tf)], wg_ring.at[slot], wsem.at[slot]),
            pltpu.make_async_copy(wgu_ref.at[e, :, pl.ds(d_ff + c0, tf)], wu_ring.at[slot], wsem.at[slot]),
            pltpu.make_async_copy(wd_ref.at[e, pl.ds(c0, tf), :], wd_ring.at[slot], wsem.at[slot]),
        )

    def w_start(g):
        @pl.when(g < n_w_tiles)
        def _():
            for cp in w_copies(g):
                cp.start()

    acc = lax.rem(c, 2)

    def x_copy(chunk, t):
        src0 = pl.multiple_of((cts_ref[chunk] + t) * rt, rt)
        return pltpu.make_async_copy(xs_ref.at[pl.ds(src0, rt)], stage_ref.at[t], xsem.at[t])

    def y_copy(chunk, t):
        a = lax.rem(chunk, 2)
        r0 = pl.multiple_of(t * rt, rt)
        dst0 = pl.multiple_of((cts_ref[chunk] + t) * rt, rt)
        return pltpu.make_async_copy(yacc_ref.at[a, pl.ds(r0, rt)], ys_ref.at[pl.ds(dst0, rt)], ysem.at[a])

    def x_start_all(chunk, n_tiles):
        for t in range(p["tpc"]):
            @pl.when(t < n_tiles)
            def _(t=t):
                x_copy(chunk, t).start()

    @pl.when(nt > 0)
    def _():
        @pl.when(c == 0)
        def _():
            for g in range(nb):
                w_start(g)
            x_start_all(c, nt)

        def load(t, _):
            x_copy(c, t).wait()
            r0 = pl.multiple_of(t * rt, rt)
            xbuf_ref[pl.ds(r0, rt), :] = stage_ref[t].astype(BF16)
            return 0

        lax.fori_loop(0, nt, load, 0)

        @pl.when(nt_next > 0)
        def _():
            x_start_all(c_next, nt_next)

        bd = bd_ref[0]

        def init(t, _):
            r0 = pl.multiple_of(t * rt, rt)
            yacc_ref[acc, pl.ds(r0, rt), :] = jnp.broadcast_to(bd, (rt, p["d"]))
            return 0

        lax.fori_loop(0, nt, init, 0)

        def hidden_tile(f, _):
            g = c * nf + f
            slot = lax.rem(g, nb)
            for cp in w_copies(g):
                cp.wait()

            bg = bgu_ref[0, pl.ds(f, 1), :]
            bu = bgu_ref[0, pl.ds(nf + f, 1), :]

            def ffn(rows):
                xt = xbuf_ref[0:rows, :]
                gate = jnp.dot(xt, wg_ring[slot].astype(BF16), preferred_element_type=F32) + bg
                up = jnp.dot(xt, wu_ring[slot].astype(BF16), preferred_element_type=F32) + bu
                gate = jnp.minimum(gate, SWIGLU_LIMIT)
                up = jnp.clip(up, -SWIGLU_LIMIT, SWIGLU_LIMIT)
                act = (up + 1.0) * gate * jax.nn.sigmoid(SWIGLU_ALPHA * gate)
                yacc_ref[acc, 0:rows, :] += jnp.dot(act.astype(BF16), wd_ring[slot].astype(BF16),
                                                    preferred_element_type=F32)

            for hs in _moe_row_steps(p["tpc"]):
                @pl.when(csel_ref[c] == hs)
                def _(hs=hs):
                    ffn(hs * (rt // 2))

            w_start(g + nb)
            return 0

        lax.fori_loop(0, nf, hidden_tile, 0)

        def drain(chunk):
            def body(t, _):
                y_copy(chunk, t).wait()
                return 0

            lax.fori_loop(0, cnt_ref[chunk], body, 0)

        @pl.when(c > 0)
        def _():
            drain(jnp.maximum(c - 1, 0))

        def store(t, _):
            y_copy(c, t).start()
            return 0

        lax.fori_loop(0, nt, store, 0)

        @pl.when(nt_next == 0)
        def _():
            drain(c)

    @pl.when(c == nch - 1)
    def _():
        stage_ref[0] = jnp.zeros((rt, p["d"]), F32)

        def tail_copy(t):
            return pltpu.make_async_copy(stage_ref.at[0], ys_ref.at[pl.ds(pl.multiple_of(t * rt, rt), rt)],
                                         ysem.at[0])

        def tail_start(t, _):
            tail_copy(t).start()
            return 0

        def tail_wait(t, _):
            tail_copy(t).wait()
            return 0

        lax.fori_loop(used_ref[0], p["n_tiles_max"], tail_start, 0)
        lax.fori_loop(used_ref[0], p["n_tiles_max"], tail_wait, 0)


def _moe(xs, ce, cts, cnt, used, csel, w_gate_up, b_gate_up, w_down, b_down, p):
    d, tf, nf, rt, cap = p["d"], p["tf"], p["nf"], p["rt"], p["cap"]
    n_exp = p["n_exp"]
    nb = MOE_W_SLOTS
    bgu = b_gate_up.reshape(n_exp, 2 * nf, tf)
    bdn = b_down.reshape(n_exp, 1, d)

    est = (nb * 3 * d * tf * 4 + cap * d * 2 + 2 * cap * d * 4 + cap * d * 4 + 4 * rt * d * 4)
    grid_spec = pltpu.PrefetchScalarGridSpec(
        num_scalar_prefetch=5,
        grid=(p["n_chunks_max"],),
        in_specs=[
            pl.BlockSpec(memory_space=pl.ANY),
            pl.BlockSpec(memory_space=pl.ANY),
            pl.BlockSpec((1, 2 * nf, tf), lambda c, ce_r, cts_r, cnt_r, u_r, s_r: (ce_r[c], 0, 0)),
            pl.BlockSpec(memory_space=pl.ANY),
            pl.BlockSpec((1, 1, d), lambda c, ce_r, cts_r, cnt_r, u_r, s_r: (ce_r[c], 0, 0)),
        ],
        out_specs=pl.BlockSpec(memory_space=pl.ANY),
        scratch_shapes=[
            pltpu.VMEM((cap, d), BF16), pltpu.VMEM((2, cap, d), F32), pltpu.VMEM((p["tpc"], rt, d), F32),
            pltpu.VMEM((nb, d, tf), F32), pltpu.VMEM((nb, d, tf), F32), pltpu.VMEM((nb, tf, d), F32),
            pltpu.SemaphoreType.DMA((p["tpc"],)), pltpu.SemaphoreType.DMA((2,)), pltpu.SemaphoreType.DMA((nb,)),
        ],
    )
    return pl.pallas_call(
        functools.partial(_moe_kernel, p=p),
        grid_spec=grid_spec,
        out_shape=jax.ShapeDtypeStruct((p["n_slots"], d), F32),
        compiler_params=pltpu.CompilerParams(
            dimension_semantics=("arbitrary",), vmem_limit_bytes=_vmem_limit(est)),
        name="moe",
    )(ce, cts, cnt, used, csel, xs, w_gate_up, bgu, w_down, bdn)


def _combine_kernel(dest_ref, dest_next_ref, h1_ref, gate_ref, g_ref, ys_ref, o_ref, ybuf_ref, sem, *, p):
    tg = p["tg"]
    step = pl.program_id(0)
    slot = step % 2

    def gather(d_ref, buf_slot):
        def issue(j, _):
            i0 = pl.multiple_of(j * SUBLANES, SUBLANES)
            for s in range(SUBLANES):
                for k in range(TOP_K):
                    src = d_ref[i0 * TOP_K + (s * TOP_K + k)]
                    pltpu.make_async_copy(ys_ref.at[pl.ds(src, 1)], ybuf_ref.at[buf_slot, k, pl.ds(i0 + s, 1)],
                                          sem.at[buf_slot]).start(priority=k % 2)
            return 0

        lax.fori_loop(0, tg // SUBLANES, issue, 0)

    @pl.when(step == 0)
    def _():
        gather(dest_ref, 0)

    def reduce_tile(cur):
        @pl.when(step + 1 < pl.num_programs(0))
        def _():
            gather(dest_next_ref, 1 - cur)

        for k in range(TOP_K):
            pltpu.make_async_copy(ys_ref.at[pl.ds(0, tg)], ybuf_ref.at[cur, k], sem.at[cur]).wait()

        lane = lax.broadcasted_iota(I32, (tg, LANES), 1)
        gates = gate_ref[...]
        h = h1_ref[...]
        for k in range(TOP_K):
            g_k = jnp.sum(jnp.where(lane == k, gates, 0.0), axis=1, keepdims=True)
            h = h + ybuf_ref[cur, k] * g_k
        ms = jnp.mean(h * h, axis=-1, keepdims=True)
        o_ref[...] = (h * lax.rsqrt(ms + RMS_EPS)) * g_ref[...]

    for cur in range(2):
        @pl.when(slot == cur)
        def _(cur=cur):
            reduce_tile(cur)


def _combine(ys, dest_flat, h1, gates, g_final, p):
    tg, d, r = p["tg"], p["d"], p["r"]
    nt = r // tg
    return pl.pallas_call(
        functools.partial(_combine_kernel, p=p),
        grid=(nt,),
        in_specs=[
            pl.BlockSpec((tg * TOP_K,), lambda i: (i,), memory_space=pltpu.SMEM),
            pl.BlockSpec((tg * TOP_K,), lambda i: (jnp.minimum(i + 1, nt - 1),), memory_space=pltpu.SMEM),
            pl.BlockSpec((tg, d), lambda i: (i, 0)),
            pl.BlockSpec((tg, LANES), lambda i: (i, 0)),
            pl.BlockSpec((1, d), lambda i: (0, 0)),
            pl.BlockSpec(memory_space=pl.ANY),
        ],
        out_specs=pl.BlockSpec((tg, d), lambda i: (i, 0)),
        out_shape=jax.ShapeDtypeStruct((r, d), F32),
        scratch_shapes=[pltpu.VMEM((2, TOP_K, tg, d), F32), pltpu.SemaphoreType.DMA((2,))],
        compiler_params=pltpu.CompilerParams(dimension_semantics=("arbitrary",)),
        name="combine",
    )(dest_flat, dest_flat, h1, gates, g_final, ys)


def kernel(x, meta_tokens, norm_mix_g, w_in, conv_w, sink, w_attn_o, w_conv_o, w_out, norm_ffn_g,
           router_w, router_b, w_gate_up, b_gate_up, w_down, b_down, norm_final_g):
    b, seq, d = x.shape
    n_exp, d_ff = w_down.shape[1], w_down.shape[2]
    assert norm_mix_g.shape[0] == 1 and meta_tokens.shape[0] == N_META
    p = _plan(b, seq, d, n_exp, d_ff)
    x2 = x.reshape(p["r"], d)

    qkv, rest, qkvm, restm = _inproj(x2, meta_tokens.astype(F32), norm_mix_g[0].reshape(1, d), w_in[0], p)
    attn_o = _attention(qkv, qkvm, sink[0].astype(F32), p)
    h1, hn2, ti, gates, rank, counts = _mixer(
        attn_o, rest, restm, x2, conv_w[0], w_attn_o[0].astype(BF16), w_conv_o[0].astype(BF16),
        w_out[0].astype(BF16), norm_ffn_g[0].reshape(1, d), router_w[0].astype(BF16),
        router_b[0].reshape(1, n_exp), p)
    row_start, cnt, start, ce, cts, cnt_tiles, used, csel = _routing_tables(counts, p)
    dest = _dest(ti, rank, row_start, p)
    dest_flat = dest[:, :TOP_K].reshape(-1)
    xs = _dispatch(hn2, dest_flat, cnt, start, p)
    ys = _moe(xs, ce, cts, cnt_tiles, used, csel, w_gate_up[0], b_gate_up[0], w_down[0], b_down[0], p)
    out = _combine(ys, dest_flat, h1, gates, norm_final_g.reshape(1, d), p)
    return out.reshape(b, seq, d)
```

```python
import functools

import numpy as np
import jax
import jax.numpy as jnp
from jax import lax
from jax.experimental import pallas as pl
from jax.experimental.pallas import tpu as pltpu

N_META = 16
BLK = 128
WINDOW = 128
HEAD_DIM = 64
Q_PER_KV = 4
ROT_DIM = HEAD_DIM // 4
ROPE_THETA = 500000.0
CONV_K = 3
TOP_K = 4
SWIGLU_ALPHA = 1.702
SWIGLU_LIMIT = 7.0
RMS_EPS = 1e-5

LANES = 128
SUBLANES = 8
VMEM_LIMIT_CAP = 60000 * 1024
MOE_ROW_TILE = 256
MOE_CHUNK_TILES = 5
MOE_W_SLOTS = 3
MOE_STEP_DIV = 4
MOE_EXTRA_STEPS = (14, 17, 18)

F32 = jnp.float32
BF16 = jnp.bfloat16
I32 = jnp.int32


def _vmem_limit(nbytes):
    return int(min(VMEM_LIMIT_CAP, max(32 * 1024 * 1024, nbytes * 5 // 4 + (4 << 20))))


def _plan(b, seq, d, n_exp, d_ff):
    attn = (d // 128) * HEAD_DIM
    kvd = attn // Q_PER_KV
    conv = d // 2
    r = b * seq
    p = dict(b=b, seq=seq, d=d, n_exp=n_exp, d_ff=d_ff, attn=attn, kvd=kvd, conv=conv, r=r)
    p["nq"] = attn // HEAD_DIM
    p["nkv"] = kvd // HEAD_DIM
    p["in_dim"] = attn + 2 * kvd + 3 * conv + 2 * d
    p["tn"] = 2 * kvd
    p["tm"] = min(1024, seq)
    p["n_qkv_tiles"] = (attn + 2 * kvd) // p["tn"]
    p["n_col_tiles"] = p["in_dim"] // p["tn"]
    p["rest_w"] = 3 * conv + 2 * d
    p["tc"] = min(256, seq)
    p["te"] = min(256, seq)
    p["tg"] = min(128, seq)
    p["tf"] = min(256, d_ff)
    p["nf"] = d_ff // p["tf"]
    p["rt"] = MOE_ROW_TILE
    p["tpc"] = MOE_CHUNK_TILES
    p["cap"] = MOE_ROW_TILE * MOE_CHUNK_TILES
    p["n_tiles_max"] = (r * TOP_K) // p["rt"] + n_exp
    p["n_chunks_max"] = n_exp + (p["n_tiles_max"] - n_exp) // p["tpc"]
    p["n_slots"] = p["n_tiles_max"] * p["rt"]
    assert seq % p["tm"] == 0 and seq % p["tc"] == 0 and seq % BLK == 0
    assert attn % p["tn"] == 0 and p["in_dim"] % p["tn"] == 0 and kvd % LANES == 0
    assert (r * TOP_K) % p["rt"] == 0 and d_ff % p["tf"] == 0
    assert p["rt"] & (p["rt"] - 1) == 0, "the zero-fill decomposition needs a power-of-two row tile"
    return p


def _rope_tables(seq):
    half = ROT_DIM // 2
    pos = jnp.arange(N_META + seq, dtype=F32)
    inv_freq = ROPE_THETA ** (-jnp.arange(0, ROT_DIM, 2, dtype=F32) / ROT_DIM)
    ang = pos[:, None] * inv_freq[None, :]
    cos, sin = jnp.cos(ang), jnp.sin(ang)
    lane = np.arange(LANES) % HEAD_DIM
    idx = np.where(lane < ROT_DIM, lane % half, 0)
    rot = jnp.asarray(lane < ROT_DIM)
    sign = jnp.asarray(np.where(lane < half, -1.0, 1.0).astype(np.float32))
    cos_l = jnp.where(rot[None, :], cos[:, idx], 1.0)
    sin_l = jnp.where(rot[None, :], sin[:, idx] * sign[None, :], 0.0)
    tab = jnp.stack([cos_l, sin_l]).astype(F32)
    return tab[:, N_META:], tab[:, :N_META]


def _rope(t, cos, sin, n_cols):
    lane = lax.broadcasted_iota(I32, (t.shape[0], LANES), 1)
    first = (lane % HEAD_DIM) < (ROT_DIM // 2)
    outs = []
    for c in range(t.shape[1] // LANES):
        s = t[:, c * LANES:(c + 1) * LANES]
        if c * LANES < n_cols:
            partner = jnp.where(first, pltpu.roll(s, LANES - ROT_DIM // 2, 1), pltpu.roll(s, ROT_DIM // 2, 1))
            s = s * cos + partner * sin
        outs.append(s)
    return jnp.concatenate(outs, axis=1)


def _inproj_kernel(x_ref, meta_ref, g_ref, w_ref, csx_ref, csm_ref,
                   qkv_ref, rest_ref, qkvm_ref, restm_ref, hn_ref, *, p):
    tm, tn, kvd = p["tm"], p["tn"], p["kvd"]
    n_q_tiles = p["attn"] // tn
    n = pl.program_id(1)

    @pl.when(n == 0)
    def _():
        g = g_ref[...]

        def norm(v):
            ms = jnp.mean(v * v, axis=-1, keepdims=True)
            return ((v * lax.rsqrt(ms + RMS_EPS)) * g).astype(BF16)

        rows = min(128, tm)

        def body(i, _):
            r0 = pl.multiple_of(i * rows, rows)
            hn_ref[pl.ds(r0, rows), :] = norm(x_ref[pl.ds(r0, rows), :])
            return 0

        lax.fori_loop(0, tm // rows, body, 0)
        hn_ref[tm:tm + N_META, :] = norm(meta_ref[...])

    def project():
        res = jnp.dot(hn_ref[...], w_ref[...].astype(BF16), preferred_element_type=F32)
        return res[:tm], res[tm:]

    def store_qkv(n_cols, scale):
        res_x, res_m = project()
        rx = _rope(res_x, csx_ref[0], csx_ref[1], n_cols)
        rm = _rope(res_m, csm_ref[0], csm_ref[1], n_cols)
        if scale != 1.0:
            rx = rx * scale
        qkv_ref[...] = rx.astype(BF16)
        qkvm_ref[0] = rm.astype(BF16)

    @pl.when(n < n_q_tiles)
    def _():
        store_qkv(tn, HEAD_DIM ** -0.5)

    @pl.when(n == n_q_tiles)
    def _():
        store_qkv(kvd, 1.0)

    @pl.when(n >= p["n_qkv_tiles"])
    def _():
        res_x, res_m = project()
        rest_ref[...] = res_x.astype(BF16)
        restm_ref[0] = res_m.astype(BF16)


def _inproj(x2, meta, g, w_in, p):
    tm, tn, d, r = p["tm"], p["tn"], p["d"], p["r"]
    nm, nn, nqkv = r // tm, p["n_col_tiles"], p["n_qkv_tiles"]
    csx, csm = _rope_tables(p["seq"])
    spt = p["seq"] // tm
    qkv_w = nqkv * tn
    est = (2 * tm * d * 4 + (tm + 16) * d * 2 + 2 * d * tn * 4 + d * tn * 2 + 4 * tm * tn * 2
           + 4 * 2 * tm * LANES * 4 + 4 * (tm + 16) * tn * 4)
    return pl.pallas_call(
        functools.partial(_inproj_kernel, p=p),
        grid=(nm, nn),
        in_specs=[
            pl.BlockSpec((tm, d), lambda m, n: (m, 0)),
            pl.BlockSpec((N_META, d), lambda m, n: (0, 0)),
            pl.BlockSpec((1, d), lambda m, n: (0, 0)),
            pl.BlockSpec((d, tn), lambda m, n: (0, n)),
            pl.BlockSpec((2, tm, LANES), lambda m, n: (0, m % spt, 0)),
            pl.BlockSpec((2, N_META, LANES), lambda m, n: (0, 0, 0)),
        ],
        out_specs=[
            pl.BlockSpec((tm, tn), lambda m, n: (m, jnp.minimum(n, nqkv - 1))),
            pl.BlockSpec((tm, tn), lambda m, n: (m, jnp.maximum(n - nqkv, 0))),
            pl.BlockSpec((1, N_META, tn), lambda m, n: (m, 0, jnp.minimum(n, nqkv - 1))),
            pl.BlockSpec((1, N_META, tn), lambda m, n: (m, 0, jnp.maximum(n - nqkv, 0))),
        ],
        out_shape=[
            jax.ShapeDtypeStruct((r, qkv_w), BF16),
            jax.ShapeDtypeStruct((r, p["rest_w"]), BF16),
            jax.ShapeDtypeStruct((nm, N_META, qkv_w), BF16),
            jax.ShapeDtypeStruct((nm, N_META, p["rest_w"]), BF16),
        ],
        scratch_shapes=[pltpu.VMEM((tm + N_META, d), BF16)],
        compiler_params=pltpu.CompilerParams(
            dimension_semantics=("arbitrary", "arbitrary"), vmem_limit_bytes=_vmem_limit(est)),
        name="inproj",
    )(x2, meta, g, w_in, csx, csm)


def _attn_kernel(sink_ref, q_ref, k0_ref, k1_ref, k2_ref, v0_ref, v1_ref, v2_ref, km_ref, vm_ref,
                 o_ref, kcat_ref, vcat_ref, *, p):
    seq, nkv = p["seq"], p["nkv"]
    nband = 3 * BLK
    nkeys = nband + N_META
    n = pl.program_id(1)
    for j, (kr, vr) in enumerate(((k0_ref, v0_ref), (k1_ref, v1_ref), (k2_ref, v2_ref))):
        kcat_ref[j * BLK:(j + 1) * BLK, :] = kr[...]
        vcat_ref[j * BLK:(j + 1) * BLK, :] = vr[...]
    kcat_ref[nband:nkeys, :] = km_ref[...]
    vcat_ref[nband:nkeys, :] = vm_ref[...]

    qi = lax.broadcasted_iota(I32, (BLK, nkeys), 0)
    sj = lax.broadcasted_iota(I32, (BLK, nkeys), 1)
    kx = (n - 1) * BLK + sj
    dq = n * BLK + qi - kx
    visible = (sj >= nband) | ((jnp.abs(dq) <= WINDOW) & (kx >= 0) & (kx < seq))
    neg = jnp.finfo(F32).min
    ones = jnp.ones((nkeys, HEAD_DIM), BF16)
    gi = lax.broadcasted_iota(I32, (Q_PER_KV, 1, 1), 0)
    for h in range(nkv):
        kh = kcat_ref[:, h * HEAD_DIM:(h + 1) * HEAD_DIM]
        vh = jnp.concatenate([vcat_ref[:, h * HEAD_DIM:(h + 1) * HEAD_DIM], ones], axis=1)
        heads = [h * Q_PER_KV + g for g in range(Q_PER_KV)]
        qs = jnp.concatenate([q_ref[:, hd * HEAD_DIM:(hd + 1) * HEAD_DIM] for hd in heads], axis=0)
        s = lax.dot_general(qs, kh, (((1,), (1,)), ((), ())), preferred_element_type=F32)
        s = jnp.where(visible[None], s.reshape(Q_PER_KV, BLK, nkeys), neg)
        snk = jnp.zeros((Q_PER_KV, 1, 1), F32)
        for g, hd in enumerate(heads):
            snk = jnp.where(gi == g, sink_ref[hd], snk)
        m = jnp.maximum(jnp.max(s, axis=2, keepdims=True), snk)
        e = jnp.exp(s - m).reshape(Q_PER_KV * BLK, nkeys)
        pv = jnp.dot(e.astype(BF16), vh, preferred_element_type=F32)
        pv = pv.reshape(Q_PER_KV, BLK, 2 * HEAD_DIM)
        o = pv[:, :, :HEAD_DIM] / (pv[:, :, HEAD_DIM:HEAD_DIM + 1] + jnp.exp(snk - m))
        for g, hd in enumerate(heads):
            o_ref[:, hd * HEAD_DIM:(hd + 1) * HEAD_DIM] = o[g].astype(BF16)


def _attention(qkv, qkvm, sink, p):
    attn, kvd, seq, b = p["attn"], p["kvd"], p["seq"], p["b"]
    nbx = seq // BLK
    kc = attn // kvd
    nkeys = 3 * BLK + N_META

    def kv_spec(off, col):
        return pl.BlockSpec((BLK, kvd), lambda bi, n: (bi * nbx + jnp.clip(n + off, 0, nbx - 1), col))

    return pl.pallas_call(
        functools.partial(_attn_kernel, p=p),
        grid=(b, nbx),
        in_specs=[
            pl.BlockSpec(memory_space=pltpu.SMEM),
            pl.BlockSpec((BLK, attn), lambda bi, n: (bi * nbx + n, 0)),
            kv_spec(-1, kc), kv_spec(0, kc), kv_spec(1, kc),
            kv_spec(-1, kc + 1), kv_spec(0, kc + 1), kv_spec(1, kc + 1),
            pl.BlockSpec((None, N_META, kvd), lambda bi, n: (0, 0, kc)),
            pl.BlockSpec((None, N_META, kvd), lambda bi, n: (0, 0, kc + 1)),
        ],
        out_specs=pl.BlockSpec((BLK, attn), lambda bi, n: (bi * nbx + n, 0)),
        out_shape=jax.ShapeDtypeStruct((p["r"], attn), BF16),
        scratch_shapes=[pltpu.VMEM((nkeys, kvd), BF16), pltpu.VMEM((nkeys, kvd), BF16)],
        compiler_params=pltpu.CompilerParams(dimension_semantics=("arbitrary", "arbitrary")),
        name="attn",
    )(sink, qkv, qkv, qkv, qkv, qkv, qkv, qkv, qkvm, qkvm)


def _lane_pack(cols, rows, dtype):
    lane = lax.broadcasted_iota(I32, (rows, LANES), 1)
    out = jnp.zeros((rows, LANES), dtype)
    for k, c in enumerate(cols):
        out = jnp.where(lane == k, c.astype(dtype), out)
    return out


def _mixer_kernel(attn_ref, rest_ref, prev_ref, next_ref, restm_ref, x_ref, cw_ref, wao_ref, wco_ref,
                  wout_ref, g_ref, rw_ref, rb_ref,
                  h1_ref, hn2_ref, ti_ref, gate_ref, rank_ref, cnt_ref, carry_ref, *, p):
    tc, conv, d, n_exp, seq = p["tc"], p["conv"], p["d"], p["n_exp"], p["seq"]
    i = pl.program_id(0)
    tiles_per_seq = seq // tc
    is_first = (i % tiles_per_seq) == 0
    is_last = (i % tiles_per_seq) == tiles_per_seq - 1
    o_ch, o_cb, o_cc, o_ga, o_gc = 0, conv, 2 * conv, 3 * conv, 3 * conv + d

    @pl.when(i == 0)
    def _():
        carry_ref[...] = jnp.zeros_like(carry_ref)

    def u_of(ref):
        return ref[:, o_cc:o_cc + conv].astype(F32) * ref[:, o_ch:o_ch + conv].astype(F32)

    u = u_of(rest_ref)
    last = N_META - 1
    u_prev = jnp.where(is_first, u_of(restm_ref)[last:last + 1], u_of(prev_ref)[last:last + 1])
    u_next = jnp.where(is_last, 0.0, u_of(next_ref)[0:1])
    row = lax.broadcasted_iota(I32, (tc, conv), 0)
    u_m1 = jnp.where(row == 0, u_prev, pltpu.roll(u, 1, 0))
    u_p1 = jnp.where(row == tc - 1, u_next, pltpu.roll(u, tc - 1, 0))
    cw = cw_ref[...]
    cv = u_m1 * cw[0:1] + u * cw[1:2] + u_p1 * cw[2:3]
    yc_in = (rest_ref[:, o_cb:o_cb + conv].astype(F32) * cv).astype(BF16)
    y_conv = jnp.dot(yc_in, wco_ref[...], preferred_element_type=F32)
    y_attn = jnp.dot(attn_ref[...], wao_ref[...], preferred_element_type=F32)
    g_a = rest_ref[:, o_ga:o_ga + d].astype(F32)
    g_c = rest_ref[:, o_gc:o_gc + d].astype(F32)
    merged = jax.nn.sigmoid(g_a) * y_attn + jax.nn.sigmoid(g_c) * y_conv
    h1 = x_ref[...] + jnp.dot(merged.astype(BF16), wout_ref[...], preferred_element_type=F32)
    h1_ref[...] = h1
    ms = jnp.mean(h1 * h1, axis=-1, keepdims=True)
    hn2 = (h1 * lax.rsqrt(ms + RMS_EPS)) * g_ref[...]
    hn2_ref[...] = hn2

    logits = jnp.dot(hn2.astype(BF16), rw_ref[...], preferred_element_type=F32) + rb_ref[...]
    lane = lax.broadcasted_iota(I32, (tc, n_exp), 1).astype(F32)
    sel = jnp.zeros((tc, n_exp), F32)
    tv, ti = [], []
    cur = logits
    for _ in range(TOP_K):
        m = jnp.max(cur, axis=1, keepdims=True)
        idx = jnp.min(jnp.where(cur == m, lane, float(n_exp)), axis=1, keepdims=True)
        hit = lane == idx
        tv.append(m)
        ti.append(idx)
        sel = jnp.where(hit, 1.0, sel)
        cur = jnp.where(hit, -jnp.inf, cur)
    ex = [jnp.exp(v - tv[0]) for v in tv]
    tot = ex[0] + ex[1] + ex[2] + ex[3]
    gates = [e / tot for e in ex]

    r_i = lax.broadcasted_iota(I32, (tc, tc), 0)
    c_i = lax.broadcasted_iota(I32, (tc, tc), 1)
    lower = jnp.where(r_i > c_i, 1.0, 0.0).astype(BF16)
    before = jnp.dot(lower, sel.astype(BF16), preferred_element_type=F32) + carry_ref[0:1, 0:n_exp]
    ranks = [jnp.sum(jnp.where(lane == t, before, 0.0), axis=1, keepdims=True) for t in ti]
    carry_ref[0:1, 0:n_exp] = carry_ref[0:1, 0:n_exp] + jnp.sum(sel, axis=0, keepdims=True)

    ti_ref[...] = _lane_pack(ti, tc, I32)
    gate_ref[...] = _lane_pack(gates, tc, F32)
    rank_ref[...] = _lane_pack(ranks, tc, I32)
    cnt_ref[...] = carry_ref[...]


def _mixer(attn_o, rest, restm, x2, conv_w, wao, wco, wout, g_ffn, router_w, router_b, p):
    tc, d, r, rw, n_exp = p["tc"], p["d"], p["r"], p["rest_w"], p["n_exp"]
    nt = r // tc
    sub = tc // N_META
    n16 = r // N_META
    const = lambda shape: pl.BlockSpec(shape, lambda i: (0,) * len(shape))
    est = (2 * (tc * p["attn"] * 2 + tc * rw * 2 + tc * d * 4 + 3 * N_META * rw * 2)
           + 2 * (p["attn"] * d + p["conv"] * d + d * d + d * n_exp) * 2
           + 2 * (2 * tc * d * 4 + 3 * tc * LANES * 4) + 10 * tc * d * 4)
    return pl.pallas_call(
        functools.partial(_mixer_kernel, p=p),
        grid=(nt,),
        in_specs=[
            pl.BlockSpec((tc, p["attn"]), lambda i: (i, 0)),
            pl.BlockSpec((tc, rw), lambda i: (i, 0)),
            pl.BlockSpec((N_META, rw), lambda i: (jnp.maximum(i * sub - 1, 0), 0)),
            pl.BlockSpec((N_META, rw), lambda i: (jnp.minimum((i + 1) * sub, n16 - 1), 0)),
            pl.BlockSpec((None, N_META, rw), lambda i: (0, 0, 0)),
            pl.BlockSpec((tc, d), lambda i: (i, 0)),
            const((CONV_K, p["conv"])),
            const((p["attn"], d)), const((p["conv"], d)), const((d, d)),
            const((1, d)), const((d, n_exp)), const((1, n_exp)),
        ],
        out_specs=[
            pl.BlockSpec((tc, d), lambda i: (i, 0)),
            pl.BlockSpec((tc, d), lambda i: (i, 0)),
            pl.BlockSpec((tc, LANES), lambda i: (i, 0)),
            pl.BlockSpec((tc, LANES), lambda i: (i, 0)),
            pl.BlockSpec((tc, LANES), lambda i: (i, 0)),
            pl.BlockSpec((8, LANES), lambda i: (0, 0)),
        ],
        out_shape=[
            jax.ShapeDtypeStruct((r, d), F32),
            jax.ShapeDtypeStruct((r, d), F32),
            jax.ShapeDtypeStruct((r, LANES), I32),
            jax.ShapeDtypeStruct((r, LANES), F32),
            jax.ShapeDtypeStruct((r, LANES), I32),
            jax.ShapeDtypeStruct((8, LANES), F32),
        ],
        scratch_shapes=[pltpu.VMEM((8, LANES), F32)],
        compiler_params=pltpu.CompilerParams(
            dimension_semantics=("arbitrary",), vmem_limit_bytes=_vmem_limit(est)),
        name="mixer",
    )(attn_o, rest, rest, rest, restm, x2, conv_w, wao, wco, wout, g_ffn, router_w, router_b)


def _dest_kernel(ti_ref, rank_ref, start_ref, dest_ref):
    rows = ti_ref.shape[0]
    lane = lax.broadcasted_iota(I32, (rows, LANES), 1).astype(F32)
    ti = ti_ref[...].astype(F32)
    rank = rank_ref[...].astype(F32)
    start = start_ref[0:1, :].astype(F32)
    out = jnp.zeros((rows, LANES), F32)
    for k in range(TOP_K):
        e_k = jnp.sum(jnp.where(lane == k, ti, 0.0), axis=1, keepdims=True)
        r_k = jnp.sum(jnp.where(lane == k, rank, 0.0), axis=1, keepdims=True)
        s_k = jnp.sum(jnp.where(lane == e_k, start, 0.0), axis=1, keepdims=True)
        out = jnp.where(lane == k, s_k + r_k, out)
    dest_ref[...] = out.astype(I32)


def _dest(ti, rank, row_start, p):
    r = p["r"]
    rows = min(1024, r)
    return pl.pallas_call(
        _dest_kernel,
        grid=(r // rows,),
        in_specs=[pl.BlockSpec((rows, LANES), lambda i: (i, 0)),
                  pl.BlockSpec((rows, LANES), lambda i: (i, 0)),
                  pl.BlockSpec((8, LANES), lambda i: (0, 0))],
        out_specs=pl.BlockSpec((rows, LANES), lambda i: (i, 0)),
        out_shape=jax.ShapeDtypeStruct((r, LANES), I32),
        name="dest",
    )(ti, rank, row_start)


def _moe_row_steps(tpc):
    whole = range(MOE_STEP_DIV, MOE_STEP_DIV * tpc + 1, MOE_STEP_DIV)
    return sorted(set(whole) | {h for h in MOE_EXTRA_STEPS if h < MOE_STEP_DIV * tpc})


def _routing_tables(counts, p):
    rt, tpc, n_exp, nch = p["rt"], p["tpc"], p["n_exp"], p["n_chunks_max"]
    cnt = counts[0, :n_exp].astype(I32)
    ntile = (cnt + rt - 1) // rt
    tile_start = jnp.cumsum(ntile) - ntile
    nchunk = (ntile + tpc - 1) // tpc
    chunk_end = jnp.cumsum(nchunk)
    c = jnp.arange(nch, dtype=I32)
    ce = jnp.minimum(jnp.sum((chunk_end[None, :] <= c[:, None]).astype(I32), axis=1), n_exp - 1)
    first = c - (chunk_end - nchunk)[ce]
    c_nt = jnp.clip(ntile[ce] - first * tpc, 0, tpc)
    c_nt = jnp.where(c < chunk_end[-1], c_nt, 0)
    c_ts = tile_start[ce] + first * tpc
    last_e = ce[jnp.maximum(chunk_end[-1] - 1, 0)]
    ce = jnp.where(c_nt > 0, ce, last_e)
    row_start = jnp.zeros((8, LANES), I32).at[0, :n_exp].set(tile_start * rt)
    used = jnp.stack([jnp.sum(ntile), chunk_end[-1]]).astype(I32)
    unit = rt // MOE_STEP_DIV
    valid = jnp.clip(cnt[ce] - first * (tpc * rt), 0, c_nt * rt)
    halves = (valid + unit - 1) // unit
    steps = jnp.asarray(_moe_row_steps(tpc), I32)
    first_ge = jnp.sum((steps[None, :] < halves[:, None]).astype(I32), axis=1)
    c_sel = steps[jnp.minimum(first_ge, steps.shape[0] - 1)]
    c_sel = jnp.where(c_nt > 0, c_sel, 0).astype(I32)
    return row_start, cnt, tile_start * rt, ce.astype(I32), c_ts.astype(I32), c_nt.astype(I32), used, c_sel


def _dispatch_kernel(cnt_ref, start_ref, dest_ref, hn_ref, xs_ref, zero_ref, sem, zsem, *, p):
    te, n_exp, rt, d = p["te"], p["n_exp"], p["rt"], p["d"]
    step = pl.program_id(0)

    @pl.when(step == 0)
    def _():
        zero_ref[...] = jnp.zeros_like(zero_ref)

        def per_expert(e, wait):
            def zero_rows(dst0, size):
                cp = pltpu.make_async_copy(zero_ref.at[pl.ds(0, size)], xs_ref.at[pl.ds(dst0, size)], zsem)
                cp.wait() if wait else cp.start()

            cnt = cnt_ref[e]
            cur = start_ref[e] + cnt
            pad = (rt - (cnt & (rt - 1))) & (rt - 1)
            head = (SUBLANES - (cur & (SUBLANES - 1))) & (SUBLANES - 1)
            for j in range(SUBLANES - 1):
                @pl.when(j < head)
                def _(j=j):
                    zero_rows(cur + j, 1)
            cur = cur + head
            rem = pad - head
            size = SUBLANES
            while size < rt:
                @pl.when((rem & size) != 0)
                def _(cur=cur, size=size):
                    zero_rows(pl.multiple_of(cur, SUBLANES), size)
                cur = cur + (rem & size)
                size *= 2

        lax.fori_loop(0, n_exp, lambda e, _: (per_expert(e, False), 0)[1], 0)
        lax.fori_loop(0, n_exp, lambda e, _: (per_expert(e, True), 0)[1], 0)

        half = rt // 2
        used = (start_ref[n_exp - 1] + cnt_ref[n_exp - 1] + rt - 1) // rt

        def tail_copy(t, j):
            r0 = pl.multiple_of(t * rt + j * half, half)
            return pltpu.make_async_copy(zero_ref, xs_ref.at[pl.ds(r0, half)], zsem)

        def tail_start(t, _):
            tail_copy(t, 0).start()
            tail_copy(t, 1).start()
            return 0

        def tail_wait(t, _):
            tail_copy(t, 0).wait()
            tail_copy(t, 1).wait()
            return 0

        lax.fori_loop(used, p["n_tiles_max"], tail_start, 0)
        lax.fori_loop(used, p["n_tiles_max"], tail_wait, 0)

    def row_copy(i, k):
        return pltpu.make_async_copy(hn_ref.at[pl.ds(i, 1)], xs_ref.at[pl.ds(dest_ref[i * TOP_K + k], 1)], sem)

    def issue(i, _):
        for k in range(TOP_K):
            row_copy(i, k).start(priority=k % 2)
        return 0

    lax.fori_loop(0, te, issue, 0)

    for k in range(TOP_K):
        pltpu.make_async_copy(hn_ref, xs_ref.at[pl.ds(0, te)], sem).wait()


def _dispatch(hn2, dest_flat, cnt, start, p):
    te, d, r = p["te"], p["d"], p["r"]
    return pl.pallas_call(
        functools.partial(_dispatch_kernel, p=p),
        grid=(r // te,),
        in_specs=[
            pl.BlockSpec(memory_space=pltpu.SMEM),
            pl.BlockSpec(memory_space=pltpu.SMEM),
            pl.BlockSpec((te * TOP_K,), lambda i: (i,), memory_space=pltpu.SMEM),
            pl.BlockSpec((te, d), lambda i: (i, 0)),
        ],
        out_specs=pl.BlockSpec(memory_space=pl.ANY),
        out_shape=jax.ShapeDtypeStruct((p["n_slots"], d), F32),
        scratch_shapes=[pltpu.VMEM((p["rt"] // 2, d), F32), pltpu.SemaphoreType.DMA, pltpu.SemaphoreType.DMA],
        compiler_params=pltpu.CompilerParams(dimension_semantics=("arbitrary",)),
        name="dispatch",
    )(cnt, start, dest_flat, hn2)


def _moe_kernel(ce_ref, cts_ref, cnt_ref, used_ref, csel_ref, xs_ref, wgu_ref, bgu_ref, wd_ref, bd_ref,
                ys_ref, xbuf_ref, yacc_ref, stage_ref, wg_ring, wu_ring, wd_ring, xsem, ysem, wsem, *, p):
    rt, nf, nch, tf, d_ff, nb = p["rt"], p["nf"], p["n_chunks_max"], p["tf"], p["d_ff"], MOE_W_SLOTS
    c = pl.program_id(0)
    nt = cnt_ref[c]
    c_next = jnp.minimum(c + 1, nch - 1)
    nt_next = jnp.where(c + 1 < nch, cnt_ref[c_next], 0)
    n_w_tiles = used_ref[1] * nf

    def w_copies(g):
        chunk = lax.div(g, nf)
        fg = lax.rem(g, nf)
        e = ce_ref[chunk]
        slot = lax.rem(g, nb)
        c0 = pl.multiple_of(fg * tf, tf)
        return (
            pltpu.make_async_copy(wgu_ref.at[e, :, pl.ds(c0, tf)], wg_ring.at[slot], wsem.at[slot]),
            pltpu.make_async_copy(wgu_ref.at[e, :, pl.ds(d_ff + c0, tf)], wu_ring.at[slot], wsem.at[slot]),
            pltpu.make_async_copy(wd_ref.at[e, pl.ds(c0, tf), :], wd_ring.at[slot], wsem.at[slot]),
        )

    def w_start(g):
        @pl.when(g < n_w_tiles)
        def _():
            for cp in w_copies(g):
                cp.start()

    acc = lax.rem(c, 2)

    def x_copy(chunk, t):
        src0 = pl.multiple_of((cts_ref[chunk] + t) * rt, rt)
        return pltpu.make_async_copy(xs_ref.at[pl.ds(src0, rt)], stage_ref.at[t], xsem.at[t])

    def y_copy(chunk, t):
        a = lax.rem(chunk, 2)
        r0 = pl.multiple_of(t * rt, rt)
        dst0 = pl.multiple_of((cts_ref[chunk] + t) * rt, rt)
        return pltpu.make_async_copy(yacc_ref.at[a, pl.ds(r0, rt)], ys_ref.at[pl.ds(dst0, rt)], ysem.at[a])

    def x_start_all(chunk, n_tiles):
        for t in range(p["tpc"]):
            @pl.when(t < n_tiles)
            def _(t=t):
                x_copy(chunk, t).start()

    @pl.when(nt > 0)
    def _():
        @pl.when(c == 0)
        def _():
            for g in range(nb):
                w_start(g)
            x_start_all(c, nt)

        def load(t, _):
            x_copy(c, t).wait()
            r0 = pl.multiple_of(t * rt, rt)
            xbuf_ref[pl.ds(r0, rt), :] = stage_ref[t].astype(BF16)
            return 0

        lax.fori_loop(0, nt, load, 0)

        @pl.when(nt_next > 0)
        def _():
            x_start_all(c_next, nt_next)

        bd = bd_ref[0]

        def init(t, _):
            r0 = pl.multiple_of(t * rt, rt)
            yacc_ref[acc, pl.ds(r0, rt), :] = jnp.broadcast_to(bd, (rt, p["d"]))
            return 0

        lax.fori_loop(0, nt, init, 0)

        def hidden_tile(f, _):
            g = c * nf + f
            slot = lax.rem(g, nb)
            for cp in w_copies(g):
                cp.wait()

            bg = bgu_ref[0, pl.ds(f, 1), :]
            bu = bgu_ref[0, pl.ds(nf + f, 1), :]

            def ffn(rows):
                xt = xbuf_ref[0:rows, :]
                gate = jnp.dot(xt, wg_ring[slot].astype(BF16), preferred_element_type=F32) + bg
                up = jnp.dot(xt, wu_ring[slot].astype(BF16), preferred_element_type=F32) + bu
                gate = jnp.minimum(gate, SWIGLU_LIMIT)
                up = jnp.clip(up, -SWIGLU_LIMIT, SWIGLU_LIMIT)
                act = (up + 1.0) * gate * jax.nn.sigmoid(SWIGLU_ALPHA * gate)
                yacc_ref[acc, 0:rows, :] += jnp.dot(act.astype(BF16), wd_ring[slot].astype(BF16),
                                                    preferred_element_type=F32)

            for hs in _moe_row_steps(p["tpc"]):
                @pl.when(csel_ref[c] == hs)
                def _(hs=hs):
                    ffn(hs * (rt // MOE_STEP_DIV))

            w_start(g + nb)
            return 0

        lax.fori_loop(0, nf, hidden_tile, 0)

        def drain(chunk):
            def body(t, _):
                y_copy(chunk, t).wait()
                return 0

            lax.fori_loop(0, cnt_ref[chunk], body, 0)

        @pl.when(c > 0)
        def _():
            drain(jnp.maximum(c - 1, 0))

        def store(t, _):
            y_copy(c, t).start()
            return 0

        lax.fori_loop(0, nt, store, 0)

        @pl.when(nt_next == 0)
        def _():
            drain(c)

    @pl.when(c == nch - 1)
    def _():
        stage_ref[0] = jnp.zeros((rt, p["d"]), F32)

        def tail_copy(t):
            return pltpu.make_async_copy(stage_ref.at[0], ys_ref.at[pl.ds(pl.multiple_of(t * rt, rt), rt)],
                                         ysem.at[0])

        def tail_start(t, _):
            tail_copy(t).start()
            return 0

        def tail_wait(t, _):
            tail_copy(t).wait()
            return 0

        lax.fori_loop(used_ref[0], p["n_tiles_max"], tail_start, 0)
        lax.fori_loop(used_ref[0], p["n_tiles_max"], tail_wait, 0)


def _moe(xs, ce, cts, cnt, used, csel, w_gate_up, b_gate_up, w_down, b_down, p):
    d, tf, nf, rt, cap = p["d"], p["tf"], p["nf"], p["rt"], p["cap"]
    n_exp = p["n_exp"]
    nb = MOE_W_SLOTS
    bgu = b_gate_up.reshape(n_exp, 2 * nf, tf)
    bdn = b_down.reshape(n_exp, 1, d)

    est = (nb * 3 * d * tf * 4 + cap * d * 2 + 2 * cap * d * 4 + cap * d * 4 + 4 * rt * d * 4)
    grid_spec = pltpu.PrefetchScalarGridSpec(
        num_scalar_prefetch=5,
        grid=(p["n_chunks_max"],),
        in_specs=[
            pl.BlockSpec(memory_space=pl.ANY),
            pl.BlockSpec(memory_space=pl.ANY),
            pl.BlockSpec((1, 2 * nf, tf), lambda c, ce_r, cts_r, cnt_r, u_r, s_r: (ce_r[c], 0, 0)),
            pl.BlockSpec(memory_space=pl.ANY),
            pl.BlockSpec((1, 1, d), lambda c, ce_r, cts_r, cnt_r, u_r, s_r: (ce_r[c], 0, 0)),
        ],
        out_specs=pl.BlockSpec(memory_space=pl.ANY),
        scratch_shapes=[
            pltpu.VMEM((cap, d), BF16), pltpu.VMEM((2, cap, d), F32), pltpu.VMEM((p["tpc"], rt, d), F32),
            pltpu.VMEM((nb, d, tf), F32), pltpu.VMEM((nb, d, tf), F32), pltpu.VMEM((nb, tf, d), F32),
            pltpu.SemaphoreType.DMA((p["tpc"],)), pltpu.SemaphoreType.DMA((2,)), pltpu.SemaphoreType.DMA((nb,)),
        ],
    )
    return pl.pallas_call(
        functools.partial(_moe_kernel, p=p),
        grid_spec=grid_spec,
        out_shape=jax.ShapeDtypeStruct((p["n_slots"], d), F32),
        compiler_params=pltpu.CompilerParams(
            dimension_semantics=("arbitrary",), vmem_limit_bytes=_vmem_limit(est)),
        name="moe",
    )(ce, cts, cnt, used, csel, xs, w_gate_up, bgu, w_down, bdn)


def _combine_kernel(dest_ref, dest_next_ref, h1_ref, gate_ref, g_ref, ys_ref, o_ref, ybuf_ref, sem, *, p):
    tg = p["tg"]
    step = pl.program_id(0)
    slot = step % 2

    def gather(d_ref, buf_slot):
        def issue(j, _):
            i0 = pl.multiple_of(j * SUBLANES, SUBLANES)
            for s in range(SUBLANES):
                for k in range(TOP_K):
                    src = d_ref[i0 * TOP_K + (s * TOP_K + k)]
                    pltpu.make_async_copy(ys_ref.at[pl.ds(src, 1)], ybuf_ref.at[buf_slot, k, pl.ds(i0 + s, 1)],
                                          sem.at[buf_slot]).start(priority=k % 2)
            return 0

        lax.fori_loop(0, tg // SUBLANES, issue, 0)

    @pl.when(step == 0)
    def _():
        gather(dest_ref, 0)

    def reduce_tile(cur):
        @pl.when(step + 1 < pl.num_programs(0))
        def _():
            gather(dest_next_ref, 1 - cur)

        for k in range(TOP_K):
            pltpu.make_async_copy(ys_ref.at[pl.ds(0, tg)], ybuf_ref.at[cur, k], sem.at[cur]).wait()

        lane = lax.broadcasted_iota(I32, (tg, LANES), 1)
        gates = gate_ref[...]
        h = h1_ref[...]
        for k in range(TOP_K):
            g_k = jnp.sum(jnp.where(lane == k, gates, 0.0), axis=1, keepdims=True)
            h = h + ybuf_ref[cur, k] * g_k
        ms = jnp.mean(h * h, axis=-1, keepdims=True)
        o_ref[...] = (h * lax.rsqrt(ms + RMS_EPS)) * g_ref[...]

    for cur in range(2):
        @pl.when(slot == cur)
        def _(cur=cur):
            reduce_tile(cur)


def _combine(ys, dest_flat, h1, gates, g_final, p):
    tg, d, r = p["tg"], p["d"], p["r"]
    nt = r // tg
    return pl.pallas_call(
        functools.partial(_combine_kernel, p=p),
        grid=(nt,),
        in_specs=[
            pl.BlockSpec((tg * TOP_K,), lambda i: (i,), memory_space=pltpu.SMEM),
            pl.BlockSpec((tg * TOP_K,), lambda i: (jnp.minimum(i + 1, nt - 1),), memory_space=pltpu.SMEM),
            pl.BlockSpec((tg, d), lambda i: (i, 0)),
            pl.BlockSpec((tg, LANES), lambda i: (i, 0)),
            pl.BlockSpec((1, d), lambda i: (0, 0)),
            pl.BlockSpec(memory_space=pl.ANY),
        ],
        out_specs=pl.BlockSpec((tg, d), lambda i: (i, 0)),
        out_shape=jax.ShapeDtypeStruct((r, d), F32),
        scratch_shapes=[pltpu.VMEM((2, TOP_K, tg, d), F32), pltpu.SemaphoreType.DMA((2,))],
        compiler_params=pltpu.CompilerParams(dimension_semantics=("arbitrary",)),
        name="combine",
    )(dest_flat, dest_flat, h1, gates, g_final, ys)


def kernel(x, meta_tokens, norm_mix_g, w_in, conv_w, sink, w_attn_o, w_conv_o, w_out, norm_ffn_g,
           router_w, router_b, w_gate_up, b_gate_up, w_down, b_down, norm_final_g):
    b, seq, d = x.shape
    n_exp, d_ff = w_down.shape[1], w_down.shape[2]
    assert norm_mix_g.shape[0] == 1 and meta_tokens.shape[0] == N_META
    p = _plan(b, seq, d, n_exp, d_ff)
    x2 = x.reshape(p["r"], d)

    qkv, rest, qkvm, restm = _inproj(x2, meta_tokens.astype(F32), norm_mix_g[0].reshape(1, d), w_in[0], p)
    attn_o = _attention(qkv, qkvm, sink[0].astype(F32), p)
    h1, hn2, ti, gates, rank, counts = _mixer(
        attn_o, rest, restm, x2, conv_w[0], w_attn_o[0].astype(BF16), w_conv_o[0].astype(BF16),
        w_out[0].astype(BF16), norm_ffn_g[0].reshape(1, d), router_w[0].astype(BF16),
        router_b[0].reshape(1, n_exp), p)
    row_start, cnt, start, ce, cts, cnt_tiles, used, csel = _routing_tables(counts, p)
    dest = _dest(ti, rank, row_start, p)
    dest_flat = dest[:, :TOP_K].reshape(-1)
    xs = _dispatch(hn2, dest_flat, cnt, start, p)
    ys = _moe(xs, ce, cts, cnt_tiles, used, csel, w_gate_up[0], b_gate_up[0], w_down[0], b_down[0], p)
    out = _combine(ys, dest_flat, h1, gates, norm_final_g.reshape(1, d), p)
    return out.reshape(b, seq, d)
```

```python
import functools

import numpy as np
import jax
import jax.numpy as jnp
from jax import lax
from jax.experimental import pallas as pl
from jax.experimental.pallas import tpu as pltpu

N_META = 16
BLK = 128
WINDOW = 128
HEAD_DIM = 64
Q_PER_KV = 4
ROT_DIM = HEAD_DIM // 4
ROPE_THETA = 500000.0
CONV_K = 3
TOP_K = 4
SWIGLU_ALPHA = 1.702
SWIGLU_LIMIT = 7.0
RMS_EPS = 1e-5

LANES = 128
SUBLANES = 8
VMEM_LIMIT_CAP = 60000 * 1024
MOE_ROW_TILE = 256
MOE_CHUNK_TILES = 5
MOE_W_SLOTS = 3
INPROJ_W_SLOTS = 3
MOE_STEP_DIV = 4
MOE_EXTRA_STEPS = (14, 17, 18)

F32 = jnp.float32
BF16 = jnp.bfloat16
I32 = jnp.int32


def _vmem_limit(nbytes):
    return int(min(VMEM_LIMIT_CAP, max(32 * 1024 * 1024, nbytes * 5 // 4 + (4 << 20))))


def _plan(b, seq, d, n_exp, d_ff):
    attn = (d // 128) * HEAD_DIM
    kvd = attn // Q_PER_KV
    conv = d // 2
    r = b * seq
    p = dict(b=b, seq=seq, d=d, n_exp=n_exp, d_ff=d_ff, attn=attn, kvd=kvd, conv=conv, r=r)
    p["nq"] = attn // HEAD_DIM
    p["nkv"] = kvd // HEAD_DIM
    p["in_dim"] = attn + 2 * kvd + 3 * conv + 2 * d
    p["tn"] = 2 * kvd
    p["tm"] = min(1024, seq)
    p["n_qkv_tiles"] = (attn + 2 * kvd) // p["tn"]
    p["n_col_tiles"] = p["in_dim"] // p["tn"]
    p["rest_w"] = 3 * conv + 2 * d
    p["tc"] = min(256, seq)
    p["te"] = min(256, seq)
    p["tg"] = min(128, seq)
    p["tf"] = min(256, d_ff)
    p["nf"] = d_ff // p["tf"]
    p["rt"] = MOE_ROW_TILE
    p["tpc"] = MOE_CHUNK_TILES
    p["cap"] = MOE_ROW_TILE * MOE_CHUNK_TILES
    p["n_tiles_max"] = (r * TOP_K) // p["rt"] + n_exp
    p["n_chunks_max"] = n_exp + (p["n_tiles_max"] - n_exp) // p["tpc"]
    p["n_slots"] = p["n_tiles_max"] * p["rt"]
    assert seq % p["tm"] == 0 and seq % p["tc"] == 0 and seq % BLK == 0
    assert attn % p["tn"] == 0 and p["in_dim"] % p["tn"] == 0 and kvd % LANES == 0
    assert (r * TOP_K) % p["rt"] == 0 and d_ff % p["tf"] == 0
    assert p["rt"] & (p["rt"] - 1) == 0, "the zero-fill decomposition needs a power-of-two row tile"
    return p


def _rope_tables(seq):
    half = ROT_DIM // 2
    pos = jnp.arange(N_META + seq, dtype=F32)
    inv_freq = ROPE_THETA ** (-jnp.arange(0, ROT_DIM, 2, dtype=F32) / ROT_DIM)
    ang = pos[:, None] * inv_freq[None, :]
    cos, sin = jnp.cos(ang), jnp.sin(ang)
    lane = np.arange(LANES) % HEAD_DIM
    idx = np.where(lane < ROT_DIM, lane % half, 0)
    rot = jnp.asarray(lane < ROT_DIM)
    sign = jnp.asarray(np.where(lane < half, -1.0, 1.0).astype(np.float32))
    cos_l = jnp.where(rot[None, :], cos[:, idx], 1.0)
    sin_l = jnp.where(rot[None, :], sin[:, idx] * sign[None, :], 0.0)
    tab = jnp.stack([cos_l, sin_l]).astype(F32)
    return tab[:, N_META:], tab[:, :N_META]


def _rope(t, cos, sin, n_cols):
    lane = lax.broadcasted_iota(I32, (t.shape[0], LANES), 1)
    first = (lane % HEAD_DIM) < (ROT_DIM // 2)
    outs = []
    for c in range(t.shape[1] // LANES):
        s = t[:, c * LANES:(c + 1) * LANES]
        if c * LANES < n_cols:
            partner = jnp.where(first, pltpu.roll(s, LANES - ROT_DIM // 2, 1), pltpu.roll(s, ROT_DIM // 2, 1))
            s = s * cos + partner * sin
        outs.append(s)
    return jnp.concatenate(outs, axis=1)


def _inproj_kernel(x_ref, meta_ref, g_ref, w_ref, csx_ref, csm_ref,
                   qkv_ref, rest_ref, qkvm_ref, restm_ref,
                   hn_ref, w_ring, obuf_ref, mbuf_ref, wsem, osem, msem, *, p):
    tm, tn, kvd, nn, nqkv, nb = p["tm"], p["tn"], p["kvd"], p["n_col_tiles"], p["n_qkv_tiles"], INPROJ_W_SLOTS
    n_q_tiles = p["attn"] // tn
    m = pl.program_id(0)
    n_w_tiles = pl.num_programs(0) * nn
    row0 = pl.multiple_of(m * tm, tm)

    def w_copy(g):
        c0 = pl.multiple_of(lax.rem(g, nn) * tn, tn)
        slot = lax.rem(g, nb)
        return pltpu.make_async_copy(w_ref.at[:, pl.ds(c0, tn)], w_ring.at[slot], wsem.at[slot])

    def w_start(g):
        @pl.when(g < n_w_tiles)
        def _():
            w_copy(g).start()

    @pl.when(m == 0)
    def _():
        for g in range(nb):
            w_start(g)

    g_row = g_ref[...]

    def norm(v):
        ms = jnp.mean(v * v, axis=-1, keepdims=True)
        return ((v * lax.rsqrt(ms + RMS_EPS)) * g_row).astype(BF16)

    rows = min(128, tm)

    def norm_rows(i, _):
        r0 = pl.multiple_of(i * rows, rows)
        hn_ref[pl.ds(r0, rows), :] = norm(x_ref[pl.ds(r0, rows), :])
        return 0

    lax.fori_loop(0, tm // rows, norm_rows, 0)
    hn_ref[tm:tm + N_META, :] = norm(meta_ref[...])

    def out_copy(dst_ref, col0, slot):
        return pltpu.make_async_copy(obuf_ref.at[slot], dst_ref.at[pl.ds(row0, tm), pl.ds(col0, tn)], osem.at[slot])

    def meta_copy(dst_ref, col0, slot):
        return pltpu.make_async_copy(mbuf_ref.at[slot], dst_ref.at[0, :, pl.ds(col0, tn)], msem.at[slot])

    def wait_slot(slot):
        out_copy(rest_ref, 0, slot).wait()

        @pl.when(m == 0)
        def _():
            meta_copy(restm_ref, 0, slot).wait()

    def column_tile(n, _):
        g = m * nn + n
        wslot = lax.rem(g, nb)
        oslot = lax.rem(n, 2)
        w_copy(g).wait()

        @pl.when(n >= 2)
        def _():
            wait_slot(oslot)

        def project():
            res = jnp.dot(hn_ref[...], w_ring[wslot].astype(BF16), preferred_element_type=F32)
            return res[:tm], res[tm:]

        def emit(rx, rm, dst_ref, dstm_ref, col0):
            obuf_ref[oslot] = rx.astype(BF16)
            mbuf_ref[oslot] = rm.astype(BF16)
            out_copy(dst_ref, col0, oslot).start()

            @pl.when(m == 0)
            def _():
                meta_copy(dstm_ref, col0, oslot).start()

        def store_qkv(n_cols, scale):
            res_x, res_m = project()
            rx = _rope(res_x, csx_ref[0], csx_ref[1], n_cols)
            rm = _rope(res_m, csm_ref[0], csm_ref[1], n_cols)
            if scale != 1.0:
                rx = rx * scale
            emit(rx, rm, qkv_ref, qkvm_ref, pl.multiple_of(n * tn, tn))

        @pl.when(n < n_q_tiles)
        def _():
            store_qkv(tn, HEAD_DIM ** -0.5)

        @pl.when(n == n_q_tiles)
        def _():
            store_qkv(kvd, 1.0)

        @pl.when(n >= nqkv)
        def _():
            res_x, res_m = project()
            emit(res_x, res_m, rest_ref, restm_ref, pl.multiple_of((n - nqkv) * tn, tn))

        w_start(g + nb)
        return 0

    lax.fori_loop(0, nn, column_tile, 0)
    wait_slot(0)
    wait_slot(1)


def _inproj(x2, meta, g, w_in, p):
    tm, tn, d, r = p["tm"], p["tn"], p["d"], p["r"]
    nm, nn, nqkv = r // tm, p["n_col_tiles"], p["n_qkv_tiles"]
    assert nn >= 2
    csx, csm = _rope_tables(p["seq"])
    spt = p["seq"] // tm
    qkv_w = nqkv * tn
    nb = INPROJ_W_SLOTS
    est = (2 * tm * d * 4 + (tm + 16) * d * 2 + nb * d * tn * 4 + 2 * (tm + 16) * tn * 2
           + 4 * 2 * tm * LANES * 4 + 4 * (tm + 16) * tn * 4)
    return pl.pallas_call(
        functools.partial(_inproj_kernel, p=p),
        grid=(nm,),
        in_specs=[
            pl.BlockSpec((tm, d), lambda m: (m, 0)),
            pl.BlockSpec((N_META, d), lambda m: (0, 0)),
            pl.BlockSpec((1, d), lambda m: (0, 0)),
            pl.BlockSpec(memory_space=pl.ANY),
            pl.BlockSpec((2, tm, LANES), lambda m: (0, m % spt, 0)),
            pl.BlockSpec((2, N_META, LANES), lambda m: (0, 0, 0)),
        ],
        out_specs=[pl.BlockSpec(memory_space=pl.ANY)] * 4,
        out_shape=[
            jax.ShapeDtypeStruct((r, qkv_w), BF16),
            jax.ShapeDtypeStruct((r, p["rest_w"]), BF16),
            jax.ShapeDtypeStruct((1, N_META, qkv_w), BF16),
            jax.ShapeDtypeStruct((1, N_META, p["rest_w"]), BF16),
        ],
        scratch_shapes=[
            pltpu.VMEM((tm + N_META, d), BF16), pltpu.VMEM((nb, d, tn), F32),
            pltpu.VMEM((2, tm, tn), BF16), pltpu.VMEM((2, N_META, tn), BF16),
            pltpu.SemaphoreType.DMA((nb,)), pltpu.SemaphoreType.DMA((2,)), pltpu.SemaphoreType.DMA((2,)),
        ],
        compiler_params=pltpu.CompilerParams(
            dimension_semantics=("arbitrary",), vmem_limit_bytes=_vmem_limit(est)),
        name="inproj",
    )(x2, meta, g, w_in, csx, csm)


def _attn_kernel(sink_ref, q_ref, k0_ref, k1_ref, k2_ref, v0_ref, v1_ref, v2_ref, km_ref, vm_ref,
                 o_ref, kcat_ref, vcat_ref, *, p):
    seq, nkv = p["seq"], p["nkv"]
    nband = 3 * BLK
    nkeys = nband + N_META
    n = pl.program_id(1)
    for j, (kr, vr) in enumerate(((k0_ref, v0_ref), (k1_ref, v1_ref), (k2_ref, v2_ref))):
        kcat_ref[j * BLK:(j + 1) * BLK, :] = kr[...]
        vcat_ref[j * BLK:(j + 1) * BLK, :] = vr[...]
    kcat_ref[nband:nkeys, :] = km_ref[...]
    vcat_ref[nband:nkeys, :] = vm_ref[...]

    qi = lax.broadcasted_iota(I32, (BLK, nkeys), 0)
    sj = lax.broadcasted_iota(I32, (BLK, nkeys), 1)
    kx = (n - 1) * BLK + sj
    dq = n * BLK + qi - kx
    visible = (sj >= nband) | ((jnp.abs(dq) <= WINDOW) & (kx >= 0) & (kx < seq))
    neg = jnp.finfo(F32).min
    ones = jnp.ones((nkeys, HEAD_DIM), BF16)
    gi = lax.broadcasted_iota(I32, (Q_PER_KV, 1, 1), 0)
    for h in range(nkv):
        kh = kcat_ref[:, h * HEAD_DIM:(h + 1) * HEAD_DIM]
        vh = jnp.concatenate([vcat_ref[:, h * HEAD_DIM:(h + 1) * HEAD_DIM], ones], axis=1)
        heads = [h * Q_PER_KV + g for g in range(Q_PER_KV)]
        qs = jnp.concatenate([q_ref[:, hd * HEAD_DIM:(hd + 1) * HEAD_DIM] for hd in heads], axis=0)
        s = lax.dot_general(qs, kh, (((1,), (1,)), ((), ())), preferred_element_type=F32)
        s = jnp.where(visible[None], s.reshape(Q_PER_KV, BLK, nkeys), neg)
        snk = jnp.zeros((Q_PER_KV, 1, 1), F32)
        for g, hd in enumerate(heads):
            snk = jnp.where(gi == g, sink_ref[hd], snk)
        m = jnp.maximum(jnp.max(s, axis=2, keepdims=True), snk)
        e = jnp.exp(s - m).reshape(Q_PER_KV * BLK, nkeys)
        pv = jnp.dot(e.astype(BF16), vh, preferred_element_type=F32)
        pv = pv.reshape(Q_PER_KV, BLK, 2 * HEAD_DIM)
        o = pv[:, :, :HEAD_DIM] / (pv[:, :, HEAD_DIM:HEAD_DIM + 1] + jnp.exp(snk - m))
        for g, hd in enumerate(heads):
            o_ref[:, hd * HEAD_DIM:(hd + 1) * HEAD_DIM] = o[g].astype(BF16)


def _attention(qkv, qkvm, sink, p):
    attn, kvd, seq, b = p["attn"], p["kvd"], p["seq"], p["b"]
    nbx = seq // BLK
    kc = attn // kvd
    nkeys = 3 * BLK + N_META

    def kv_spec(off, col):
        return pl.BlockSpec((BLK, kvd), lambda bi, n: (bi * nbx + jnp.clip(n + off, 0, nbx - 1), col))

    return pl.pallas_call(
        functools.partial(_attn_kernel, p=p),
        grid=(b, nbx),
        in_specs=[
            pl.BlockSpec(memory_space=pltpu.SMEM),
            pl.BlockSpec((BLK, attn), lambda bi, n: (bi * nbx + n, 0)),
            kv_spec(-1, kc), kv_spec(0, kc), kv_spec(1, kc),
            kv_spec(-1, kc + 1), kv_spec(0, kc + 1), kv_spec(1, kc + 1),
            pl.BlockSpec((None, N_META, kvd), lambda bi, n: (0, 0, kc)),
            pl.BlockSpec((None, N_META, kvd), lambda bi, n: (0, 0, kc + 1)),
        ],
        out_specs=pl.BlockSpec((BLK, attn), lambda bi, n: (bi * nbx + n, 0)),
        out_shape=jax.ShapeDtypeStruct((p["r"], attn), BF16),
        scratch_shapes=[pltpu.VMEM((nkeys, kvd), BF16), pltpu.VMEM((nkeys, kvd), BF16)],
        compiler_params=pltpu.CompilerParams(dimension_semantics=("arbitrary", "arbitrary")),
        name="attn",
    )(sink, qkv, qkv, qkv, qkv, qkv, qkv, qkv, qkvm, qkvm)


def _lane_pack(cols, rows, dtype):
    lane = lax.broadcasted_iota(I32, (rows, LANES), 1)
    out = jnp.zeros((rows, LANES), dtype)
    for k, c in enumerate(cols):
        out = jnp.where(lane == k, c.astype(dtype), out)
    return out


def _mixer_kernel(attn_ref, rest_ref, prev_ref, next_ref, restm_ref, x_ref, cw_ref, wao_ref, wco_ref,
                  wout_ref, g_ref, rw_ref, rb_ref,
                  h1_ref, hn2_ref, ti_ref, gate_ref, rank_ref, cnt_ref, carry_ref, *, p):
    tc, conv, d, n_exp, seq = p["tc"], p["conv"], p["d"], p["n_exp"], p["seq"]
    i = pl.program_id(0)
    tiles_per_seq = seq // tc
    is_first = (i % tiles_per_seq) == 0
    is_last = (i % tiles_per_seq) == tiles_per_seq - 1
    o_ch, o_cb, o_cc, o_ga, o_gc = 0, conv, 2 * conv, 3 * conv, 3 * conv + d

    @pl.when(i == 0)
    def _():
        carry_ref[...] = jnp.zeros_like(carry_ref)

    def u_of(ref):
        return ref[:, o_cc:o_cc + conv].astype(F32) * ref[:, o_ch:o_ch + conv].astype(F32)

    u = u_of(rest_ref)
    last = N_META - 1
    u_prev = jnp.where(is_first, u_of(restm_ref)[last:last + 1], u_of(prev_ref)[last:last + 1])
    u_next = jnp.where(is_last, 0.0, u_of(next_ref)[0:1])
    row = lax.broadcasted_iota(I32, (tc, conv), 0)
    u_m1 = jnp.where(row == 0, u_prev, pltpu.roll(u, 1, 0))
    u_p1 = jnp.where(row == tc - 1, u_next, pltpu.roll(u, tc - 1, 0))
    cw = cw_ref[...]
    cv = u_m1 * cw[0:1] + u * cw[1:2] + u_p1 * cw[2:3]
    yc_in = (rest_ref[:, o_cb:o_cb + conv].astype(F32) * cv).astype(BF16)
    y_conv = jnp.dot(yc_in, wco_ref[...], preferred_element_type=F32)
    y_attn = jnp.dot(attn_ref[...], wao_ref[...], preferred_element_type=F32)
    g_a = rest_ref[:, o_ga:o_ga + d].astype(F32)
    g_c = rest_ref[:, o_gc:o_gc + d].astype(F32)
    merged = jax.nn.sigmoid(g_a) * y_attn + jax.nn.sigmoid(g_c) * y_conv
    h1 = x_ref[...] + jnp.dot(merged.astype(BF16), wout_ref[...], preferred_element_type=F32)
    h1_ref[...] = h1
    ms = jnp.mean(h1 * h1, axis=-1, keepdims=True)
    hn2 = (h1 * lax.rsqrt(ms + RMS_EPS)) * g_ref[...]
    hn2_ref[...] = hn2

    logits = jnp.dot(hn2.astype(BF16), rw_ref[...], preferred_element_type=F32) + rb_ref[...]
    lane = lax.broadcasted_iota(I32, (tc, n_exp), 1).astype(F32)
    sel = jnp.zeros((tc, n_exp), F32)
    tv, ti = [], []
    cur = logits
    for _ in range(TOP_K):
        m = jnp.max(cur, axis=1, keepdims=True)
        idx = jnp.min(jnp.where(cur == m, lane, float(n_exp)), axis=1, keepdims=True)
        hit = lane == idx
        tv.append(m)
        ti.append(idx)
        sel = jnp.where(hit, 1.0, sel)
        cur = jnp.where(hit, -jnp.inf, cur)
    ex = [jnp.exp(v - tv[0]) for v in tv]
    tot = ex[0] + ex[1] + ex[2] + ex[3]
    gates = [e / tot for e in ex]

    r_i = lax.broadcasted_iota(I32, (tc, tc), 0)
    c_i = lax.broadcasted_iota(I32, (tc, tc), 1)
    lower = jnp.where(r_i > c_i, 1.0, 0.0).astype(BF16)
    before = jnp.dot(lower, sel.astype(BF16), preferred_element_type=F32) + carry_ref[0:1, 0:n_exp]
    ranks = [jnp.sum(jnp.where(lane == t, before, 0.0), axis=1, keepdims=True) for t in ti]
    carry_ref[0:1, 0:n_exp] = carry_ref[0:1, 0:n_exp] + jnp.sum(sel, axis=0, keepdims=True)

    ti_ref[...] = _lane_pack(ti, tc, I32)
    gate_ref[...] = _lane_pack(gates, tc, F32)
    rank_ref[...] = _lane_pack(ranks, tc, I32)
    cnt_ref[...] = carry_ref[...]


def _mixer(attn_o, rest, restm, x2, conv_w, wao, wco, wout, g_ffn, router_w, router_b, p):
    tc, d, r, rw, n_exp = p["tc"], p["d"], p["r"], p["rest_w"], p["n_exp"]
    nt = r // tc
    sub = tc // N_META
    n16 = r // N_META
    const = lambda shape: pl.BlockSpec(shape, lambda i: (0,) * len(shape))
    est = (2 * (tc * p["attn"] * 2 + tc * rw * 2 + tc * d * 4 + 3 * N_META * rw * 2)
           + 2 * (p["attn"] * d + p["conv"] * d + d * d + d * n_exp) * 2
           + 2 * (2 * tc * d * 4 + 3 * tc * LANES * 4) + 10 * tc * d * 4)
    return pl.pallas_call(
        functools.partial(_mixer_kernel, p=p),
        grid=(nt,),
        in_specs=[
            pl.BlockSpec((tc, p["attn"]), lambda i: (i, 0)),
            pl.BlockSpec((tc, rw), lambda i: (i, 0)),
            pl.BlockSpec((N_META, rw), lambda i: (jnp.maximum(i * sub - 1, 0), 0)),
            pl.BlockSpec((N_META, rw), lambda i: (jnp.minimum((i + 1) * sub, n16 - 1), 0)),
            pl.BlockSpec((None, N_META, rw), lambda i: (0, 0, 0)),
            pl.BlockSpec((tc, d), lambda i: (i, 0)),
            const((CONV_K, p["conv"])),
            const((p["attn"], d)), const((p["conv"], d)), const((d, d)),
            const((1, d)), const((d, n_exp)), const((1, n_exp)),
        ],
        out_specs=[
            pl.BlockSpec((tc, d), lambda i: (i, 0)),
            pl.BlockSpec((tc, d), lambda i: (i, 0)),
            pl.BlockSpec((tc, LANES), lambda i: (i, 0)),
            pl.BlockSpec((tc, LANES), lambda i: (i, 0)),
            pl.BlockSpec((tc, LANES), lambda i: (i, 0)),
            pl.BlockSpec((8, LANES), lambda i: (0, 0)),
        ],
        out_shape=[
            jax.ShapeDtypeStruct((r, d), F32),
            jax.ShapeDtypeStruct((r, d), F32),
            jax.ShapeDtypeStruct((r, LANES), I32),
            jax.ShapeDtypeStruct((r, LANES), F32),
            jax.ShapeDtypeStruct((r, LANES), I32),
            jax.ShapeDtypeStruct((8, LANES), F32),
        ],
        scratch_shapes=[pltpu.VMEM((8, LANES), F32)],
        compiler_params=pltpu.CompilerParams(
            dimension_semantics=("arbitrary",), vmem_limit_bytes=_vmem_limit(est)),
        name="mixer",
    )(attn_o, rest, rest, rest, restm, x2, conv_w, wao, wco, wout, g_ffn, router_w, router_b)


def _dest_kernel(ti_ref, rank_ref, start_ref, dest_ref):
    rows = ti_ref.shape[0]
    lane = lax.broadcasted_iota(I32, (rows, LANES), 1).astype(F32)
    ti = ti_ref[...].astype(F32)
    rank = rank_ref[...].astype(F32)
    start = start_ref[0:1, :].astype(F32)
    out = jnp.zeros((rows, LANES), F32)
    for k in range(TOP_K):
        e_k = jnp.sum(jnp.where(lane == k, ti, 0.0), axis=1, keepdims=True)
        r_k = jnp.sum(jnp.where(lane == k, rank, 0.0), axis=1, keepdims=True)
        s_k = jnp.sum(jnp.where(lane == e_k, start, 0.0), axis=1, keepdims=True)
        out = jnp.where(lane == k, s_k + r_k, out)
    dest_ref[...] = out.astype(I32)


def _dest(ti, rank, row_start, p):
    r = p["r"]
    rows = min(1024, r)
    return pl.pallas_call(
        _dest_kernel,
        grid=(r // rows,),
        in_specs=[pl.BlockSpec((rows, LANES), lambda i: (i, 0)),
                  pl.BlockSpec((rows, LANES), lambda i: (i, 0)),
                  pl.BlockSpec((8, LANES), lambda i: (0, 0))],
        out_specs=pl.BlockSpec((rows, LANES), lambda i: (i, 0)),
        out_shape=jax.ShapeDtypeStruct((r, LANES), I32),
        name="dest",
    )(ti, rank, row_start)


def _moe_row_steps(tpc):
    whole = range(MOE_STEP_DIV, MOE_STEP_DIV * tpc + 1, MOE_STEP_DIV)
    return sorted(set(whole) | {h for h in MOE_EXTRA_STEPS if h < MOE_STEP_DIV * tpc})


def _routing_tables(counts, p):
    rt, tpc, n_exp, nch = p["rt"], p["tpc"], p["n_exp"], p["n_chunks_max"]
    cnt = counts[0, :n_exp].astype(I32)
    ntile = (cnt + rt - 1) // rt
    tile_start = jnp.cumsum(ntile) - ntile
    nchunk = (ntile + tpc - 1) // tpc
    chunk_end = jnp.cumsum(nchunk)
    c = jnp.arange(nch, dtype=I32)
    ce = jnp.minimum(jnp.sum((chunk_end[None, :] <= c[:, None]).astype(I32), axis=1), n_exp - 1)
    first = c - (chunk_end - nchunk)[ce]
    c_nt = jnp.clip(ntile[ce] - first * tpc, 0, tpc)
    c_nt = jnp.where(c < chunk_end[-1], c_nt, 0)
    c_ts = tile_start[ce] + first * tpc
    last_e = ce[jnp.maximum(chunk_end[-1] - 1, 0)]
    ce = jnp.where(c_nt > 0, ce, last_e)
    row_start = jnp.zeros((8, LANES), I32).at[0, :n_exp].set(tile_start * rt)
    used = jnp.stack([jnp.sum(ntile), chunk_end[-1]]).astype(I32)
    unit = rt // MOE_STEP_DIV
    valid = jnp.clip(cnt[ce] - first * (tpc * rt), 0, c_nt * rt)
    halves = (valid + unit - 1) // unit
    steps = jnp.asarray(_moe_row_steps(tpc), I32)
    first_ge = jnp.sum((steps[None, :] < halves[:, None]).astype(I32), axis=1)
    c_sel = steps[jnp.minimum(first_ge, steps.shape[0] - 1)]
    c_sel = jnp.where(c_nt > 0, c_sel, 0).astype(I32)
    return row_start, cnt, tile_start * rt, ce.astype(I32), c_ts.astype(I32), c_nt.astype(I32), used, c_sel


def _dispatch_kernel(cnt_ref, start_ref, dest_ref, hn_ref, xs_ref, zero_ref, sem, zsem, *, p):
    te, n_exp, rt, d = p["te"], p["n_exp"], p["rt"], p["d"]
    step = pl.program_id(0)

    @pl.when(step == 0)
    def _():
        zero_ref[...] = jnp.zeros_like(zero_ref)

        def per_expert(e, wait):
            def zero_rows(dst0, size):
                cp = pltpu.make_async_copy(zero_ref.at[pl.ds(0, size)], xs_ref.at[pl.ds(dst0, size)], zsem)
                cp.wait() if wait else cp.start()

            cnt = cnt_ref[e]
            cur = start_ref[e] + cnt
            pad = (rt - (cnt & (rt - 1))) & (rt - 1)
            head = (SUBLANES - (cur & (SUBLANES - 1))) & (SUBLANES - 1)
            for j in range(SUBLANES - 1):
                @pl.when(j < head)
                def _(j=j):
                    zero_rows(cur + j, 1)
            cur = cur + head
            rem = pad - head
            size = SUBLANES
            while size < rt:
                @pl.when((rem & size) != 0)
                def _(cur=cur, size=size):
                    zero_rows(pl.multiple_of(cur, SUBLANES), size)
                cur = cur + (rem & size)
                size *= 2

        lax.fori_loop(0, n_exp, lambda e, _: (per_expert(e, False), 0)[1], 0)
        lax.fori_loop(0, n_exp, lambda e, _: (per_expert(e, True), 0)[1], 0)

        half = rt // 2
        used = (start_ref[n_exp - 1] + cnt_ref[n_exp - 1] + rt - 1) // rt

        def tail_copy(t, j):
            r0 = pl.multiple_of(t * rt + j * half, half)
            return pltpu.make_async_copy(zero_ref, xs_ref.at[pl.ds(r0, half)], zsem)

        def tail_start(t, _):
            tail_copy(t, 0).start()
            tail_copy(t, 1).start()
            return 0

        def tail_wait(t, _):
            tail_copy(t, 0).wait()
            tail_copy(t, 1).wait()
            return 0

        lax.fori_loop(used, p["n_tiles_max"], tail_start, 0)
        lax.fori_loop(used, p["n_tiles_max"], tail_wait, 0)

    def row_copy(i, k):
        return pltpu.make_async_copy(hn_ref.at[pl.ds(i, 1)], xs_ref.at[pl.ds(dest_ref[i * TOP_K + k], 1)], sem)

    def issue(i, _):
        for k in range(TOP_K):
            row_copy(i, k).start(priority=k % 2)
        return 0

    lax.fori_loop(0, te, issue, 0)

    for k in range(TOP_K):
        pltpu.make_async_copy(hn_ref, xs_ref.at[pl.ds(0, te)], sem).wait()


def _dispatch(hn2, dest_flat, cnt, start, p):
    te, d, r = p["te"], p["d"], p["r"]
    return pl.pallas_call(
        functools.partial(_dispatch_kernel, p=p),
        grid=(r // te,),
        in_specs=[
            pl.BlockSpec(memory_space=pltpu.SMEM),
            pl.BlockSpec(memory_space=pltpu.SMEM),
            pl.BlockSpec((te * TOP_K,), lambda i: (i,), memory_space=pltpu.SMEM),
            pl.BlockSpec((te, d), lambda i: (i, 0)),
        ],
        out_specs=pl.BlockSpec(memory_space=pl.ANY),
        out_shape=jax.ShapeDtypeStruct((p["n_slots"], d), F32),
        scratch_shapes=[pltpu.VMEM((p["rt"] // 2, d), F32), pltpu.SemaphoreType.DMA, pltpu.SemaphoreType.DMA],
        compiler_params=pltpu.CompilerParams(dimension_semantics=("arbitrary",)),
        name="dispatch",
    )(cnt, start, dest_flat, hn2)


def _moe_kernel(ce_ref, cts_ref, cnt_ref, used_ref, csel_ref, xs_ref, wgu_ref, bgu_ref, wd_ref, bd_ref,
                ys_ref, xbuf_ref, yacc_ref, stage_ref, wg_ring, wu_ring, wd_ring, xsem, ysem, wsem, *, p):
    rt, nf, nch, tf, d_ff, nb = p["rt"], p["nf"], p["n_chunks_max"], p["tf"], p["d_ff"], MOE_W_SLOTS
    c = pl.program_id(0)
    nt = cnt_ref[c]
    c_next = jnp.minimum(c + 1, nch - 1)
    nt_next = jnp.where(c + 1 < nch, cnt_ref[c_next], 0)
    n_w_tiles = used_ref[1] * nf

    def w_copies(g):
        chunk = lax.div(g, nf)
        fg = lax.rem(g, nf)
        e = ce_ref[chunk]
        slot = lax.rem(g, nb)
        c0 = pl.multiple_of(fg * tf, tf)
        return (
            pltpu.make_async_copy(wgu_ref.at[e, :, pl.ds(c0, tf)], wg_ring.at[slot], wsem.at[slot]),
            pltpu.make_async_copy(wgu_ref.at[e, :, pl.ds(d_ff + c0, tf)], wu_ring.at[slot], wsem.at[slot]),
            pltpu.make_async_copy(wd_ref.at[e, pl.ds(c0, tf), :], wd_ring.at[slot], wsem.at[slot]),
        )

    def w_start(g):
        @pl.when(g < n_w_tiles)
        def _():
            for cp in w_copies(g):
                cp.start()

    acc = lax.rem(c, 2)

    def x_copy(chunk, t):
        src0 = pl.multiple_of((cts_ref[chunk] + t) * rt, rt)
        return pltpu.make_async_copy(xs_ref.at[pl.ds(src0, rt)], stage_ref.at[t], xsem.at[t])

    def y_copy(chunk, t):
        a = lax.rem(chunk, 2)
        r0 = pl.multiple_of(t * rt, rt)
        dst0 = pl.multiple_of((cts_ref[chunk] + t) * rt, rt)
        return pltpu.make_async_copy(yacc_ref.at[a, pl.ds(r0, rt)], ys_ref.at[pl.ds(dst0, rt)], ysem.at[a])

    def x_start_all(chunk, n_tiles):
        for t in range(p["tpc"]):
            @pl.when(t < n_tiles)
            def _(t=t):
                x_copy(chunk, t).start()

    @pl.when(nt > 0)
    def _():
        @pl.when(c == 0)
        def _():
            for g in range(nb):
                w_start(g)
            x_start_all(c, nt)

        def load(t, _):
            x_copy(c, t).wait()
            r0 = pl.multiple_of(t * rt, rt)
            xbuf_ref[pl.ds(r0, rt), :] = stage_ref[t].astype(BF16)
            return 0

        lax.fori_loop(0, nt, load, 0)

        @pl.when(nt_next > 0)
        def _():
            x_start_all(c_next, nt_next)

        bd = bd_ref[0]

        def init(t, _):
            r0 = pl.multiple_of(t * rt, rt)
            yacc_ref[acc, pl.ds(r0, rt), :] = jnp.broadcast_to(bd, (rt, p["d"]))
            return 0

        lax.fori_loop(0, nt, init, 0)

        def hidden_tile(f, _):
            g = c * nf + f
            slot = lax.rem(g, nb)
            for cp in w_copies(g):
                cp.wait()

            bg = bgu_ref[0, pl.ds(f, 1), :]
            bu = bgu_ref[0, pl.ds(nf + f, 1), :]

            def ffn(rows):
                xt = xbuf_ref[0:rows, :]
                gate = jnp.dot(xt, wg_ring[slot].astype(BF16), preferred_element_type=F32) + bg
                up = jnp.dot(xt, wu_ring[slot].astype(BF16), preferred_element_type=F32) + bu
                gate = jnp.minimum(gate, SWIGLU_LIMIT)
                up = jnp.clip(up, -SWIGLU_LIMIT, SWIGLU_LIMIT)
                act = (up + 1.0) * gate * jax.nn.sigmoid(SWIGLU_ALPHA * gate)
                yacc_ref[acc, 0:rows, :] += jnp.dot(act.astype(BF16), wd_ring[slot].astype(BF16),
                                                    preferred_element_type=F32)

            for hs in _moe_row_steps(p["tpc"]):
                @pl.when(csel_ref[c] == hs)
                def _(hs=hs):
                    ffn(hs * (rt // MOE_STEP_DIV))

            w_start(g + nb)
            return 0

        lax.fori_loop(0, nf, hidden_tile, 0)

        def drain(chunk):
            def body(t, _):
                y_copy(chunk, t).wait()
                return 0

            lax.fori_loop(0, cnt_ref[chunk], body, 0)

        @pl.when(c > 0)
        def _():
            drain(jnp.maximum(c - 1, 0))

        def store(t, _):
            y_copy(c, t).start()
            return 0

        lax.fori_loop(0, nt, store, 0)

        @pl.when(nt_next == 0)
        def _():
            drain(c)

    @pl.when(c == nch - 1)
    def _():
        stage_ref[0] = jnp.zeros((rt, p["d"]), F32)

        def tail_copy(t):
            return pltpu.make_async_copy(stage_ref.at[0], ys_ref.at[pl.ds(pl.multiple_of(t * rt, rt), rt)],
                                         ysem.at[0])

        def tail_start(t, _):
            tail_copy(t).start()
            return 0

        def tail_wait(t, _):
            tail_copy(t).wait()
            return 0

        lax.fori_loop(used_ref[0], p["n_tiles_max"], tail_start, 0)
        lax.fori_loop(used_ref[0], p["n_tiles_max"], tail_wait, 0)


def _moe(xs, ce, cts, cnt, used, csel, w_gate_up, b_gate_up, w_down, b_down, p):
    d, tf, nf, rt, cap = p["d"], p["tf"], p["nf"], p["rt"], p["cap"]
    n_exp = p["n_exp"]
    nb = MOE_W_SLOTS
    bgu = b_gate_up.reshape(n_exp, 2 * nf, tf)
    bdn = b_down.reshape(n_exp, 1, d)

    est = (nb * 3 * d * tf * 4 + cap * d * 2 + 2 * cap * d * 4 + cap * d * 4 + 4 * rt * d * 4)
    grid_spec = pltpu.PrefetchScalarGridSpec(
        num_scalar_prefetch=5,
        grid=(p["n_chunks_max"],),
        in_specs=[
            pl.BlockSpec(memory_space=pl.ANY),
            pl.BlockSpec(memory_space=pl.ANY),
            pl.BlockSpec((1, 2 * nf, tf), lambda c, ce_r, cts_r, cnt_r, u_r, s_r: (ce_r[c], 0, 0)),
            pl.BlockSpec(memory_space=pl.ANY),
            pl.BlockSpec((1, 1, d), lambda c, ce_r, cts_r, cnt_r, u_r, s_r: (ce_r[c], 0, 0)),
        ],
        out_specs=pl.BlockSpec(memory_space=pl.ANY),
        scratch_shapes=[
            pltpu.VMEM((cap, d), BF16), pltpu.VMEM((2, cap, d), F32), pltpu.VMEM((p["tpc"], rt, d), F32),
            pltpu.VMEM((nb, d, tf), F32), pltpu.VMEM((nb, d, tf), F32), pltpu.VMEM((nb, tf, d), F32),
            pltpu.SemaphoreType.DMA((p["tpc"],)), pltpu.SemaphoreType.DMA((2,)), pltpu.SemaphoreType.DMA((nb,)),
        ],
    )
    return pl.pallas_call(
        functools.partial(_moe_kernel, p=p),
        grid_spec=grid_spec,
        out_shape=jax.ShapeDtypeStruct((p["n_slots"], d), F32),
        compiler_params=pltpu.CompilerParams(
            dimension_semantics=("arbitrary",), vmem_limit_bytes=_vmem_limit(est)),
        name="moe",
    )(ce, cts, cnt, used, csel, xs, w_gate_up, bgu, w_down, bdn)


def _combine_kernel(dest_ref, dest_next_ref, h1_ref, gate_ref, g_ref, ys_ref, o_ref, ybuf_ref, sem, *, p):
    tg = p["tg"]
    step = pl.program_id(0)
    slot = step % 2

    def gather(d_ref, buf_slot):
        def issue(j, _):
            i0 = pl.multiple_of(j * SUBLANES, SUBLANES)
            for s in range(SUBLANES):
                for k in range(TOP_K):
                    src = d_ref[i0 * TOP_K + (s * TOP_K + k)]
                    pltpu.make_async_copy(ys_ref.at[pl.ds(src, 1)], ybuf_ref.at[buf_slot, k, pl.ds(i0 + s, 1)],
                                          sem.at[buf_slot]).start(priority=k % 2)
            return 0

        lax.fori_loop(0, tg // SUBLANES, issue, 0)

    @pl.when(step == 0)
    def _():
        gather(dest_ref, 0)

    def reduce_tile(cur):
        @pl.when(step + 1 < pl.num_programs(0))
        def _():
            gather(dest_next_ref, 1 - cur)

        for k in range(TOP_K):
            pltpu.make_async_copy(ys_ref.at[pl.ds(0, tg)], ybuf_ref.at[cur, k], sem.at[cur]).wait()

        lane = lax.broadcasted_iota(I32, (tg, LANES), 1)
        gates = gate_ref[...]
        h = h1_ref[...]
        for k in range(TOP_K):
            g_k = jnp.sum(jnp.where(lane == k, gates, 0.0), axis=1, keepdims=True)
            h = h + ybuf_ref[cur, k] * g_k
        ms = jnp.mean(h * h, axis=-1, keepdims=True)
        o_ref[...] = (h * lax.rsqrt(ms + RMS_EPS)) * g_ref[...]

    for cur in range(2):
        @pl.when(slot == cur)
        def _(cur=cur):
            reduce_tile(cur)


def _combine(ys, dest_flat, h1, gates, g_final, p):
    tg, d, r = p["tg"], p["d"], p["r"]
    nt = r // tg
    return pl.pallas_call(
        functools.partial(_combine_kernel, p=p),
        grid=(nt,),
        in_specs=[
            pl.BlockSpec((tg * TOP_K,), lambda i: (i,), memory_space=pltpu.SMEM),
            pl.BlockSpec((tg * TOP_K,), lambda i: (jnp.minimum(i + 1, nt - 1),), memory_space=pltpu.SMEM),
            pl.BlockSpec((tg, d), lambda i: (i, 0)),
            pl.BlockSpec((tg, LANES), lambda i: (i, 0)),
            pl.BlockSpec((1, d), lambda i: (0, 0)),
            pl.BlockSpec(memory_space=pl.ANY),
        ],
        out_specs=pl.BlockSpec((tg, d), lambda i: (i, 0)),
        out_shape=jax.ShapeDtypeStruct((r, d), F32),
        scratch_shapes=[pltpu.VMEM((2, TOP_K, tg, d), F32), pltpu.SemaphoreType.DMA((2,))],
        compiler_params=pltpu.CompilerParams(dimension_semantics=("arbitrary",)),
        name="combine",
    )(dest_flat, dest_flat, h1, gates, g_final, ys)


def kernel(x, meta_tokens, norm_mix_g, w_in, conv_w, sink, w_attn_o, w_conv_o, w_out, norm_ffn_g,
           router_w, router_b, w_gate_up, b_gate_up, w_down, b_down, norm_final_g):
    b, seq, d = x.shape
    n_exp, d_ff = w_down.shape[1], w_down.shape[2]
    assert norm_mix_g.shape[0] == 1 and meta_tokens.shape[0] == N_META
    p = _plan(b, seq, d, n_exp, d_ff)
    x2 = x.reshape(p["r"], d)

    qkv, rest, qkvm, restm = _inproj(x2, meta_tokens.astype(F32), norm_mix_g[0].reshape(1, d), w_in[0], p)
    attn_o = _attention(qkv, qkvm, sink[0].astype(F32), p)
    h1, hn2, ti, gates, rank, counts = _mixer(
        attn_o, rest, restm, x2, conv_w[0], w_attn_o[0].astype(BF16), w_conv_o[0].astype(BF16),
        w_out[0].astype(BF16), norm_ffn_g[0].reshape(1, d), router_w[0].astype(BF16),
        router_b[0].reshape(1, n_exp), p)
    row_start, cnt, start, ce, cts, cnt_tiles, used, csel = _routing_tables(counts, p)
    dest = _dest(ti, rank, row_start, p)
    dest_flat = dest[:, :TOP_K].reshape(-1)
    xs = _dispatch(hn2, dest_flat, cnt, start, p)
    ys = _moe(xs, ce, cts, cnt_tiles, used, csel, w_gate_up[0], b_gate_up[0], w_down[0], b_down[0], p)
    out = _combine(ys, dest_flat, h1, gates, norm_final_g.reshape(1, d), p)
    return out.reshape(b, seq, d)
```

```python
import functools

import numpy as np
import jax
import jax.numpy as jnp
from jax import lax
from jax.experimental import pallas as pl
from jax.experimental.pallas import tpu as pltpu

N_META = 16
BLK = 128
WINDOW = 128
HEAD_DIM = 64
Q_PER_KV = 4
ROT_DIM = HEAD_DIM // 4
ROPE_THETA = 500000.0
CONV_K = 3
TOP_K = 4
SWIGLU_ALPHA = 1.702
SWIGLU_LIMIT = 7.0
RMS_EPS = 1e-5

LANES = 128
SUBLANES = 8
VMEM_LIMIT_CAP = 60000 * 1024
MOE_ROW_TILE = 256
MOE_CHUNK_TILES = 5
MOE_W_SLOTS = 3
ATTN_Q_BLOCKS = 2
INPROJ_W_SLOTS = 3
MOE_STEP_DIV = 4
MOE_EXTRA_STEPS = (14, 17, 18)

F32 = jnp.float32
BF16 = jnp.bfloat16
I32 = jnp.int32


def _vmem_limit(nbytes):
    return int(min(VMEM_LIMIT_CAP, max(32 * 1024 * 1024, nbytes * 5 // 4 + (4 << 20))))


def _plan(b, seq, d, n_exp, d_ff):
    attn = (d // 128) * HEAD_DIM
    kvd = attn // Q_PER_KV
    conv = d // 2
    r = b * seq
    p = dict(b=b, seq=seq, d=d, n_exp=n_exp, d_ff=d_ff, attn=attn, kvd=kvd, conv=conv, r=r)
    p["nq"] = attn // HEAD_DIM
    p["nkv"] = kvd // HEAD_DIM
    p["in_dim"] = attn + 2 * kvd + 3 * conv + 2 * d
    p["tn"] = 2 * kvd
    p["tm"] = min(1024, seq)
    p["n_qkv_tiles"] = (attn + 2 * kvd) // p["tn"]
    p["n_col_tiles"] = p["in_dim"] // p["tn"]
    p["rest_w"] = 3 * conv + 2 * d
    p["tc"] = min(256, seq)
    p["te"] = min(512, seq)
    p["tg"] = min(256, seq)
    p["tf"] = min(256, d_ff)
    p["nf"] = d_ff // p["tf"]
    p["rt"] = MOE_ROW_TILE
    p["tpc"] = MOE_CHUNK_TILES
    p["cap"] = MOE_ROW_TILE * MOE_CHUNK_TILES
    p["n_tiles_max"] = (r * TOP_K) // p["rt"] + n_exp
    p["n_chunks_max"] = n_exp + (p["n_tiles_max"] - n_exp) // p["tpc"]
    p["n_slots"] = p["n_tiles_max"] * p["rt"]
    assert seq % p["tm"] == 0 and seq % p["tc"] == 0 and seq % BLK == 0
    assert attn % p["tn"] == 0 and p["in_dim"] % p["tn"] == 0 and kvd % LANES == 0
    assert (r * TOP_K) % p["rt"] == 0 and d_ff % p["tf"] == 0
    assert p["rt"] & (p["rt"] - 1) == 0, "the zero-fill decomposition needs a power-of-two row tile"
    return p


def _rope_tables(seq):
    half = ROT_DIM // 2
    pos = jnp.arange(N_META + seq, dtype=F32)
    inv_freq = ROPE_THETA ** (-jnp.arange(0, ROT_DIM, 2, dtype=F32) / ROT_DIM)
    ang = pos[:, None] * inv_freq[None, :]
    cos, sin = jnp.cos(ang), jnp.sin(ang)
    lane = np.arange(LANES) % HEAD_DIM
    idx = np.where(lane < ROT_DIM, lane % half, 0)
    rot = jnp.asarray(lane < ROT_DIM)
    sign = jnp.asarray(np.where(lane < half, -1.0, 1.0).astype(np.float32))
    cos_l = jnp.where(rot[None, :], cos[:, idx], 1.0)
    sin_l = jnp.where(rot[None, :], sin[:, idx] * sign[None, :], 0.0)
    tab = jnp.stack([cos_l, sin_l]).astype(F32)
    return tab[:, N_META:], tab[:, :N_META]


def _rope(t, cos, sin, n_cols):
    lane = lax.broadcasted_iota(I32, (t.shape[0], LANES), 1)
    first = (lane % HEAD_DIM) < (ROT_DIM // 2)
    outs = []
    for c in range(t.shape[1] // LANES):
        s = t[:, c * LANES:(c + 1) * LANES]
        if c * LANES < n_cols:
            partner = jnp.where(first, pltpu.roll(s, LANES - ROT_DIM // 2, 1), pltpu.roll(s, ROT_DIM // 2, 1))
            s = s * cos + partner * sin
        outs.append(s)
    return jnp.concatenate(outs, axis=1)


def _inproj_kernel(x_ref, meta_ref, g_ref, w_ref, csx_ref, csm_ref,
                   qkv_ref, rest_ref, qkvm_ref, restm_ref,
                   hn_ref, w_ring, obuf_ref, mbuf_ref, wsem, osem, msem, *, p):
    tm, tn, kvd, nn, nqkv, nb = p["tm"], p["tn"], p["kvd"], p["n_col_tiles"], p["n_qkv_tiles"], INPROJ_W_SLOTS
    n_q_tiles = p["attn"] // tn
    m = pl.program_id(0)
    n_w_tiles = pl.num_programs(0) * nn
    row0 = pl.multiple_of(m * tm, tm)

    def w_copy(g):
        c0 = pl.multiple_of(lax.rem(g, nn) * tn, tn)
        slot = lax.rem(g, nb)
        return pltpu.make_async_copy(w_ref.at[:, pl.ds(c0, tn)], w_ring.at[slot], wsem.at[slot])

    def w_start(g):
        @pl.when(g < n_w_tiles)
        def _():
            w_copy(g).start()

    @pl.when(m == 0)
    def _():
        for g in range(nb):
            w_start(g)

    g_row = g_ref[...]

    def norm(v):
        ms = jnp.mean(v * v, axis=-1, keepdims=True)
        return ((v * lax.rsqrt(ms + RMS_EPS)) * g_row).astype(BF16)

    rows = min(128, tm)

    def norm_rows(i, _):
        r0 = pl.multiple_of(i * rows, rows)
        hn_ref[pl.ds(r0, rows), :] = norm(x_ref[pl.ds(r0, rows), :])
        return 0

    lax.fori_loop(0, tm // rows, norm_rows, 0)
    hn_ref[tm:tm + N_META, :] = norm(meta_ref[...])

    def out_copy(dst_ref, col0, slot):
        return pltpu.make_async_copy(obuf_ref.at[slot], dst_ref.at[pl.ds(row0, tm), pl.ds(col0, tn)], osem.at[slot])

    def meta_copy(dst_ref, col0, slot):
        return pltpu.make_async_copy(mbuf_ref.at[slot], dst_ref.at[0, :, pl.ds(col0, tn)], msem.at[slot])

    def wait_slot(slot):
        out_copy(rest_ref, 0, slot).wait()

        @pl.when(m == 0)
        def _():
            meta_copy(restm_ref, 0, slot).wait()

    def column_tile(n, _):
        g = m * nn + n
        wslot = lax.rem(g, nb)
        oslot = lax.rem(n, 2)
        w_copy(g).wait()

        @pl.when(n >= 2)
        def _():
            wait_slot(oslot)

        def project():
            res = jnp.dot(hn_ref[...], w_ring[wslot].astype(BF16), preferred_element_type=F32)
            return res[:tm], res[tm:]

        def emit(rx, rm, dst_ref, dstm_ref, col0):
            obuf_ref[oslot] = rx.astype(BF16)
            mbuf_ref[oslot] = rm.astype(BF16)
            out_copy(dst_ref, col0, oslot).start()

            @pl.when(m == 0)
            def _():
                meta_copy(dstm_ref, col0, oslot).start()

        def store_qkv(n_cols, scale):
            res_x, res_m = project()
            rx = _rope(res_x, csx_ref[0], csx_ref[1], n_cols)
            rm = _rope(res_m, csm_ref[0], csm_ref[1], n_cols)
            if scale != 1.0:
                rx = rx * scale
            emit(rx, rm, qkv_ref, qkvm_ref, pl.multiple_of(n * tn, tn))

        @pl.when(n < n_q_tiles)
        def _():
            store_qkv(tn, HEAD_DIM ** -0.5)

        @pl.when(n == n_q_tiles)
        def _():
            store_qkv(kvd, 1.0)

        @pl.when(n >= nqkv)
        def _():
            res_x, res_m = project()
            emit(res_x, res_m, rest_ref, restm_ref, pl.multiple_of((n - nqkv) * tn, tn))

        w_start(g + nb)
        return 0

    lax.fori_loop(0, nn, column_tile, 0)
    wait_slot(0)
    wait_slot(1)


def _inproj(x2, meta, g, w_in, p):
    tm, tn, d, r = p["tm"], p["tn"], p["d"], p["r"]
    nm, nn, nqkv = r // tm, p["n_col_tiles"], p["n_qkv_tiles"]
    assert nn >= 2
    csx, csm = _rope_tables(p["seq"])
    spt = p["seq"] // tm
    qkv_w = nqkv * tn
    nb = INPROJ_W_SLOTS
    est = (2 * tm * d * 4 + (tm + 16) * d * 2 + nb * d * tn * 4 + 2 * (tm + 16) * tn * 2
           + 4 * 2 * tm * LANES * 4 + 4 * (tm + 16) * tn * 4)
    return pl.pallas_call(
        functools.partial(_inproj_kernel, p=p),
        grid=(nm,),
        in_specs=[
            pl.BlockSpec((tm, d), lambda m: (m, 0)),
            pl.BlockSpec((N_META, d), lambda m: (0, 0)),
            pl.BlockSpec((1, d), lambda m: (0, 0)),
            pl.BlockSpec(memory_space=pl.ANY),
            pl.BlockSpec((2, tm, LANES), lambda m: (0, m % spt, 0)),
            pl.BlockSpec((2, N_META, LANES), lambda m: (0, 0, 0)),
        ],
        out_specs=[pl.BlockSpec(memory_space=pl.ANY)] * 4,
        out_shape=[
            jax.ShapeDtypeStruct((r, qkv_w), BF16),
            jax.ShapeDtypeStruct((r, p["rest_w"]), BF16),
            jax.ShapeDtypeStruct((1, N_META, qkv_w), BF16),
            jax.ShapeDtypeStruct((1, N_META, p["rest_w"]), BF16),
        ],
        scratch_shapes=[
            pltpu.VMEM((tm + N_META, d), BF16), pltpu.VMEM((nb, d, tn), F32),
            pltpu.VMEM((2, tm, tn), BF16), pltpu.VMEM((2, N_META, tn), BF16),
            pltpu.SemaphoreType.DMA((nb,)), pltpu.SemaphoreType.DMA((2,)), pltpu.SemaphoreType.DMA((2,)),
        ],
        compiler_params=pltpu.CompilerParams(
            dimension_semantics=("arbitrary",), vmem_limit_bytes=_vmem_limit(est)),
        name="inproj",
    )(x2, meta, g, w_in, csx, csm)


def _attn_kernel(sink_ref, q_ref, k0_ref, k1_ref, k2_ref, k3_ref, v0_ref, v1_ref, v2_ref, v3_ref, km_ref, vm_ref,
                 o_ref, kcat_ref, vcat_ref, *, p):
    k_refs = (k0_ref, k1_ref, k2_ref, k3_ref)
    v_refs = (v0_ref, v1_ref, v2_ref, v3_ref)
    for sub in range(ATTN_Q_BLOCKS):
        _attn_block(sink_ref, q_ref.at[sub * BLK:(sub + 1) * BLK], k_refs[sub:sub + 3], v_refs[sub:sub + 3],
                    km_ref, vm_ref, o_ref.at[sub * BLK:(sub + 1) * BLK], kcat_ref.at[sub], vcat_ref.at[sub],
                    ATTN_Q_BLOCKS * pl.program_id(1) + sub, p)


def _attn_block(sink_ref, q_ref, k_refs, v_refs, km_ref, vm_ref, o_ref, kcat_ref, vcat_ref, n, p):
    seq, nkv = p["seq"], p["nkv"]
    nband = 3 * BLK
    nkeys = nband + N_META
    for j, (kr, vr) in enumerate(zip(k_refs, v_refs)):
        kcat_ref[j * BLK:(j + 1) * BLK, :] = kr[...]
        vcat_ref[j * BLK:(j + 1) * BLK, :] = vr[...]
    kcat_ref[nband:nkeys, :] = km_ref[...]
    vcat_ref[nband:nkeys, :] = vm_ref[...]

    qi = lax.broadcasted_iota(I32, (BLK, nkeys), 0)
    sj = lax.broadcasted_iota(I32, (BLK, nkeys), 1)
    kx = (n - 1) * BLK + sj
    dq = n * BLK + qi - kx
    visible = (sj >= nband) | ((jnp.abs(dq) <= WINDOW) & (kx >= 0) & (kx < seq))
    neg = jnp.finfo(F32).min
    ones = jnp.ones((nkeys, HEAD_DIM), BF16)
    gi = lax.broadcasted_iota(I32, (Q_PER_KV, 1, 1), 0)
    for h in range(nkv):
        kh = kcat_ref[:, h * HEAD_DIM:(h + 1) * HEAD_DIM]
        vh = jnp.concatenate([vcat_ref[:, h * HEAD_DIM:(h + 1) * HEAD_DIM], ones], axis=1)
        heads = [h * Q_PER_KV + g for g in range(Q_PER_KV)]
        qs = jnp.concatenate([q_ref[:, hd * HEAD_DIM:(hd + 1) * HEAD_DIM] for hd in heads], axis=0)
        s = lax.dot_general(qs, kh, (((1,), (1,)), ((), ())), preferred_element_type=F32)
        s = jnp.where(visible[None], s.reshape(Q_PER_KV, BLK, nkeys), neg)
        snk = jnp.zeros((Q_PER_KV, 1, 1), F32)
        for g, hd in enumerate(heads):
            snk = jnp.where(gi == g, sink_ref[hd], snk)
        m = jnp.maximum(jnp.max(s, axis=2, keepdims=True), snk)
        e = jnp.exp(s - m).reshape(Q_PER_KV * BLK, nkeys)
        pv = jnp.dot(e.astype(BF16), vh, preferred_element_type=F32)
        pv = pv.reshape(Q_PER_KV, BLK, 2 * HEAD_DIM)
        o = pv[:, :, :HEAD_DIM] / (pv[:, :, HEAD_DIM:HEAD_DIM + 1] + jnp.exp(snk - m))
        for g, hd in enumerate(heads):
            o_ref[:, hd * HEAD_DIM:(hd + 1) * HEAD_DIM] = o[g].astype(BF16)


def _attention(qkv, qkvm, sink, p):
    attn, kvd, seq, b = p["attn"], p["kvd"], p["seq"], p["b"]
    nbx = seq // BLK
    kc = attn // kvd
    nkeys = 3 * BLK + N_META
    qb = ATTN_Q_BLOCKS
    assert nbx % qb == 0
    npair = nbx // qb

    def kv_spec(off, col):
        return pl.BlockSpec((BLK, kvd), lambda bi, j: (bi * nbx + jnp.clip(qb * j + off, 0, nbx - 1), col))

    offs = range(-1, qb + 1)
    return pl.pallas_call(
        functools.partial(_attn_kernel, p=p),
        grid=(b, npair),
        in_specs=[
            pl.BlockSpec(memory_space=pltpu.SMEM),
            pl.BlockSpec((qb * BLK, attn), lambda bi, j: (bi * npair + j, 0)),
            *[kv_spec(off, kc) for off in offs],
            *[kv_spec(off, kc + 1) for off in offs],
            pl.BlockSpec((None, N_META, kvd), lambda bi, j: (0, 0, kc)),
            pl.BlockSpec((None, N_META, kvd), lambda bi, j: (0, 0, kc + 1)),
        ],
        out_specs=pl.BlockSpec((qb * BLK, attn), lambda bi, j: (bi * npair + j, 0)),
        out_shape=jax.ShapeDtypeStruct((p["r"], attn), BF16),
        scratch_shapes=[pltpu.VMEM((qb, nkeys, kvd), BF16), pltpu.VMEM((qb, nkeys, kvd), BF16)],
        compiler_params=pltpu.CompilerParams(dimension_semantics=("arbitrary", "arbitrary")),
        name="attn",
    )(sink, qkv, *([qkv] * (2 * (qb + 2))), qkvm, qkvm)


def _lane_pack(cols, rows, dtype):
    lane = lax.broadcasted_iota(I32, (rows, LANES), 1)
    out = jnp.zeros((rows, LANES), dtype)
    for k, c in enumerate(cols):
        out = jnp.where(lane == k, c.astype(dtype), out)
    return out


def _mixer_kernel(attn_ref, rest_ref, prev_ref, next_ref, restm_ref, x_ref, cw_ref, wao_ref, wco_ref,
                  wout_ref, g_ref, rw_ref, rb_ref,
                  h1_ref, hn2_ref, ti_ref, gate_ref, rank_ref, cnt_ref, carry_ref, *, p):
    tc, conv, d, n_exp, seq = p["tc"], p["conv"], p["d"], p["n_exp"], p["seq"]
    i = pl.program_id(0)
    tiles_per_seq = seq // tc
    is_first = (i % tiles_per_seq) == 0
    is_last = (i % tiles_per_seq) == tiles_per_seq - 1
    o_ch, o_cb, o_cc, o_ga, o_gc = 0, conv, 2 * conv, 3 * conv, 3 * conv + d

    @pl.when(i == 0)
    def _():
        carry_ref[...] = jnp.zeros_like(carry_ref)

    def u_of(ref):
        return ref[:, o_cc:o_cc + conv].astype(F32) * ref[:, o_ch:o_ch + conv].astype(F32)

    u = u_of(rest_ref)
    last = N_META - 1
    u_prev = jnp.where(is_first, u_of(restm_ref)[last:last + 1], u_of(prev_ref)[last:last + 1])
    u_next = jnp.where(is_last, 0.0, u_of(next_ref)[0:1])
    row = lax.broadcasted_iota(I32, (tc, conv), 0)
    u_m1 = jnp.where(row == 0, u_prev, pltpu.roll(u, 1, 0))
    u_p1 = jnp.where(row == tc - 1, u_next, pltpu.roll(u, tc - 1, 0))
    cw = cw_ref[...]
    cv = u_m1 * cw[0:1] + u * cw[1:2] + u_p1 * cw[2:3]
    yc_in = (rest_ref[:, o_cb:o_cb + conv].astype(F32) * cv).astype(BF16)
    y_conv = jnp.dot(yc_in, wco_ref[...], preferred_element_type=F32)
    y_attn = jnp.dot(attn_ref[...], wao_ref[...], preferred_element_type=F32)
    g_a = rest_ref[:, o_ga:o_ga + d].astype(F32)
    g_c = rest_ref[:, o_gc:o_gc + d].astype(F32)
    merged = jax.nn.sigmoid(g_a) * y_attn + jax.nn.sigmoid(g_c) * y_conv
    h1 = x_ref[...] + jnp.dot(merged.astype(BF16), wout_ref[...], preferred_element_type=F32)
    h1_ref[...] = h1
    ms = jnp.mean(h1 * h1, axis=-1, keepdims=True)
    hn2 = (h1 * lax.rsqrt(ms + RMS_EPS)) * g_ref[...]
    hn2_ref[...] = hn2

    logits = jnp.dot(hn2.astype(BF16), rw_ref[...], preferred_element_type=F32) + rb_ref[...]
    lane = lax.broadcasted_iota(I32, (tc, n_exp), 1).astype(F32)
    sel = jnp.zeros((tc, n_exp), F32)
    tv, ti = [], []
    cur = logits
    for _ in range(TOP_K):
        m = jnp.max(cur, axis=1, keepdims=True)
        idx = jnp.min(jnp.where(cur == m, lane, float(n_exp)), axis=1, keepdims=True)
        hit = lane == idx
        tv.append(m)
        ti.append(idx)
        sel = jnp.where(hit, 1.0, sel)
        cur = jnp.where(hit, -jnp.inf, cur)
    ex = [jnp.exp(v - tv[0]) for v in tv]
    tot = ex[0] + ex[1] + ex[2] + ex[3]
    gates = [e / tot for e in ex]

    r_i = lax.broadcasted_iota(I32, (tc, tc), 0)
    c_i = lax.broadcasted_iota(I32, (tc, tc), 1)
    lower = jnp.where(r_i > c_i, 1.0, 0.0).astype(BF16)
    before = jnp.dot(lower, sel.astype(BF16), preferred_element_type=F32) + carry_ref[0:1, 0:n_exp]
    ranks = [jnp.sum(jnp.where(lane == t, before, 0.0), axis=1, keepdims=True) for t in ti]
    carry_ref[0:1, 0:n_exp] = carry_ref[0:1, 0:n_exp] + jnp.sum(sel, axis=0, keepdims=True)

    ti_ref[...] = _lane_pack(ti, tc, I32)
    gate_ref[...] = _lane_pack(gates, tc, F32)
    rank_ref[...] = _lane_pack(ranks, tc, I32)
    cnt_ref[...] = carry_ref[...]


def _mixer(attn_o, rest, restm, x2, conv_w, wao, wco, wout, g_ffn, router_w, router_b, p):
    tc, d, r, rw, n_exp = p["tc"], p["d"], p["r"], p["rest_w"], p["n_exp"]
    nt = r // tc
    sub = tc // N_META
    n16 = r // N_META
    const = lambda shape: pl.BlockSpec(shape, lambda i: (0,) * len(shape))
    est = (2 * (tc * p["attn"] * 2 + tc * rw * 2 + tc * d * 4 + 3 * N_META * rw * 2)
           + 2 * (p["attn"] * d + p["conv"] * d + d * d + d * n_exp) * 2
           + 2 * (2 * tc * d * 4 + 3 * tc * LANES * 4) + 10 * tc * d * 4)
    return pl.pallas_call(
        functools.partial(_mixer_kernel, p=p),
        grid=(nt,),
        in_specs=[
            pl.BlockSpec((tc, p["attn"]), lambda i: (i, 0)),
            pl.BlockSpec((tc, rw), lambda i: (i, 0)),
            pl.BlockSpec((N_META, rw), lambda i: (jnp.maximum(i * sub - 1, 0), 0)),
            pl.BlockSpec((N_META, rw), lambda i: (jnp.minimum((i + 1) * sub, n16 - 1), 0)),
            pl.BlockSpec((None, N_META, rw), lambda i: (0, 0, 0)),
            pl.BlockSpec((tc, d), lambda i: (i, 0)),
            const((CONV_K, p["conv"])),
            const((p["attn"], d)), const((p["conv"], d)), const((d, d)),
            const((1, d)), const((d, n_exp)), const((1, n_exp)),
        ],
        out_specs=[
            pl.BlockSpec((tc, d), lambda i: (i, 0)),
            pl.BlockSpec((tc, d), lambda i: (i, 0)),
            pl.BlockSpec((tc, LANES), lambda i: (i, 0)),
            pl.BlockSpec((tc, LANES), lambda i: (i, 0)),
            pl.BlockSpec((tc, LANES), lambda i: (i, 0)),
            pl.BlockSpec((8, LANES), lambda i: (0, 0)),
        ],
        out_shape=[
            jax.ShapeDtypeStruct((r, d), F32),
            jax.ShapeDtypeStruct((r, d), F32),
            jax.ShapeDtypeStruct((r, LANES), I32),
            jax.ShapeDtypeStruct((r, LANES), F32),
            jax.ShapeDtypeStruct((r, LANES), I32),
            jax.ShapeDtypeStruct((8, LANES), F32),
        ],
        scratch_shapes=[pltpu.VMEM((8, LANES), F32)],
        compiler_params=pltpu.CompilerParams(
            dimension_semantics=("arbitrary",), vmem_limit_bytes=_vmem_limit(est)),
        name="mixer",
    )(attn_o, rest, rest, rest, restm, x2, conv_w, wao, wco, wout, g_ffn, router_w, router_b)


def _dest_kernel(ti_ref, rank_ref, start_ref, dest_ref):
    rows = ti_ref.shape[0]
    lane = lax.broadcasted_iota(I32, (rows, LANES), 1).astype(F32)
    ti = ti_ref[...].astype(F32)
    rank = rank_ref[...].astype(F32)
    start = start_ref[0:1, :].astype(F32)
    out = jnp.zeros((rows, LANES), F32)
    for k in range(TOP_K):
        e_k = jnp.sum(jnp.where(lane == k, ti, 0.0), axis=1, keepdims=True)
        r_k = jnp.sum(jnp.where(lane == k, rank, 0.0), axis=1, keepdims=True)
        s_k = jnp.sum(jnp.where(lane == e_k, start, 0.0), axis=1, keepdims=True)
        out = jnp.where(lane == k, s_k + r_k, out)
    dest_ref[...] = out.astype(I32)


def _dest(ti, rank, row_start, p):
    r = p["r"]
    rows = min(2048, r)
    return pl.pallas_call(
        _dest_kernel,
        grid=(r // rows,),
        in_specs=[pl.BlockSpec((rows, LANES), lambda i: (i, 0)),
                  pl.BlockSpec((rows, LANES), lambda i: (i, 0)),
                  pl.BlockSpec((8, LANES), lambda i: (0, 0))],
        out_specs=pl.BlockSpec((rows, LANES), lambda i: (i, 0)),
        out_shape=jax.ShapeDtypeStruct((r, LANES), I32),
        name="dest",
    )(ti, rank, row_start)


def _moe_row_steps(tpc):
    whole = range(MOE_STEP_DIV, MOE_STEP_DIV * tpc + 1, MOE_STEP_DIV)
    return sorted(set(whole) | {h for h in MOE_EXTRA_STEPS if h < MOE_STEP_DIV * tpc})


def _routing_tables(counts, p):
    rt, tpc, n_exp, nch = p["rt"], p["tpc"], p["n_exp"], p["n_chunks_max"]
    cnt = counts[0, :n_exp].astype(I32)
    ntile = (cnt + rt - 1) // rt
    tile_start = jnp.cumsum(ntile) - ntile
    nchunk = (ntile + tpc - 1) // tpc
    chunk_end = jnp.cumsum(nchunk)
    c = jnp.arange(nch, dtype=I32)
    ce = jnp.minimum(jnp.sum((chunk_end[None, :] <= c[:, None]).astype(I32), axis=1), n_exp - 1)
    first = c - (chunk_end - nchunk)[ce]
    c_nt = jnp.clip(ntile[ce] - first * tpc, 0, tpc)
    c_nt = jnp.where(c < chunk_end[-1], c_nt, 0)
    c_ts = tile_start[ce] + first * tpc
    last_e = ce[jnp.maximum(chunk_end[-1] - 1, 0)]
    ce = jnp.where(c_nt > 0, ce, last_e)
    row_start = jnp.zeros((8, LANES), I32).at[0, :n_exp].set(tile_start * rt)
    used = jnp.stack([jnp.sum(ntile), chunk_end[-1]]).astype(I32)
    unit = rt // MOE_STEP_DIV
    valid = jnp.clip(cnt[ce] - first * (tpc * rt), 0, c_nt * rt)
    halves = (valid + unit - 1) // unit
    steps = jnp.asarray(_moe_row_steps(tpc), I32)
    first_ge = jnp.sum((steps[None, :] < halves[:, None]).astype(I32), axis=1)
    c_sel = steps[jnp.minimum(first_ge, steps.shape[0] - 1)]
    c_sel = jnp.where(c_nt > 0, c_sel, 0).astype(I32)
    return row_start, cnt, tile_start * rt, ce.astype(I32), c_ts.astype(I32), c_nt.astype(I32), used, c_sel


def _dispatch_kernel(cnt_ref, start_ref, dest_ref, hn_ref, xs_ref, zero_ref, sem, zsem, *, p):
    te, n_exp, rt, d = p["te"], p["n_exp"], p["rt"], p["d"]
    step = pl.program_id(0)

    @pl.when(step == 0)
    def _():
        zero_ref[...] = jnp.zeros_like(zero_ref)

        def per_expert(e, wait):
            def zero_rows(dst0, size):
                cp = pltpu.make_async_copy(zero_ref.at[pl.ds(0, size)], xs_ref.at[pl.ds(dst0, size)], zsem)
                cp.wait() if wait else cp.start()

            cnt = cnt_ref[e]
            cur = start_ref[e] + cnt
            pad = (rt - (cnt & (rt - 1))) & (rt - 1)
            head = (SUBLANES - (cur & (SUBLANES - 1))) & (SUBLANES - 1)
            for j in range(SUBLANES - 1):
                @pl.when(j < head)
                def _(j=j):
                    zero_rows(cur + j, 1)
            cur = cur + head
            rem = pad - head
            size = SUBLANES
            while size < rt:
                @pl.when((rem & size) != 0)
                def _(cur=cur, size=size):
                    zero_rows(pl.multiple_of(cur, SUBLANES), size)
                cur = cur + (rem & size)
                size *= 2

        lax.fori_loop(0, n_exp, lambda e, _: (per_expert(e, False), 0)[1], 0)
        lax.fori_loop(0, n_exp, lambda e, _: (per_expert(e, True), 0)[1], 0)

        half = rt // 2
        used = (start_ref[n_exp - 1] + cnt_ref[n_exp - 1] + rt - 1) // rt

        def tail_copy(t, j):
            r0 = pl.multiple_of(t * rt + j * half, half)
            return pltpu.make_async_copy(zero_ref, xs_ref.at[pl.ds(r0, half)], zsem)

        def tail_start(t, _):
            tail_copy(t, 0).start()
            tail_copy(t, 1).start()
            return 0

        def tail_wait(t, _):
            tail_copy(t, 0).wait()
            tail_copy(t, 1).wait()
            return 0

        lax.fori_loop(used, p["n_tiles_max"], tail_start, 0)
        lax.fori_loop(used, p["n_tiles_max"], tail_wait, 0)

    def row_copy(i, k):
        return pltpu.make_async_copy(hn_ref.at[pl.ds(i, 1)], xs_ref.at[pl.ds(dest_ref[i * TOP_K + k], 1)], sem)

    def issue(i, _):
        for k in range(TOP_K):
            row_copy(i, k).start(priority=k % 2)
        return 0

    lax.fori_loop(0, te, issue, 0)

    for k in range(TOP_K):
        pltpu.make_async_copy(hn_ref, xs_ref.at[pl.ds(0, te)], sem).wait()


def _dispatch(hn2, dest_flat, cnt, start, p):
    te, d, r = p["te"], p["d"], p["r"]
    return pl.pallas_call(
        functools.partial(_dispatch_kernel, p=p),
        grid=(r // te,),
        in_specs=[
            pl.BlockSpec(memory_space=pltpu.SMEM),
            pl.BlockSpec(memory_space=pltpu.SMEM),
            pl.BlockSpec((te * TOP_K,), lambda i: (i,), memory_space=pltpu.SMEM),
            pl.BlockSpec((te, d), lambda i: (i, 0)),
        ],
        out_specs=pl.BlockSpec(memory_space=pl.ANY),
        out_shape=jax.ShapeDtypeStruct((p["n_slots"], d), F32),
        scratch_shapes=[pltpu.VMEM((p["rt"] // 2, d), F32), pltpu.SemaphoreType.DMA, pltpu.SemaphoreType.DMA],
        compiler_params=pltpu.CompilerParams(dimension_semantics=("arbitrary",)),
        name="dispatch",
    )(cnt, start, dest_flat, hn2)


def _moe_kernel(ce_ref, cts_ref, cnt_ref, used_ref, csel_ref, xs_ref, wgu_ref, bgu_ref, wd_ref, bd_ref,
                ys_ref, xbuf_ref, yacc_ref, stage_ref, wg_ring, wu_ring, wd_ring, xsem, ysem, wsem, *, p):
    rt, nf, nch, tf, d_ff, nb = p["rt"], p["nf"], p["n_chunks_max"], p["tf"], p["d_ff"], MOE_W_SLOTS
    c = pl.program_id(0)
    nt = cnt_ref[c]
    c_next = jnp.minimum(c + 1, nch - 1)
    nt_next = jnp.where(c + 1 < nch, cnt_ref[c_next], 0)
    n_w_tiles = used_ref[1] * nf

    def w_copies(g):
        chunk = lax.div(g, nf)
        fg = lax.rem(g, nf)
        e = ce_ref[chunk]
        slot = lax.rem(g, nb)
        c0 = pl.multiple_of(fg * tf, tf)
        return (
            pltpu.make_async_copy(wgu_ref.at[e, :, pl.ds(c0, tf)], wg_ring.at[slot], wsem.at[slot]),
            pltpu.make_async_copy(wgu_ref.at[e, :, pl.ds(d_ff + c0, tf)], wu_ring.at[slot], wsem.at[slot]),
            pltpu.make_async_copy(wd_ref.at[e, pl.ds(c0, tf), :], wd_ring.at[slot], wsem.at[slot]),
        )

    def w_start(g):
        @pl.when(g < n_w_tiles)
        def _():
            for cp in w_copies(g):
                cp.start()

    acc = lax.rem(c, 2)

    def x_copy(chunk, t):
        src0 = pl.multiple_of((cts_ref[chunk] + t) * rt, rt)
        return pltpu.make_async_copy(xs_ref.at[pl.ds(src0, rt)], stage_ref.at[t], xsem.at[t])

    def y_copy(chunk, t):
        a = lax.rem(chunk, 2)
        r0 = pl.multiple_of(t * rt, rt)
        dst0 = pl.multiple_of((cts_ref[chunk] + t) * rt, rt)
        return pltpu.make_async_copy(yacc_ref.at[a, pl.ds(r0, rt)], ys_ref.at[pl.ds(dst0, rt)], ysem.at[a])

    def x_start_all(chunk, n_tiles):
        for t in range(p["tpc"]):
            @pl.when(t < n_tiles)
            def _(t=t):
                x_copy(chunk, t).start()

    @pl.when(nt > 0)
    def _():
        @pl.when(c == 0)
        def _():
            for g in range(nb):
                w_start(g)
            x_start_all(c, nt)

        def load(t, _):
            x_copy(c, t).wait()
            r0 = pl.multiple_of(t * rt, rt)
            xbuf_ref[pl.ds(r0, rt), :] = stage_ref[t].astype(BF16)
            return 0

        lax.fori_loop(0, nt, load, 0)

        @pl.when(nt_next > 0)
        def _():
            x_start_all(c_next, nt_next)

        bd = bd_ref[0]

        def init(t, _):
            r0 = pl.multiple_of(t * rt, rt)
            yacc_ref[acc, pl.ds(r0, rt), :] = jnp.broadcast_to(bd, (rt, p["d"]))
            return 0

        lax.fori_loop(0, nt, init, 0)

        def hidden_tile(f, _):
            g = c * nf + f
            slot = lax.rem(g, nb)
            for cp in w_copies(g):
                cp.wait()

            bg = bgu_ref[0, pl.ds(f, 1), :]
            bu = bgu_ref[0, pl.ds(nf + f, 1), :]

            def ffn(rows):
                xt = xbuf_ref[0:rows, :]
                gate = jnp.dot(xt, wg_ring[slot].astype(BF16), preferred_element_type=F32) + bg
                up = jnp.dot(xt, wu_ring[slot].astype(BF16), preferred_element_type=F32) + bu
                gate = jnp.minimum(gate, SWIGLU_LIMIT)
                up = jnp.clip(up, -SWIGLU_LIMIT, SWIGLU_LIMIT)
                act = (up + 1.0) * gate * jax.nn.sigmoid(SWIGLU_ALPHA * gate)
                yacc_ref[acc, 0:rows, :] += jnp.dot(act.astype(BF16), wd_ring[slot].astype(BF16),
                                                    preferred_element_type=F32)

            for hs in _moe_row_steps(p["tpc"]):
                @pl.when(csel_ref[c] == hs)
                def _(hs=hs):
                    ffn(hs * (rt // MOE_STEP_DIV))

            w_start(g + nb)
            return 0

        lax.fori_loop(0, nf, hidden_tile, 0)

        def drain(chunk):
            def body(t, _):
                y_copy(chunk, t).wait()
                return 0

            lax.fori_loop(0, cnt_ref[chunk], body, 0)

        @pl.when(c > 0)
        def _():
            drain(jnp.maximum(c - 1, 0))

        def store(t, _):
            y_copy(c, t).start()
            return 0

        lax.fori_loop(0, nt, store, 0)

        @pl.when(nt_next == 0)
        def _():
            drain(c)

    @pl.when(c == nch - 1)
    def _():
        stage_ref[0] = jnp.zeros((rt, p["d"]), F32)

        def tail_copy(t):
            return pltpu.make_async_copy(stage_ref.at[0], ys_ref.at[pl.ds(pl.multiple_of(t * rt, rt), rt)],
                                         ysem.at[0])

        def tail_start(t, _):
            tail_copy(t).start()
            return 0

        def tail_wait(t, _):
            tail_copy(t).wait()
            return 0

        lax.fori_loop(used_ref[0], p["n_tiles_max"], tail_start, 0)
        lax.fori_loop(used_ref[0], p["n_tiles_max"], tail_wait, 0)


def _moe(xs, ce, cts, cnt, used, csel, w_gate_up, b_gate_up, w_down, b_down, p):
    d, tf, nf, rt, cap = p["d"], p["tf"], p["nf"], p["rt"], p["cap"]
    n_exp = p["n_exp"]
    nb = MOE_W_SLOTS
    bgu = b_gate_up.reshape(n_exp, 2 * nf, tf)
    bdn = b_down.reshape(n_exp, 1, d)

    est = (nb * 3 * d * tf * 4 + cap * d * 2 + 2 * cap * d * 4 + cap * d * 4 + 4 * rt * d * 4)
    grid_spec = pltpu.PrefetchScalarGridSpec(
        num_scalar_prefetch=5,
        grid=(p["n_chunks_max"],),
        in_specs=[
            pl.BlockSpec(memory_space=pl.ANY),
            pl.BlockSpec(memory_space=pl.ANY),
            pl.BlockSpec((1, 2 * nf, tf), lambda c, ce_r, cts_r, cnt_r, u_r, s_r: (ce_r[c], 0, 0)),
            pl.BlockSpec(memory_space=pl.ANY),
            pl.BlockSpec((1, 1, d), lambda c, ce_r, cts_r, cnt_r, u_r, s_r: (ce_r[c], 0, 0)),
        ],
        out_specs=pl.BlockSpec(memory_space=pl.ANY),
        scratch_shapes=[
            pltpu.VMEM((cap, d), BF16), pltpu.VMEM((2, cap, d), F32), pltpu.VMEM((p["tpc"], rt, d), F32),
            pltpu.VMEM((nb, d, tf), F32), pltpu.VMEM((nb, d, tf), F32), pltpu.VMEM((nb, tf, d), F32),
            pltpu.SemaphoreType.DMA((p["tpc"],)), pltpu.SemaphoreType.DMA((2,)), pltpu.SemaphoreType.DMA((nb,)),
        ],
    )
    return pl.pallas_call(
        functools.partial(_moe_kernel, p=p),
        grid_spec=grid_spec,
        out_shape=jax.ShapeDtypeStruct((p["n_slots"], d), F32),
        compiler_params=pltpu.CompilerParams(
            dimension_semantics=("arbitrary",), vmem_limit_bytes=_vmem_limit(est)),
        name="moe",
    )(ce, cts, cnt, used, csel, xs, w_gate_up, bgu, w_down, bdn)


def _combine_kernel(dest_ref, dest_next_ref, h1_ref, gate_ref, g_ref, ys_ref, o_ref, ybuf_ref, sem, *, p):
    tg = p["tg"]
    step = pl.program_id(0)
    slot = step % 2

    def gather(d_ref, buf_slot):
        def issue(j, _):
            i0 = pl.multiple_of(j * SUBLANES, SUBLANES)
            for s in range(SUBLANES):
                for k in range(TOP_K):
                    src = d_ref[i0 * TOP_K + (s * TOP_K + k)]
                    pltpu.make_async_copy(ys_ref.at[pl.ds(src, 1)], ybuf_ref.at[buf_slot, k, pl.ds(i0 + s, 1)],
                                          sem.at[buf_slot]).start(priority=k % 2)
            return 0

        lax.fori_loop(0, tg // SUBLANES, issue, 0)

    @pl.when(step == 0)
    def _():
        gather(dest_ref, 0)

    def reduce_tile(cur):
        @pl.when(step + 1 < pl.num_programs(0))
        def _():
            gather(dest_next_ref, 1 - cur)

        for k in range(TOP_K):
            pltpu.make_async_copy(ys_ref.at[pl.ds(0, tg)], ybuf_ref.at[cur, k], sem.at[cur]).wait()

        lane = lax.broadcasted_iota(I32, (tg, LANES), 1)
        gates = gate_ref[...]
        h = h1_ref[...]
        for k in range(TOP_K):
            g_k = jnp.sum(jnp.where(lane == k, gates, 0.0), axis=1, keepdims=True)
            h = h + ybuf_ref[cur, k] * g_k
        ms = jnp.mean(h * h, axis=-1, keepdims=True)
        o_ref[...] = (h * lax.rsqrt(ms + RMS_EPS)) * g_ref[...]

    for cur in range(2):
        @pl.when(slot == cur)
        def _(cur=cur):
            reduce_tile(cur)


def _combine(ys, dest_flat, h1, gates, g_final, p):
    tg, d, r = p["tg"], p["d"], p["r"]
    nt = r // tg
    return pl.pallas_call(
        functools.partial(_combine_kernel, p=p),
        grid=(nt,),
        in_specs=[
            pl.BlockSpec((tg * TOP_K,), lambda i: (i,), memory_space=pltpu.SMEM),
            pl.BlockSpec((tg * TOP_K,), lambda i: (jnp.minimum(i + 1, nt - 1),), memory_space=pltpu.SMEM),
            pl.BlockSpec((tg, d), lambda i: (i, 0)),
            pl.BlockSpec((tg, LANES), lambda i: (i, 0)),
            pl.BlockSpec((1, d), lambda i: (0, 0)),
            pl.BlockSpec(memory_space=pl.ANY),
        ],
        out_specs=pl.BlockSpec((tg, d), lambda i: (i, 0)),
        out_shape=jax.ShapeDtypeStruct((r, d), F32),
        scratch_shapes=[pltpu.VMEM((2, TOP_K, tg, d), F32), pltpu.SemaphoreType.DMA((2,))],
        compiler_params=pltpu.CompilerParams(
            dimension_semantics=("arbitrary",),
            vmem_limit_bytes=_vmem_limit(2 * TOP_K * tg * d * 4 + 4 * tg * d * 4 + 6 * tg * d * 4)),
        name="combine",
    )(dest_flat, dest_flat, h1, gates, g_final, ys)


def kernel(x, meta_tokens, norm_mix_g, w_in, conv_w, sink, w_attn_o, w_conv_o, w_out, norm_ffn_g,
           router_w, router_b, w_gate_up, b_gate_up, w_down, b_down, norm_final_g):
    b, seq, d = x.shape
    n_exp, d_ff = w_down.shape[1], w_down.shape[2]
    assert norm_mix_g.shape[0] == 1 and meta_tokens.shape[0] == N_META
    p = _plan(b, seq, d, n_exp, d_ff)
    x2 = x.reshape(p["r"], d)

    qkv, rest, qkvm, restm = _inproj(x2, meta_tokens.astype(F32), norm_mix_g[0].reshape(1, d), w_in[0], p)
    attn_o = _attention(qkv, qkvm, sink[0].astype(F32), p)
    h1, hn2, ti, gates, rank, counts = _mixer(
        attn_o, rest, restm, x2, conv_w[0], w_attn_o[0].astype(BF16), w_conv_o[0].astype(BF16),
        w_out[0].astype(BF16), norm_ffn_g[0].reshape(1, d), router_w[0].astype(BF16),
        router_b[0].reshape(1, n_exp), p)
    row_start, cnt, start, ce, cts, cnt_tiles, used, csel = _routing_tables(counts, p)
    dest = _dest(ti, rank, row_start, p)
    dest_flat = dest[:, :TOP_K].reshape(-1)
    xs = _dispatch(hn2, dest_flat, cnt, start, p)
    ys = _moe(xs, ce, cts, cnt_tiles, used, csel, w_gate_up[0], b_gate_up[0], w_down[0], b_down[0], p)
    out = _combine(ys, dest_flat, h1, gates, norm_final_g.reshape(1, d), p)
    return out.reshape(b, seq, d)
```

```python
import functools

import numpy as np
import jax
import jax.numpy as jnp
from jax import lax
from jax.experimental import pallas as pl
from jax.experimental.pallas import tpu as pltpu

N_META = 16
BLK = 128
WINDOW = 128
HEAD_DIM = 64
Q_PER_KV = 4
ROT_DIM = HEAD_DIM // 4
ROPE_THETA = 500000.0
CONV_K = 3
TOP_K = 4
SWIGLU_ALPHA = 1.702
SWIGLU_LIMIT = 7.0
RMS_EPS = 1e-5

LANES = 128
SUBLANES = 8
VMEM_LIMIT_CAP = 60000 * 1024
MOE_ROW_TILE = 256
MOE_CHUNK_TILES = 5
MOE_W_SLOTS = 3
ATTN_Q_BLOCKS = 2
INPROJ_W_SLOTS = 3
MOE_STEP_DIV = 4
MOE_EXTRA_STEPS = (14, 17, 18)

F32 = jnp.float32
BF16 = jnp.bfloat16
I32 = jnp.int32


def _vmem_limit(nbytes):
    return int(min(VMEM_LIMIT_CAP, max(32 * 1024 * 1024, nbytes * 5 // 4 + (4 << 20))))


def _plan(b, seq, d, n_exp, d_ff):
    attn = (d // 128) * HEAD_DIM
    kvd = attn // Q_PER_KV
    conv = d // 2
    r = b * seq
    p = dict(b=b, seq=seq, d=d, n_exp=n_exp, d_ff=d_ff, attn=attn, kvd=kvd, conv=conv, r=r)
    p["nq"] = attn // HEAD_DIM
    p["nkv"] = kvd // HEAD_DIM
    p["in_dim"] = attn + 2 * kvd + 3 * conv + 2 * d
    p["tn"] = 2 * kvd
    p["tm"] = min(1024, seq)
    p["n_qkv_tiles"] = (attn + 2 * kvd) // p["tn"]
    p["n_col_tiles"] = p["in_dim"] // p["tn"]
    p["rest_w"] = 3 * conv + 2 * d
    p["tc"] = min(256, seq)
    p["te"] = min(512, seq)
    p["tg"] = min(256, seq)
    p["tf"] = min(256, d_ff)
    p["nf"] = d_ff // p["tf"]
    p["rt"] = MOE_ROW_TILE
    p["tpc"] = MOE_CHUNK_TILES
    p["cap"] = MOE_ROW_TILE * MOE_CHUNK_TILES
    p["n_tiles_max"] = (r * TOP_K) // p["rt"] + n_exp
    p["n_chunks_max"] = n_exp + (p["n_tiles_max"] - n_exp) // p["tpc"]
    p["n_slots"] = p["n_tiles_max"] * p["rt"]
    assert seq % p["tm"] == 0 and seq % p["tc"] == 0 and seq % BLK == 0
    assert attn % p["tn"] == 0 and p["in_dim"] % p["tn"] == 0 and kvd % LANES == 0
    assert (r * TOP_K) % p["rt"] == 0 and d_ff % p["tf"] == 0
    assert p["rt"] & (p["rt"] - 1) == 0, "the zero-fill decomposition needs a power-of-two row tile"
    return p


def _rope_tables(seq):
    half = ROT_DIM // 2
    pos = jnp.arange(N_META + seq, dtype=F32)
    inv_freq = ROPE_THETA ** (-jnp.arange(0, ROT_DIM, 2, dtype=F32) / ROT_DIM)
    ang = pos[:, None] * inv_freq[None, :]
    cos, sin = jnp.cos(ang), jnp.sin(ang)
    lane = np.arange(LANES) % HEAD_DIM
    idx = np.where(lane < ROT_DIM, lane % half, 0)
    rot = jnp.asarray(lane < ROT_DIM)
    sign = jnp.asarray(np.where(lane < half, -1.0, 1.0).astype(np.float32))
    cos_l = jnp.where(rot[None, :], cos[:, idx], 1.0)
    sin_l = jnp.where(rot[None, :], sin[:, idx] * sign[None, :], 0.0)
    tab = jnp.stack([cos_l, sin_l]).astype(F32)
    return tab[:, N_META:], tab[:, :N_META]


def _rope(t, cos, sin, n_cols):
    lane = lax.broadcasted_iota(I32, (t.shape[0], LANES), 1)
    first = (lane % HEAD_DIM) < (ROT_DIM // 2)
    outs = []
    for c in range(t.shape[1] // LANES):
        s = t[:, c * LANES:(c + 1) * LANES]
        if c * LANES < n_cols:
            partner = jnp.where(first, pltpu.roll(s, LANES - ROT_DIM // 2, 1), pltpu.roll(s, ROT_DIM // 2, 1))
            s = s * cos + partner * sin
        outs.append(s)
    return jnp.concatenate(outs, axis=1)


def _inproj_kernel(x_ref, meta_ref, g_ref, w_ref, csx_ref, csm_ref,
                   qkv_ref, rest_ref, qkvm_ref, restm_ref,
                   hn_ref, w_ring, obuf_ref, mbuf_ref, wsem, osem, msem, *, p):
    tm, tn, kvd, nn, nqkv, nb = p["tm"], p["tn"], p["kvd"], p["n_col_tiles"], p["n_qkv_tiles"], INPROJ_W_SLOTS
    n_q_tiles = p["attn"] // tn
    m = pl.program_id(0)
    n_w_tiles = pl.num_programs(0) * nn
    row0 = pl.multiple_of(m * tm, tm)

    def w_copy(g):
        c0 = pl.multiple_of(lax.rem(g, nn) * tn, tn)
        slot = lax.rem(g, nb)
        return pltpu.make_async_copy(w_ref.at[:, pl.ds(c0, tn)], w_ring.at[slot], wsem.at[slot])

    def w_start(g):
        @pl.when(g < n_w_tiles)
        def _():
            w_copy(g).start()

    @pl.when(m == 0)
    def _():
        for g in range(nb):
            w_start(g)

    g_row = g_ref[...]

    def norm(v):
        ms = jnp.mean(v * v, axis=-1, keepdims=True)
        return ((v * lax.rsqrt(ms + RMS_EPS)) * g_row).astype(BF16)

    rows = min(128, tm)

    def norm_rows(i, _):
        r0 = pl.multiple_of(i * rows, rows)
        hn_ref[pl.ds(r0, rows), :] = norm(x_ref[pl.ds(r0, rows), :])
        return 0

    lax.fori_loop(0, tm // rows, norm_rows, 0)
    hn_ref[tm:tm + N_META, :] = norm(meta_ref[...])

    def out_copy(dst_ref, col0, slot):
        return pltpu.make_async_copy(obuf_ref.at[slot], dst_ref.at[pl.ds(row0, tm), pl.ds(col0, tn)], osem.at[slot])

    def meta_copy(dst_ref, col0, slot):
        return pltpu.make_async_copy(mbuf_ref.at[slot], dst_ref.at[0, :, pl.ds(col0, tn)], msem.at[slot])

    def wait_slot(slot):
        out_copy(rest_ref, 0, slot).wait()

        @pl.when(m == 0)
        def _():
            meta_copy(restm_ref, 0, slot).wait()

    def column_tile(n, _):
        g = m * nn + n
        wslot = lax.rem(g, nb)
        oslot = lax.rem(n, 2)
        w_copy(g).wait()

        @pl.when(n >= 2)
        def _():
            wait_slot(oslot)

        def project():
            res = jnp.dot(hn_ref[...], w_ring[wslot].astype(BF16), preferred_element_type=F32)
            return res[:tm], res[tm:]

        def emit(rx, rm, dst_ref, dstm_ref, col0):
            obuf_ref[oslot] = rx.astype(BF16)
            mbuf_ref[oslot] = rm.astype(BF16)
            out_copy(dst_ref, col0, oslot).start()

            @pl.when(m == 0)
            def _():
                meta_copy(dstm_ref, col0, oslot).start()

        def store_qkv(n_cols, scale):
            res_x, res_m = project()
            rx = _rope(res_x, csx_ref[0], csx_ref[1], n_cols)
            rm = _rope(res_m, csm_ref[0], csm_ref[1], n_cols)
            if scale != 1.0:
                rx = rx * scale
            emit(rx, rm, qkv_ref, qkvm_ref, pl.multiple_of(n * tn, tn))

        @pl.when(n < n_q_tiles)
        def _():
            store_qkv(tn, HEAD_DIM ** -0.5)

        @pl.when(n == n_q_tiles)
        def _():
            store_qkv(kvd, 1.0)

        @pl.when(n >= nqkv)
        def _():
            res_x, res_m = project()
            emit(res_x, res_m, rest_ref, restm_ref, pl.multiple_of((n - nqkv) * tn, tn))

        w_start(g + nb)
        return 0

    lax.fori_loop(0, nn, column_tile, 0)
    wait_slot(0)
    wait_slot(1)


def _inproj(x2, meta, g, w_in, p):
    tm, tn, d, r = p["tm"], p["tn"], p["d"], p["r"]
    nm, nn, nqkv = r // tm, p["n_col_tiles"], p["n_qkv_tiles"]
    assert nn >= 2
    csx, csm = _rope_tables(p["seq"])
    spt = p["seq"] // tm
    qkv_w = nqkv * tn
    nb = INPROJ_W_SLOTS
    est = (2 * tm * d * 4 + (tm + 16) * d * 2 + nb * d * tn * 4 + 2 * (tm + 16) * tn * 2
           + 4 * 2 * tm * LANES * 4 + 4 * (tm + 16) * tn * 4)
    return pl.pallas_call(
        functools.partial(_inproj_kernel, p=p),
        grid=(nm,),
        in_specs=[
            pl.BlockSpec((tm, d), lambda m: (m, 0)),
            pl.BlockSpec((N_META, d), lambda m: (0, 0)),
            pl.BlockSpec((1, d), lambda m: (0, 0)),
            pl.BlockSpec(memory_space=pl.ANY),
            pl.BlockSpec((2, tm, LANES), lambda m: (0, m % spt, 0)),
            pl.BlockSpec((2, N_META, LANES), lambda m: (0, 0, 0)),
        ],
        out_specs=[pl.BlockSpec(memory_space=pl.ANY)] * 4,
        out_shape=[
            jax.ShapeDtypeStruct((r, qkv_w), BF16),
            jax.ShapeDtypeStruct((r, p["rest_w"]), BF16),
            jax.ShapeDtypeStruct((1, N_META, qkv_w), BF16),
            jax.ShapeDtypeStruct((1, N_META, p["rest_w"]), BF16),
        ],
        scratch_shapes=[
            pltpu.VMEM((tm + N_META, d), BF16), pltpu.VMEM((nb, d, tn), F32),
            pltpu.VMEM((2, tm, tn), BF16), pltpu.VMEM((2, N_META, tn), BF16),
            pltpu.SemaphoreType.DMA((nb,)), pltpu.SemaphoreType.DMA((2,)), pltpu.SemaphoreType.DMA((2,)),
        ],
        compiler_params=pltpu.CompilerParams(
            dimension_semantics=("arbitrary",), vmem_limit_bytes=_vmem_limit(est)),
        name="inproj",
    )(x2, meta, g, w_in, csx, csm)


def _attn_kernel(sink_ref, q_ref, k0_ref, k1_ref, k2_ref, k3_ref, v0_ref, v1_ref, v2_ref, v3_ref, km_ref, vm_ref,
                 o_ref, kcat_ref, vcat_ref, *, p):
    k_refs = (k0_ref, k1_ref, k2_ref, k3_ref)
    v_refs = (v0_ref, v1_ref, v2_ref, v3_ref)
    for sub in range(ATTN_Q_BLOCKS):
        _attn_block(sink_ref, q_ref.at[sub * BLK:(sub + 1) * BLK], k_refs[sub:sub + 3], v_refs[sub:sub + 3],
                    km_ref, vm_ref, o_ref.at[sub * BLK:(sub + 1) * BLK], kcat_ref.at[sub], vcat_ref.at[sub],
                    ATTN_Q_BLOCKS * pl.program_id(1) + sub, p)


def _attn_block(sink_ref, q_ref, k_refs, v_refs, km_ref, vm_ref, o_ref, kcat_ref, vcat_ref, n, p):
    seq, nkv = p["seq"], p["nkv"]
    nband = 3 * BLK
    nkeys = nband + N_META
    for j, (kr, vr) in enumerate(zip(k_refs, v_refs)):
        kcat_ref[j * BLK:(j + 1) * BLK, :] = kr[...]
        vcat_ref[j * BLK:(j + 1) * BLK, :] = vr[...]
    kcat_ref[nband:nkeys, :] = km_ref[...]
    vcat_ref[nband:nkeys, :] = vm_ref[...]

    qi = lax.broadcasted_iota(I32, (BLK, nkeys), 0)
    sj = lax.broadcasted_iota(I32, (BLK, nkeys), 1)
    kx = (n - 1) * BLK + sj
    dq = n * BLK + qi - kx
    visible = (sj >= nband) | ((jnp.abs(dq) <= WINDOW) & (kx >= 0) & (kx < seq))
    neg = jnp.finfo(F32).min
    ones = jnp.ones((nkeys, HEAD_DIM), BF16)
    gi = lax.broadcasted_iota(I32, (Q_PER_KV, 1, 1), 0)
    for h in range(nkv):
        kh = kcat_ref[:, h * HEAD_DIM:(h + 1) * HEAD_DIM]
        vh = jnp.concatenate([vcat_ref[:, h * HEAD_DIM:(h + 1) * HEAD_DIM], ones], axis=1)
        heads = [h * Q_PER_KV + g for g in range(Q_PER_KV)]
        qs = jnp.concatenate([q_ref[:, hd * HEAD_DIM:(hd + 1) * HEAD_DIM] for hd in heads], axis=0)
        s = lax.dot_general(qs, kh, (((1,), (1,)), ((), ())), preferred_element_type=F32)
        s = jnp.where(visible[None], s.reshape(Q_PER_KV, BLK, nkeys), neg)
        snk = jnp.zeros((Q_PER_KV, 1, 1), F32)
        for g, hd in enumerate(heads):
            snk = jnp.where(gi == g, sink_ref[hd], snk)
        m = jnp.maximum(jnp.max(s, axis=2, keepdims=True), snk)
        e = jnp.exp(s - m).reshape(Q_PER_KV * BLK, nkeys)
        pv = jnp.dot(e.astype(BF16), vh, preferred_element_type=F32)
        pv = pv.reshape(Q_PER_KV, BLK, 2 * HEAD_DIM)
        o = pv[:, :, :HEAD_DIM] / (pv[:, :, HEAD_DIM:HEAD_DIM + 1] + jnp.exp(snk - m))
        for g, hd in enumerate(heads):
            o_ref[:, hd * HEAD_DIM:(hd + 1) * HEAD_DIM] = o[g].astype(BF16)


def _attention(qkv, qkvm, sink, p):
    attn, kvd, seq, b = p["attn"], p["kvd"], p["seq"], p["b"]
    nbx = seq // BLK
    kc = attn // kvd
    nkeys = 3 * BLK + N_META
    qb = ATTN_Q_BLOCKS
    assert nbx % qb == 0
    npair = nbx // qb

    def kv_spec(off, col):
        return pl.BlockSpec((BLK, kvd), lambda bi, j: (bi * nbx + jnp.clip(qb * j + off, 0, nbx - 1), col))

    offs = range(-1, qb + 1)
    return pl.pallas_call(
        functools.partial(_attn_kernel, p=p),
        grid=(b, npair),
        in_specs=[
            pl.BlockSpec(memory_space=pltpu.SMEM),
            pl.BlockSpec((qb * BLK, attn), lambda bi, j: (bi * npair + j, 0)),
            *[kv_spec(off, kc) for off in offs],
            *[kv_spec(off, kc + 1) for off in offs],
            pl.BlockSpec((None, N_META, kvd), lambda bi, j: (0, 0, kc)),
            pl.BlockSpec((None, N_META, kvd), lambda bi, j: (0, 0, kc + 1)),
        ],
        out_specs=pl.BlockSpec((qb * BLK, attn), lambda bi, j: (bi * npair + j, 0)),
        out_shape=jax.ShapeDtypeStruct((p["r"], attn), BF16),
        scratch_shapes=[pltpu.VMEM((qb, nkeys, kvd), BF16), pltpu.VMEM((qb, nkeys, kvd), BF16)],
        compiler_params=pltpu.CompilerParams(dimension_semantics=("arbitrary", "arbitrary")),
        name="attn",
    )(sink, qkv, *([qkv] * (2 * (qb + 2))), qkvm, qkvm)


def _lane_pack(cols, rows, dtype):
    lane = lax.broadcasted_iota(I32, (rows, LANES), 1)
    out = jnp.zeros((rows, LANES), dtype)
    for k, c in enumerate(cols):
        out = jnp.where(lane == k, c.astype(dtype), out)
    return out


def _mixer_kernel(attn_ref, rest_ref, prev_ref, next_ref, restm_ref, x_ref, cw_ref, wao_ref, wco_ref,
                  wout_ref, g_ref, rw_ref, rb_ref,
                  h1_ref, hn2_ref, ti_ref, gate_ref, rank_ref, cnt_ref, carry_ref, *, p):
    tc, conv, d, n_exp, seq = p["tc"], p["conv"], p["d"], p["n_exp"], p["seq"]
    i = pl.program_id(0)
    tiles_per_seq = seq // tc
    is_first = (i % tiles_per_seq) == 0
    is_last = (i % tiles_per_seq) == tiles_per_seq - 1
    o_ch, o_cb, o_cc, o_ga, o_gc = 0, conv, 2 * conv, 3 * conv, 3 * conv + d

    @pl.when(i == 0)
    def _():
        carry_ref[...] = jnp.zeros_like(carry_ref)

    def u_of(ref):
        return ref[:, o_cc:o_cc + conv].astype(F32) * ref[:, o_ch:o_ch + conv].astype(F32)

    u = u_of(rest_ref)
    last = N_META - 1
    u_prev = jnp.where(is_first, u_of(restm_ref)[last:last + 1], u_of(prev_ref)[last:last + 1])
    u_next = jnp.where(is_last, 0.0, u_of(next_ref)[0:1])
    row = lax.broadcasted_iota(I32, (tc, conv), 0)
    u_m1 = jnp.where(row == 0, u_prev, pltpu.roll(u, 1, 0))
    u_p1 = jnp.where(row == tc - 1, u_next, pltpu.roll(u, tc - 1, 0))
    cw = cw_ref[...]
    cv = u_m1 * cw[0:1] + u * cw[1:2] + u_p1 * cw[2:3]
    yc_in = (rest_ref[:, o_cb:o_cb + conv].astype(F32) * cv).astype(BF16)
    y_conv = jnp.dot(yc_in, wco_ref[...], preferred_element_type=F32)
    y_attn = jnp.dot(attn_ref[...], wao_ref[...], preferred_element_type=F32)
    g_a = rest_ref[:, o_ga:o_ga + d].astype(F32)
    g_c = rest_ref[:, o_gc:o_gc + d].astype(F32)
    merged = jax.nn.sigmoid(g_a) * y_attn + jax.nn.sigmoid(g_c) * y_conv
    h1 = x_ref[...] + jnp.dot(merged.astype(BF16), wout_ref[...], preferred_element_type=F32)
    h1_ref[...] = h1
    ms = jnp.mean(h1 * h1, axis=-1, keepdims=True)
    hn2 = (h1 * lax.rsqrt(ms + RMS_EPS)) * g_ref[...]
    hn2_ref[...] = hn2

    logits = jnp.dot(hn2.astype(BF16), rw_ref[...], preferred_element_type=F32) + rb_ref[...]
    lane = lax.broadcasted_iota(I32, (tc, n_exp), 1).astype(F32)
    sel = jnp.zeros((tc, n_exp), F32)
    tv, ti = [], []
    cur = logits
    for _ in range(TOP_K):
        m = jnp.max(cur, axis=1, keepdims=True)
        idx = jnp.min(jnp.where(cur == m, lane, float(n_exp)), axis=1, keepdims=True)
        hit = lane == idx
        tv.append(m)
        ti.append(idx)
        sel = jnp.where(hit, 1.0, sel)
        cur = jnp.where(hit, -jnp.inf, cur)
    ex = [jnp.exp(v - tv[0]) for v in tv]
    tot = ex[0] + ex[1] + ex[2] + ex[3]
    gates = [e / tot for e in ex]

    r_i = lax.broadcasted_iota(I32, (tc, tc), 0)
    c_i = lax.broadcasted_iota(I32, (tc, tc), 1)
    lower = jnp.where(r_i > c_i, 1.0, 0.0).astype(BF16)
    before = jnp.dot(lower, sel.astype(BF16), preferred_element_type=F32) + carry_ref[0:1, 0:n_exp]
    ranks = [jnp.sum(jnp.where(lane == t, before, 0.0), axis=1, keepdims=True) for t in ti]
    carry_ref[0:1, 0:n_exp] = carry_ref[0:1, 0:n_exp] + jnp.sum(sel, axis=0, keepdims=True)

    ti_ref[...] = _lane_pack(ti, tc, I32)
    gate_ref[...] = _lane_pack(gates, tc, F32)
    rank_ref[...] = _lane_pack(ranks, tc, I32)
    cnt_ref[...] = carry_ref[...]


def _mixer(attn_o, rest, restm, x2, conv_w, wao, wco, wout, g_ffn, router_w, router_b, p):
    tc, d, r, rw, n_exp = p["tc"], p["d"], p["r"], p["rest_w"], p["n_exp"]
    nt = r // tc
    sub = tc // N_META
    n16 = r // N_META
    const = lambda shape: pl.BlockSpec(shape, lambda i: (0,) * len(shape))
    est = (2 * (tc * p["attn"] * 2 + tc * rw * 2 + tc * d * 4 + 3 * N_META * rw * 2)
           + 2 * (p["attn"] * d + p["conv"] * d + d * d + d * n_exp) * 2
           + 2 * (2 * tc * d * 4 + 3 * tc * LANES * 4) + 10 * tc * d * 4)
    return pl.pallas_call(
        functools.partial(_mixer_kernel, p=p),
        grid=(nt,),
        in_specs=[
            pl.BlockSpec((tc, p["attn"]), lambda i: (i, 0)),
            pl.BlockSpec((tc, rw), lambda i: (i, 0)),
            pl.BlockSpec((N_META, rw), lambda i: (jnp.maximum(i * sub - 1, 0), 0)),
            pl.BlockSpec((N_META, rw), lambda i: (jnp.minimum((i + 1) * sub, n16 - 1), 0)),
            pl.BlockSpec((None, N_META, rw), lambda i: (0, 0, 0)),
            pl.BlockSpec((tc, d), lambda i: (i, 0)),
            const((CONV_K, p["conv"])),
            const((p["attn"], d)), const((p["conv"], d)), const((d, d)),
            const((1, d)), const((d, n_exp)), const((1, n_exp)),
        ],
        out_specs=[
            pl.BlockSpec((tc, d), lambda i: (i, 0)),
            pl.BlockSpec((tc, d), lambda i: (i, 0)),
            pl.BlockSpec((tc, LANES), lambda i: (i, 0)),
            pl.BlockSpec((tc, LANES), lambda i: (i, 0)),
            pl.BlockSpec((tc, LANES), lambda i: (i, 0)),
            pl.BlockSpec((8, LANES), lambda i: (0, 0)),
        ],
        out_shape=[
            jax.ShapeDtypeStruct((r, d), F32),
            jax.ShapeDtypeStruct((r, d), F32),
            jax.ShapeDtypeStruct((r, LANES), I32),
            jax.ShapeDtypeStruct((r, LANES), F32),
            jax.ShapeDtypeStruct((r, LANES), I32),
            jax.ShapeDtypeStruct((8, LANES), F32),
        ],
        scratch_shapes=[pltpu.VMEM((8, LANES), F32)],
        compiler_params=pltpu.CompilerParams(
            dimension_semantics=("arbitrary",), vmem_limit_bytes=_vmem_limit(est)),
        name="mixer",
    )(attn_o, rest, rest, rest, restm, x2, conv_w, wao, wco, wout, g_ffn, router_w, router_b)


def _dest_kernel(ti_ref, rank_ref, start_ref, dest_ref):
    rows = ti_ref.shape[0]
    lane = lax.broadcasted_iota(I32, (rows, LANES), 1).astype(F32)
    ti = ti_ref[...].astype(F32)
    rank = rank_ref[...].astype(F32)
    start = start_ref[0:1, :].astype(F32)
    out = jnp.zeros((rows, LANES), F32)
    for k in range(TOP_K):
        e_k = jnp.sum(jnp.where(lane == k, ti, 0.0), axis=1, keepdims=True)
        r_k = jnp.sum(jnp.where(lane == k, rank, 0.0), axis=1, keepdims=True)
        s_k = jnp.sum(jnp.where(lane == e_k, start, 0.0), axis=1, keepdims=True)
        out = jnp.where(lane == k, s_k + r_k, out)
    dest_ref[...] = out.astype(I32)


def _dest(ti, rank, row_start, p):
    r = p["r"]
    rows = min(2048, r)
    return pl.pallas_call(
        _dest_kernel,
        grid=(r // rows,),
        in_specs=[pl.BlockSpec((rows, LANES), lambda i: (i, 0)),
                  pl.BlockSpec((rows, LANES), lambda i: (i, 0)),
                  pl.BlockSpec((8, LANES), lambda i: (0, 0))],
        out_specs=pl.BlockSpec((rows, LANES), lambda i: (i, 0)),
        out_shape=jax.ShapeDtypeStruct((r, LANES), I32),
        name="dest",
    )(ti, rank, row_start)


def _moe_row_steps(tpc):
    whole = range(MOE_STEP_DIV, MOE_STEP_DIV * tpc + 1, MOE_STEP_DIV)
    return sorted(set(whole) | {h for h in MOE_EXTRA_STEPS if h < MOE_STEP_DIV * tpc})


def _routing_tables(counts, p):
    rt, tpc, n_exp, nch = p["rt"], p["tpc"], p["n_exp"], p["n_chunks_max"]
    cnt = counts[0, :n_exp].astype(I32)
    ntile = (cnt + rt - 1) // rt
    tile_start = jnp.cumsum(ntile) - ntile
    nchunk = (ntile + tpc - 1) // tpc
    chunk_end = jnp.cumsum(nchunk)
    c = jnp.arange(nch, dtype=I32)
    ce = jnp.minimum(jnp.sum((chunk_end[None, :] <= c[:, None]).astype(I32), axis=1), n_exp - 1)
    first = c - (chunk_end - nchunk)[ce]
    c_nt = jnp.clip(ntile[ce] - first * tpc, 0, tpc)
    c_nt = jnp.where(c < chunk_end[-1], c_nt, 0)
    c_ts = tile_start[ce] + first * tpc
    last_e = ce[jnp.maximum(chunk_end[-1] - 1, 0)]
    ce = jnp.where(c_nt > 0, ce, last_e)
    row_start = jnp.zeros((8, LANES), I32).at[0, :n_exp].set(tile_start * rt)
    used = jnp.stack([jnp.sum(ntile), chunk_end[-1]]).astype(I32)
    unit = rt // MOE_STEP_DIV
    valid = jnp.clip(cnt[ce] - first * (tpc * rt), 0, c_nt * rt)
    halves = (valid + unit - 1) // unit
    steps = jnp.asarray(_moe_row_steps(tpc), I32)
    first_ge = jnp.sum((steps[None, :] < halves[:, None]).astype(I32), axis=1)
    c_sel = steps[jnp.minimum(first_ge, steps.shape[0] - 1)]
    c_sel = jnp.where(c_nt > 0, c_sel, 0).astype(I32)
    return row_start, cnt, tile_start * rt, ce.astype(I32), c_ts.astype(I32), c_nt.astype(I32), used, c_sel


def _dispatch_kernel(cnt_ref, start_ref, dest_ref, hn_ref, xs_ref, zero_ref, sem, zsem, *, p):
    te, n_exp, rt, d = p["te"], p["n_exp"], p["rt"], p["d"]
    step = pl.program_id(0)

    @pl.when(step == 0)
    def _():
        zero_ref[...] = jnp.zeros_like(zero_ref)

        def per_expert(e, wait):
            def zero_rows(dst0, size):
                cp = pltpu.make_async_copy(zero_ref.at[pl.ds(0, size)], xs_ref.at[pl.ds(dst0, size)], zsem)
                cp.wait() if wait else cp.start()

            cnt = cnt_ref[e]
            cur = start_ref[e] + cnt
            pad = (rt - (cnt & (rt - 1))) & (rt - 1)
            head = (SUBLANES - (cur & (SUBLANES - 1))) & (SUBLANES - 1)
            for j in range(SUBLANES - 1):
                @pl.when(j < head)
                def _(j=j):
                    zero_rows(cur + j, 1)
            cur = cur + head
            rem = pad - head
            size = SUBLANES
            while size < rt:
                @pl.when((rem & size) != 0)
                def _(cur=cur, size=size):
                    zero_rows(pl.multiple_of(cur, SUBLANES), size)
                cur = cur + (rem & size)
                size *= 2

        lax.fori_loop(0, n_exp, lambda e, _: (per_expert(e, False), 0)[1], 0)
        lax.fori_loop(0, n_exp, lambda e, _: (per_expert(e, True), 0)[1], 0)

        half = rt // 2
        used = (start_ref[n_exp - 1] + cnt_ref[n_exp - 1] + rt - 1) // rt

        def tail_copy(t, j):
            r0 = pl.multiple_of(t * rt + j * half, half)
            return pltpu.make_async_copy(zero_ref, xs_ref.at[pl.ds(r0, half)], zsem)

        def tail_start(t, _):
            tail_copy(t, 0).start()
            tail_copy(t, 1).start()
            return 0

        def tail_wait(t, _):
            tail_copy(t, 0).wait()
            tail_copy(t, 1).wait()
            return 0

        lax.fori_loop(used, p["n_tiles_max"], tail_start, 0)
        lax.fori_loop(used, p["n_tiles_max"], tail_wait, 0)

    def row_copy(i, k):
        return pltpu.make_async_copy(hn_ref.at[pl.ds(i, 1)], xs_ref.at[pl.ds(dest_ref[i * TOP_K + k], 1)], sem)

    def issue(i, _):
        for k in range(TOP_K):
            row_copy(i, k).start(priority=k % 2)
        return 0

    lax.fori_loop(0, te, issue, 0)

    for k in range(TOP_K):
        pltpu.make_async_copy(hn_ref, xs_ref.at[pl.ds(0, te)], sem).wait()


def _dispatch(hn2, dest_flat, cnt, start, p):
    te, d, r = p["te"], p["d"], p["r"]
    return pl.pallas_call(
        functools.partial(_dispatch_kernel, p=p),
        grid=(r // te,),
        in_specs=[
            pl.BlockSpec(memory_space=pltpu.SMEM),
            pl.BlockSpec(memory_space=pltpu.SMEM),
            pl.BlockSpec((te * TOP_K,), lambda i: (i,), memory_space=pltpu.SMEM),
            pl.BlockSpec((te, d), lambda i: (i, 0)),
        ],
        out_specs=pl.BlockSpec(memory_space=pl.ANY),
        out_shape=jax.ShapeDtypeStruct((p["n_slots"], d), F32),
        scratch_shapes=[pltpu.VMEM((p["rt"] // 2, d), F32), pltpu.SemaphoreType.DMA, pltpu.SemaphoreType.DMA],
        compiler_params=pltpu.CompilerParams(dimension_semantics=("arbitrary",)),
        name="dispatch",
    )(cnt, start, dest_flat, hn2)


def _moe_kernel(ce_ref, cts_ref, cnt_ref, used_ref, csel_ref, xs_ref, wgu_ref, bgu_ref, wd_ref, bd_ref,
                ys_ref, xbuf_ref, yacc_ref, stage_ref, wg_ring, wu_ring, wd_ring, xsem, ysem, wsem, *, p):
    rt, nf, nch, tf, d_ff, nb = p["rt"], p["nf"], p["n_chunks_max"], p["tf"], p["d_ff"], MOE_W_SLOTS
    c = pl.program_id(0)
    nt = cnt_ref[c]
    c_next = jnp.minimum(c + 1, nch - 1)
    nt_next = jnp.where(c + 1 < nch, cnt_ref[c_next], 0)
    n_w_tiles = used_ref[1] * nf

    def w_copies(g):
        chunk = lax.div(g, nf)
        fg = lax.rem(g, nf)
        e = ce_ref[chunk]
        slot = lax.rem(g, nb)
        c0 = pl.multiple_of(fg * tf, tf)
        return (
            pltpu.make_async_copy(wgu_ref.at[e, :, pl.ds(c0, tf)], wg_ring.at[slot], wsem.at[slot]),
            pltpu.make_async_copy(wgu_ref.at[e, :, pl.ds(d_ff + c0, tf)], wu_ring.at[slot], wsem.at[slot]),
            pltpu.make_async_copy(wd_ref.at[e, pl.ds(c0, tf), :], wd_ring.at[slot], wsem.at[slot]),
        )

    def w_start(g):
        @pl.when(g < n_w_tiles)
        def _():
            for cp in w_copies(g):
                cp.start()

    acc = lax.rem(c, 2)

    def x_copy(chunk, t):
        src0 = pl.multiple_of((cts_ref[chunk] + t) * rt, rt)
        return pltpu.make_async_copy(xs_ref.at[pl.ds(src0, rt)], stage_ref.at[t], xsem.at[t])

    def y_copy(chunk, t):
        a = lax.rem(chunk, 2)
        r0 = pl.multiple_of(t * rt, rt)
        dst0 = pl.multiple_of((cts_ref[chunk] + t) * rt, rt)
        return pltpu.make_async_copy(yacc_ref.at[a, pl.ds(r0, rt)], ys_ref.at[pl.ds(dst0, rt)], ysem.at[a])

    def x_start_all(chunk, n_tiles):
        for t in range(p["tpc"]):
            @pl.when(t < n_tiles)
            def _(t=t):
                x_copy(chunk, t).start()

    @pl.when(nt > 0)
    def _():
        @pl.when(c == 0)
        def _():
            for g in range(nb):
                w_start(g)
            x_start_all(c, nt)
            yacc_ref[...] = jnp.zeros_like(yacc_ref)

        def load(t, _):
            x_copy(c, t).wait()
            r0 = pl.multiple_of(t * rt, rt)
            xbuf_ref[pl.ds(r0, rt), :] = stage_ref[t].astype(BF16)
            return 0

        lax.fori_loop(0, nt, load, 0)

        @pl.when(nt_next > 0)
        def _():
            x_start_all(c_next, nt_next)

        bd = bd_ref[0]

        def hidden_tile(f, _):
            g = c * nf + f
            slot = lax.rem(g, nb)
            for cp in w_copies(g):
                cp.wait()

            bg = bgu_ref[0, pl.ds(f, 1), :]
            bu = bgu_ref[0, pl.ds(nf + f, 1), :]

            def ffn(rows):
                xt = xbuf_ref[0:rows, :]
                gate = jnp.dot(xt, wg_ring[slot].astype(BF16), preferred_element_type=F32) + bg
                up = jnp.dot(xt, wu_ring[slot].astype(BF16), preferred_element_type=F32) + bu
                gate = jnp.minimum(gate, SWIGLU_LIMIT)
                up = jnp.clip(up, -SWIGLU_LIMIT, SWIGLU_LIMIT)
                act = (up + 1.0) * gate * jax.nn.sigmoid(SWIGLU_ALPHA * gate)
                base = jnp.where(f == 0, bd, yacc_ref[acc, 0:rows, :])
                yacc_ref[acc, 0:rows, :] = base + jnp.dot(act.astype(BF16), wd_ring[slot].astype(BF16),
                                                          preferred_element_type=F32)

            for hs in _moe_row_steps(p["tpc"]):
                @pl.when(csel_ref[c] == hs)
                def _(hs=hs):
                    ffn(hs * (rt // MOE_STEP_DIV))

            w_start(g + nb)
            return 0

        lax.fori_loop(0, nf, hidden_tile, 0)

        def drain(chunk):
            def body(t, _):
                y_copy(chunk, t).wait()
                return 0

            lax.fori_loop(0, cnt_ref[chunk], body, 0)

        @pl.when(c > 0)
        def _():
            drain(jnp.maximum(c - 1, 0))

        def store(t, _):
            y_copy(c, t).start()
            return 0

        lax.fori_loop(0, nt, store, 0)

        @pl.when(nt_next == 0)
        def _():
            drain(c)

    @pl.when(c == nch - 1)
    def _():
        stage_ref[0] = jnp.zeros((rt, p["d"]), F32)

        def tail_copy(t):
            return pltpu.make_async_copy(stage_ref.at[0], ys_ref.at[pl.ds(pl.multiple_of(t * rt, rt), rt)],
                                         ysem.at[0])

        def tail_start(t, _):
            tail_copy(t).start()
            return 0

        def tail_wait(t, _):
            tail_copy(t).wait()
            return 0

        lax.fori_loop(used_ref[0], p["n_tiles_max"], tail_start, 0)
        lax.fori_loop(used_ref[0], p["n_tiles_max"], tail_wait, 0)


def _moe(xs, ce, cts, cnt, used, csel, w_gate_up, b_gate_up, w_down, b_down, p):
    d, tf, nf, rt, cap = p["d"], p["tf"], p["nf"], p["rt"], p["cap"]
    n_exp = p["n_exp"]
    nb = MOE_W_SLOTS
    bgu = b_gate_up.reshape(n_exp, 2 * nf, tf)
    bdn = b_down.reshape(n_exp, 1, d)

    est = (nb * 3 * d * tf * 4 + cap * d * 2 + 2 * cap * d * 4 + cap * d * 4 + 4 * rt * d * 4)
    grid_spec = pltpu.PrefetchScalarGridSpec(
        num_scalar_prefetch=5,
        grid=(p["n_chunks_max"],),
        in_specs=[
            pl.BlockSpec(memory_space=pl.ANY),
            pl.BlockSpec(memory_space=pl.ANY),
            pl.BlockSpec((1, 2 * nf, tf), lambda c, ce_r, cts_r, cnt_r, u_r, s_r: (ce_r[c], 0, 0)),
            pl.BlockSpec(memory_space=pl.ANY),
            pl.BlockSpec((1, 1, d), lambda c, ce_r, cts_r, cnt_r, u_r, s_r: (ce_r[c], 0, 0)),
        ],
        out_specs=pl.BlockSpec(memory_space=pl.ANY),
        scratch_shapes=[
            pltpu.VMEM((cap, d), BF16), pltpu.VMEM((2, cap, d), F32), pltpu.VMEM((p["tpc"], rt, d), F32),
            pltpu.VMEM((nb, d, tf), F32), pltpu.VMEM((nb, d, tf), F32), pltpu.VMEM((nb, tf, d), F32),
            pltpu.SemaphoreType.DMA((p["tpc"],)), pltpu.SemaphoreType.DMA((2,)), pltpu.SemaphoreType.DMA((nb,)),
        ],
    )
    return pl.pallas_call(
        functools.partial(_moe_kernel, p=p),
        grid_spec=grid_spec,
        out_shape=jax.ShapeDtypeStruct((p["n_slots"], d), F32),
        compiler_params=pltpu.CompilerParams(
            dimension_semantics=("arbitrary",), vmem_limit_bytes=_vmem_limit(est)),
        name="moe",
    )(ce, cts, cnt, used, csel, xs, w_gate_up, bgu, w_down, bdn)


def _combine_kernel(dest_ref, dest_next_ref, h1_ref, gate_ref, g_ref, ys_ref, o_ref, ybuf_ref, sem, *, p):
    tg = p["tg"]
    step = pl.program_id(0)
    slot = step % 2

    def gather(d_ref, buf_slot):
        def issue(j, _):
            i0 = pl.multiple_of(j * SUBLANES, SUBLANES)
            for s in range(SUBLANES):
                for k in range(TOP_K):
                    src = d_ref[i0 * TOP_K + (s * TOP_K + k)]
                    pltpu.make_async_copy(ys_ref.at[pl.ds(src, 1)], ybuf_ref.at[buf_slot, k, pl.ds(i0 + s, 1)],
                                          sem.at[buf_slot]).start(priority=k % 2)
            return 0

        lax.fori_loop(0, tg // SUBLANES, issue, 0)

    @pl.when(step == 0)
    def _():
        gather(dest_ref, 0)

    def reduce_tile(cur):
        @pl.when(step + 1 < pl.num_programs(0))
        def _():
            gather(dest_next_ref, 1 - cur)

        for k in range(TOP_K):
            pltpu.make_async_copy(ys_ref.at[pl.ds(0, tg)], ybuf_ref.at[cur, k], sem.at[cur]).wait()

        lane = lax.broadcasted_iota(I32, (tg, LANES), 1)
        gates = gate_ref[...]
        h = h1_ref[...]
        for k in range(TOP_K):
            g_k = jnp.sum(jnp.where(lane == k, gates, 0.0), axis=1, keepdims=True)
            h = h + ybuf_ref[cur, k] * g_k
        ms = jnp.mean(h * h, axis=-1, keepdims=True)
        o_ref[...] = (h * lax.rsqrt(ms + RMS_EPS)) * g_ref[...]

    for cur in range(2):
        @pl.when(slot == cur)
        def _(cur=cur):
            reduce_tile(cur)


def _combine(ys, dest_flat, h1, gates, g_final, p):
    tg, d, r = p["tg"], p["d"], p["r"]
    nt = r // tg
    return pl.pallas_call(
        functools.partial(_combine_kernel, p=p),
        grid=(nt,),
        in_specs=[
            pl.BlockSpec((tg * TOP_K,), lambda i: (i,), memory_space=pltpu.SMEM),
            pl.BlockSpec((tg * TOP_K,), lambda i: (jnp.minimum(i + 1, nt - 1),), memory_space=pltpu.SMEM),
            pl.BlockSpec((tg, d), lambda i: (i, 0)),
            pl.BlockSpec((tg, LANES), lambda i: (i, 0)),
            pl.BlockSpec((1, d), lambda i: (0, 0)),
            pl.BlockSpec(memory_space=pl.ANY),
        ],
        out_specs=pl.BlockSpec((tg, d), lambda i: (i, 0)),
        out_shape=jax.ShapeDtypeStruct((r, d), F32),
        scratch_shapes=[pltpu.VMEM((2, TOP_K, tg, d), F32), pltpu.SemaphoreType.DMA((2,))],
        compiler_params=pltpu.CompilerParams(
            dimension_semantics=("arbitrary",),
            vmem_limit_bytes=_vmem_limit(2 * TOP_K * tg * d * 4 + 4 * tg * d * 4 + 6 * tg * d * 4)),
        name="combine",
    )(dest_flat, dest_flat, h1, gates, g_final, ys)


def kernel(x, meta_tokens, norm_mix_g, w_in, conv_w, sink, w_attn_o, w_conv_o, w_out, norm_ffn_g,
           router_w, router_b, w_gate_up, b_gate_up, w_down, b_down, norm_final_g):
    b, seq, d = x.shape
    n_exp, d_ff = w_down.shape[1], w_down.shape[2]
    assert norm_mix_g.shape[0] == 1 and meta_tokens.shape[0] == N_META
    p = _plan(b, seq, d, n_exp, d_ff)
    x2 = x.reshape(p["r"], d)

    qkv, rest, qkvm, restm = _inproj(x2, meta_tokens.astype(F32), norm_mix_g[0].reshape(1, d), w_in[0], p)
    attn_o = _attention(qkv, qkvm, sink[0].astype(F32), p)
    h1, hn2, ti, gates, rank, counts = _mixer(
        attn_o, rest, restm, x2, conv_w[0], w_attn_o[0].astype(BF16), w_conv_o[0].astype(BF16),
        w_out[0].astype(BF16), norm_ffn_g[0].reshape(1, d), router_w[0].astype(BF16),
        router_b[0].reshape(1, n_exp), p)
    row_start, cnt, start, ce, cts, cnt_tiles, used, csel = _routing_tables(counts, p)
    dest = _dest(ti, rank, row_start, p)
    dest_flat = dest[:, :TOP_K].reshape(-1)
    xs = _dispatch(hn2, dest_flat, cnt, start, p)
    ys = _moe(xs, ce, cts, cnt_tiles, used, csel, w_gate_up[0], b_gate_up[0], w_down[0], b_down[0], p)
    out = _combine(ys, dest_flat, h1, gates, norm_final_g.reshape(1, d), p)
    return out.reshape(b, seq, d)
```
